```python
import math
import jax, jax.numpy as jnp
from jax import lax
import numpy as np

D_MODEL = 2048
BATCH = 2
SEQ = 4096
DEPTH = 2
DEC_BATCH = 8
DEC_SEQ = 32
PAST_LEN = 4096

CHUNK = 64
N_EVEN = (DEPTH + 1) // 2
N_ODD = DEPTH // 2
EPS = 1e-6
GLA_HEADS = 4
GLA_DK = D_MODEL // 16
GLA_DV = D_MODEL // 8
GLA_GATE_RANK = 16
GLA_TAU = 16.0
GLA_QK = GLA_HEADS * GLA_DK
GLA_VW = GLA_HEADS * GLA_DV
SSD_HEADDIM = 64
SSD_INNER = D_MODEL // 2
SSD_HEADS = SSD_INNER // SSD_HEADDIM
SSD_STATE = 128
SSD_GROUPS = 4
SSD_CONV = 4
SSD_CONV_DIM = SSD_INNER + 2 * SSD_GROUPS * SSD_STATE
MIX_WIDTH = GLA_VW + SSD_INNER
IN_SIZES = (GLA_QK, GLA_QK, GLA_VW, GLA_GATE_RANK, GLA_VW, SSD_INNER, SSD_CONV_DIM, SSD_HEADS)
IN_DIM = 2 * GLA_QK + 2 * GLA_VW + GLA_GATE_RANK + SSD_INNER + SSD_CONV_DIM + SSD_HEADS
SCAN_BLOCK = 16
DIFF_HEADS = 8
DIFF_DH = 128
ATT_WIDTH = DIFF_HEADS * 2 * DIFF_DH
Q_BLOCK = 128
MOE_GROUPS = 4
MOE_PER_GROUP = 8
N_EXPERTS = MOE_GROUPS * MOE_PER_GROUP
MOE_TOP_K = 2
D_EXPERT = D_MODEL // 4
MOE_BLOCK = 64

kernel_name = 'hybrid_gla_ssd_diffattn_hmoe_stream_step'


def rms_norm(x, g):
    xf = x.astype(jnp.float32)
    y = xf * lax.rsqrt(jnp.mean(xf * xf, axis=-1, keepdims=True) + EPS)
    return (y * g.astype(jnp.float32)).astype(x.dtype)


def split_cols(a, sizes):
    idx, acc = [], 0
    for s in sizes[:-1]:
        acc += s
        idx.append(acc)
    return jnp.split(a, idx, axis=-1)


def gated_linear_recurrence(q, k, v, log_a, s0):
    B, T, H, K = q.shape
    V = v.shape[-1]
    G = log_a.shape[-1]
    L = SCAN_BLOCK
    pad = (-T) % L
    n = (T + pad) // L

    def blocks(a):
        a = jnp.pad(a.astype(jnp.float32), ((0, 0), (0, pad), (0, 0), (0, 0)))
        return a.reshape(B, n, L, H, a.shape[-1]).transpose(1, 0, 3, 2, 4)

    qb, kb, vb, ab = blocks(q), blocks(k), blocks(v), blocks(log_a)
    b = jnp.cumsum(ab, axis=3)
    b_last = b[:, :, :, -1:, :]
    q_in = qb * jnp.exp(b)
    k_st = kb * jnp.exp(b_last - b)
    causal = jnp.tril(jnp.ones((L, L), dtype=bool))
    if G == 1:
        seg = b[..., :, None, 0] - b[..., None, :, 0]
        decay = jnp.where(causal, jnp.exp(jnp.where(causal, seg, 0.0)), 0.0)
        scores = jnp.einsum('nbhik,nbhjk->nbhij', qb, kb) * decay
    else:
        scores = jnp.where(causal, jnp.einsum('nbhik,nbhjk->nbhij', q_in, kb * jnp.exp(-b)), 0.0)
    o_intra = jnp.einsum('nbhij,nbhjv->nbhiv', scores, vb)

    def step(S, blk):
        q_c, k_c, v_c, a_last = blk
        o = jnp.einsum('bhik,bhkv->bhiv', q_c, S)
        S = S * jnp.exp(a_last)[..., None] + jnp.einsum('bhjk,bhjv->bhkv', k_c, v_c)
        return S, o

    s_fin, o_inter = lax.scan(step, s0.astype(jnp.float32), (q_in, k_st, vb, b_last[:, :, :, 0, :]))
    o = (o_intra + o_inter).transpose(1, 0, 3, 2, 4).reshape(B, n * L, H, V)[:, :T]
    return o.astype(v.dtype), s_fin.astype(s0.dtype)


def causal_depthwise_conv(x_ext, w):
    return lax.conv_general_dilated(x_ext, w[:, None, :].astype(x_ext.dtype), (1,), 'VALID',
                                    dimension_numbers=('NWC', 'WIO', 'NWC'),
                                    feature_group_count=x_ext.shape[-1])


def gla_ssd_mixer(h, s_gla0, s_ssm0, conv0, w_in, w_gate_up, b_gate, g_gla, conv_w, conv_b,
                  dt_bias, a_log, d_skip, g_ssd, w_out):
    B, T, _ = h.shape
    proj = h @ w_in
    q, k, v, g_lr, r, z, xbc, dt = split_cols(proj, IN_SIZES)
    q = q.reshape(B, T, GLA_HEADS, GLA_DK) * (GLA_DK ** -0.5)
    k = k.reshape(B, T, GLA_HEADS, GLA_DK)
    v = v.reshape(B, T, GLA_HEADS, GLA_DV)
    log_a = jax.nn.log_sigmoid((g_lr @ w_gate_up + b_gate).astype(jnp.float32)) / GLA_TAU
    log_a = log_a.reshape(B, T, GLA_HEADS, GLA_DK)
    o_a, s_gla = gated_linear_recurrence(q, k, v, log_a, s_gla0)
    o_a = rms_norm(o_a, g_gla).reshape(B, T, GLA_VW) * jax.nn.silu(r)
    xbc_ext = jnp.concatenate([conv0.astype(xbc.dtype), xbc], axis=1)
    new_conv = xbc_ext[:, xbc_ext.shape[1] - (SSD_CONV - 1):]
    xbc = jax.nn.silu(causal_depthwise_conv(xbc_ext, conv_w) + conv_b)
    xs, bm, cm = split_cols(xbc, (SSD_INNER, SSD_GROUPS * SSD_STATE, SSD_GROUPS * SSD_STATE))
    xs = xs.reshape(B, T, SSD_HEADS, SSD_HEADDIM)
    rep = SSD_HEADS // SSD_GROUPS
    bm = jnp.repeat(bm.reshape(B, T, SSD_GROUPS, SSD_STATE), rep, axis=2)
    cm = jnp.repeat(cm.reshape(B, T, SSD_GROUPS, SSD_STATE), rep, axis=2)
    dt = jax.nn.softplus((dt + dt_bias).astype(jnp.float32))
    log_decay = (dt * -jnp.exp(a_log.astype(jnp.float32)))[..., None]
    xdt = (xs * dt[..., None]).astype(xs.dtype)
    y, s_ssm = gated_linear_recurrence(cm, bm, xdt, log_decay, s_ssm0)
    y = y + xs * d_skip[:, None]
    y = rms_norm(y.reshape(B, T, SSD_INNER) * jax.nn.silu(z), g_ssd)
    out = jnp.concatenate([o_a, y], axis=-1) @ w_out
    return out, s_gla, s_ssm, new_conv


def diff_attention(h, past_k, past_v, w_qkv, lam_q1, lam_k1, lam_q2, lam_k2, g_subln, w_o, lambda_init):
    B, T, _ = h.shape
    q, k, v = jnp.split(h @ w_qkv, 3, axis=-1)
    q = q.reshape(B, T, DIFF_HEADS, 2 * DIFF_DH)
    k = k.reshape(B, T, DIFF_HEADS, 2 * DIFF_DH)
    v = v.reshape(B, T, DIFF_HEADS, 2 * DIFF_DH)
    k_all = jnp.concatenate([past_k.astype(k.dtype), k], axis=1)
    v_all = jnp.concatenate([past_v.astype(v.dtype), v], axis=1)
    past = past_k.shape[1]
    k_pos = jnp.arange(past + T, dtype=jnp.int32)
    q_pos = past + jnp.arange(T, dtype=jnp.int32)
    lam = (jnp.exp(jnp.sum(lam_q1.astype(jnp.float32) * lam_k1.astype(jnp.float32)))
           - jnp.exp(jnp.sum(lam_q2.astype(jnp.float32) * lam_k2.astype(jnp.float32))) + lambda_init)
    slopes = jnp.exp2(-8.0 * jnp.arange(1, DIFF_HEADS + 1, dtype=jnp.float32) / DIFF_HEADS)
    k1, k2 = k_all[..., :DIFF_DH], k_all[..., DIFF_DH:]
    k_chunk = k_pos // CHUNK
    scale = DIFF_DH ** -0.5

    def attend(q_blk, qp):
        dist = jnp.abs(qp[:, None] - k_pos[None, :]).astype(jnp.float32)
        bias = -slopes[:, None, None] * dist[None]
        visible = k_chunk[None, :] <= (qp // CHUNK)[:, None]

        def probs(qh, kh):
            s = jnp.einsum('blhd,bshd->bhls', qh, kh).astype(jnp.float32) * scale + bias
            return jax.nn.softmax(jnp.where(visible, s, -jnp.inf), axis=-1)

        a = probs(q_blk[..., :DIFF_DH], k1) - lam * probs(q_blk[..., DIFF_DH:], k2)
        return jnp.einsum('bhls,bshv->blhv', a.astype(v_all.dtype), v_all)

    if T > Q_BLOCK and T % Q_BLOCK == 0:
        nb = T // Q_BLOCK
        qb = q.reshape(B, nb, Q_BLOCK, DIFF_HEADS, 2 * DIFF_DH).transpose(1, 0, 2, 3, 4)
        o = lax.map(lambda blk: attend(blk[0], blk[1]), (qb, q_pos.reshape(nb, Q_BLOCK)))
        o = o.transpose(1, 0, 2, 3, 4).reshape(B, T, DIFF_HEADS, 2 * DIFF_DH)
    else:
        o = attend(q, q_pos)
    o = rms_norm(o, g_subln) * (1.0 - lambda_init)
    return o.reshape(B, T, ATT_WIDTH) @ w_o, k, v


def expert_dispatch(x, eid, gate, tok, w1, w3, w2):
    n_tok, D = x.shape
    S = eid.shape[0]
    M = MOE_BLOCK
    n_blocks = (S + N_EXPERTS * (M - 1) + M - 1) // M
    counts = jax.ops.segment_sum(jnp.ones_like(eid), eid, num_segments=N_EXPERTS)
    padded = (counts + M - 1) // M * M
    p_end = jnp.cumsum(padded)
    p_start = p_end - padded
    start = jnp.cumsum(counts) - counts
    order = jnp.argsort(eid)
    e_sorted = eid[order]
    dest = p_start[e_sorted] + jnp.arange(S, dtype=jnp.int32) - start[e_sorted]
    slot_tok = jnp.full((n_blocks * M,), n_tok, jnp.int32).at[dest].set(tok[order])
    slot_gate = jnp.zeros((n_blocks * M,), jnp.float32).at[dest].set(gate[order])
    block_e = jnp.minimum(jnp.searchsorted(p_end, jnp.arange(n_blocks, dtype=jnp.int32) * M, side='right'),
                          N_EXPERTS - 1)
    x_pad = jnp.concatenate([x, jnp.zeros((1, D), x.dtype)], axis=0)

    def run_block(blk):
        toks, e = blk
        xb = x_pad[toks]
        hb = jax.nn.silu(xb @ w1[e]) * (xb @ w3[e])
        return hb @ w2[e]

    yb = lax.map(run_block, (slot_tok.reshape(n_blocks, M), block_e))
    y = jax.ops.segment_sum(yb.reshape(-1, D).astype(jnp.float32) * slot_gate[:, None], slot_tok,
                            num_segments=n_tok + 1)
    return y[:n_tok].astype(x.dtype)


def hierarchical_moe(h, wg, bg, we, be, w1, w3, w2):
    B, T, D = h.shape
    n_tok = B * T
    x = h.reshape(n_tok, D)
    g_logits = (x @ wg).astype(jnp.float32) + bg
    g_sel = jnp.argmax(g_logits, axis=-1).astype(jnp.int32)
    p_g = jnp.take_along_axis(jax.nn.softmax(g_logits, axis=-1), g_sel[:, None], axis=-1)
    e_logits = ((x @ we).astype(jnp.float32) + be).reshape(n_tok, MOE_GROUPS, MOE_PER_GROUP)
    e_in = jnp.take_along_axis(e_logits, g_sel[:, None, None], axis=1)[:, 0]
    top_v, top_i = lax.top_k(e_in, MOE_TOP_K)
    gate = jax.nn.softmax(top_v, axis=-1) * p_g
    eid = (g_sel[:, None] * MOE_PER_GROUP + top_i.astype(jnp.int32)).reshape(-1)
    tok = jnp.repeat(jnp.arange(n_tok, dtype=jnp.int32), MOE_TOP_K)
    y = expert_dispatch(x, eid, gate.reshape(-1), tok, w1, w3, w2)
    return y.reshape(B, T, D)


def trunk(x, gla0, ssm0, conv0, past_k, past_v, p):
    gla_out, ssm_out, conv_out, k_out, v_out = [], [], [], [], []
    for layer in range(DEPTH):
        i = layer // 2
        h = rms_norm(x, p['norm_mix'][layer])
        if layer % 2 == 0:
            out, s_g, s_s, buf = gla_ssd_mixer(
                h, gla0[i], ssm0[i], conv0[i], p['w_in'][i], p['w_gate_up'][i], p['b_gate'][i],
                p['g_gla'][i], p['conv_w'][i], p['conv_b'][i], p['dt_bias'][i], p['a_log'][i],
                p['d_skip'][i], p['g_ssd'][i], p['w_out_mix'][i])
            gla_out.append(s_g)
            ssm_out.append(s_s)
            conv_out.append(buf)
        else:
            out, k_new, v_new = diff_attention(
                h, past_k[i], past_v[i], p['w_qkv'][i], p['lam_q1'][i], p['lam_k1'][i],
                p['lam_q2'][i], p['lam_k2'][i], p['g_subln'][i], p['w_o'][i],
                0.8 - 0.6 * math.exp(-0.3 * layer))
            k_out.append(k_new)
            v_out.append(v_new)
        x = x + out.astype(x.dtype)
        h = rms_norm(x, p['norm_ffn'][layer])
        x = x + hierarchical_moe(h, p['router_group_w'][layer], p['router_group_b'][layer],
                                 p['router_expert_w'][layer], p['router_expert_b'][layer],
                                 p['w1'][layer], p['w3'][layer], p['w2'][layer]).astype(x.dtype)
    y = rms_norm(x, p['norm_final'])
    return y, jnp.stack(gla_out), jnp.stack(ssm_out), jnp.stack(conv_out), jnp.stack(k_out), jnp.stack(v_out)


def setup_inputs(seed: int = 0) -> dict:
    key = jax.random.key(seed)
    keys = iter(jax.random.split(key, 48))

    def nrm(shape, scale):
        return jax.random.normal(next(keys), shape, jnp.float32) * scale

    def gain(shape):
        return 1.0 + nrm(shape, 0.02)

    dt0 = jnp.exp(jax.random.uniform(next(keys), (N_EVEN, SSD_HEADS), jnp.float32,
                                     math.log(1e-3), math.log(1e-1)))
    dt_bias = dt0 + jnp.log(-jnp.expm1(-dt0))
    a_log = jnp.log(jax.random.uniform(next(keys), (N_EVEN, SSD_HEADS), jnp.float32, 1.0, 16.0))
    return {
        'x_prompt': nrm((BATCH, SEQ, D_MODEL), 1.0),
        'x_sample': nrm((DEC_BATCH, DEC_SEQ, D_MODEL), 1.0),
        'state_gla': nrm((N_EVEN, DEC_BATCH, GLA_HEADS, GLA_DK, GLA_DV), 0.5),
        'state_ssm': nrm((N_EVEN, DEC_BATCH, SSD_HEADS, SSD_STATE, SSD_HEADDIM), 0.5),
        'state_conv': nrm((N_EVEN, DEC_BATCH, SSD_CONV - 1, SSD_CONV_DIM), 1.0),
        'cache_k': nrm((N_ODD, DEC_BATCH, PAST_LEN, DIFF_HEADS, 2 * DIFF_DH), 1.0),
        'cache_v': nrm((N_ODD, DEC_BATCH, PAST_LEN, DIFF_HEADS, 2 * DIFF_DH), 1.0),
        'norm_mix': gain((DEPTH, D_MODEL)),
        'norm_ffn': gain((DEPTH, D_MODEL)),
        'norm_final': gain((D_MODEL,)),
        'w_in': nrm((N_EVEN, D_MODEL, IN_DIM), D_MODEL ** -0.5),
        'w_gate_up': nrm((N_EVEN, GLA_GATE_RANK, GLA_QK), GLA_GATE_RANK ** -0.5),
        'b_gate': nrm((N_EVEN, GLA_QK), 0.02),
        'g_gla': gain((N_EVEN, GLA_HEADS, GLA_DV)),
        'conv_w': nrm((N_EVEN, SSD_CONV, SSD_CONV_DIM), SSD_CONV ** -0.5),
        'conv_b': nrm((N_EVEN, SSD_CONV_DIM), 0.02),
        'dt_bias': dt_bias,
        'a_log': a_log,
        'd_skip': gain((N_EVEN, SSD_HEADS)),
        'g_ssd': gain((N_EVEN, SSD_INNER)),
        'w_out_mix': nrm((N_EVEN, MIX_WIDTH, D_MODEL), MIX_WIDTH ** -0.5),
        'w_qkv': nrm((N_ODD, D_MODEL, 3 * ATT_WIDTH), D_MODEL ** -0.5),
        'lam_q1': nrm((N_ODD, DIFF_DH), 0.1),
        'lam_k1': nrm((N_ODD, DIFF_DH), 0.1),
        'lam_q2': nrm((N_ODD, DIFF_DH), 0.1),
        'lam_k2': nrm((N_ODD, DIFF_DH), 0.1),
        'g_subln': gain((N_ODD, DIFF_HEADS, 2 * DIFF_DH)),
        'w_o': nrm((N_ODD, ATT_WIDTH, D_MODEL), ATT_WIDTH ** -0.5),
        'router_group_w': nrm((DEPTH, D_MODEL, MOE_GROUPS), D_MODEL ** -0.5),
        'router_group_b': nrm((DEPTH, MOE_GROUPS), 0.01),
        'router_expert_w': nrm((DEPTH, D_MODEL, N_EXPERTS), D_MODEL ** -0.5),
        'router_expert_b': nrm((DEPTH, N_EXPERTS), 0.01),
        'w1': nrm((DEPTH, N_EXPERTS, D_MODEL, D_EXPERT), D_MODEL ** -0.5),
        'w3': nrm((DEPTH, N_EXPERTS, D_MODEL, D_EXPERT), D_MODEL ** -0.5),
        'w2': nrm((DEPTH, N_EXPERTS, D_EXPERT, D_MODEL), D_EXPERT ** -0.5),
    }


def reference(x_prompt, x_sample, state_gla, state_ssm, state_conv, cache_k, cache_v,
              norm_mix, norm_ffn, norm_final, w_in, w_gate_up, b_gate, g_gla, conv_w, conv_b,
              dt_bias, a_log, d_skip, g_ssd, w_out_mix, w_qkv, lam_q1, lam_k1, lam_q2, lam_k2,
              g_subln, w_o, router_group_w, router_group_b, router_expert_w, router_expert_b,
              w1, w3, w2):
    params = dict(norm_mix=norm_mix, norm_ffn=norm_ffn, norm_final=norm_final, w_in=w_in,
                  w_gate_up=w_gate_up, b_gate=b_gate, g_gla=g_gla, conv_w=conv_w, conv_b=conv_b,
                  dt_bias=dt_bias, a_log=a_log, d_skip=d_skip, g_ssd=g_ssd, w_out_mix=w_out_mix,
                  w_qkv=w_qkv, lam_q1=lam_q1, lam_k1=lam_k1, lam_q2=lam_q2, lam_k2=lam_k2,
                  g_subln=g_subln, w_o=w_o, router_group_w=router_group_w,
                  router_group_b=router_group_b, router_expert_w=router_expert_w,
                  router_expert_b=router_expert_b, w1=w1, w3=w3, w2=w2)
    B = x_prompt.shape[0]
    zero_gla = jnp.zeros((N_EVEN, B, GLA_HEADS, GLA_DK, GLA_DV), state_gla.dtype)
    zero_ssm = jnp.zeros((N_EVEN, B, SSD_HEADS, SSD_STATE, SSD_HEADDIM), state_ssm.dtype)
    zero_conv = jnp.zeros((N_EVEN, B, SSD_CONV - 1, SSD_CONV_DIM), state_conv.dtype)
    empty_kv = jnp.zeros((N_ODD, B, 0, DIFF_HEADS, 2 * DIFF_DH), cache_k.dtype)
    y_prompt, p_gla, p_ssm, p_conv, p_k, p_v = trunk(x_prompt, zero_gla, zero_ssm, zero_conv,
                                                     empty_kv, empty_kv, params)
    y_sample, s_gla, s_ssm, s_conv, s_k, s_v = trunk(x_sample, state_gla, state_ssm, state_conv,
                                                     cache_k, cache_v, params)
    return (y_prompt, y_sample, p_gla, p_ssm, p_conv, p_k, p_v, s_gla, s_ssm, s_conv, s_k, s_v)
```

```python
import functools
import math

import numpy as np
import jax
import jax.numpy as jnp
from jax import lax
from jax.experimental import pallas as pl
from jax.experimental.pallas import tpu as pltpu

F32 = jnp.float32
BF16 = jnp.bfloat16

EPS = 1e-6
CHUNK = 64
GLA_HEADS = 4
GLA_TAU = 16.0
SSD_HEADDIM = 64
SSD_STATE = 128
SSD_GROUPS = 4
SSD_CONV = 4
DIFF_HEADS = 8
DIFF_DH = 128
MOE_GROUPS = 4
MOE_PER_GROUP = 8
N_EXPERTS = MOE_GROUPS * MOE_PER_GROUP

LANES = 128
REC_ROWS = 128
STRIP = 16
CONV_PAD = 8
VMEM_LIMIT = 56 * 1024 * 1024


def _cparams(sem, vmem=VMEM_LIMIT):
    return pltpu.CompilerParams(dimension_semantics=sem, vmem_limit_bytes=vmem)


def _dot(a, b):
    return jnp.dot(a, b, preferred_element_type=F32)


def _dot_nt(a, b):
    return lax.dot_general(a, b, (((1,), (1,)), ((), ())), preferred_element_type=F32)


def _dot_tn(a, b):
    return lax.dot_general(a, b, (((0,), (0,)), ((), ())), preferred_element_type=F32)


def _split_hi_lo(a):
    hi = a.astype(BF16)
    lo = (a - hi.astype(F32)).astype(BF16)
    return hi, lo


def _exact_left(m, a):
    hi, lo = _split_hi_lo(a)
    return _dot(m, hi) + _dot(m, lo)


def _exact_right(a, m):
    hi, lo = _split_hi_lo(a)
    return _dot(hi, m) + _dot(lo, m)


def _silu(x):
    return x / (1.0 + jnp.exp(-x))


def _log1p_exp_neg_abs(x):
    e = jnp.exp(-jnp.abs(x))
    u = 1.0 + e
    return jnp.where(u == 1.0, e, jnp.log(u) * (e / (u - 1.0)))


def _rms(x, g):
    return x * lax.rsqrt(jnp.mean(x * x, axis=-1, keepdims=True) + EPS) * g


def _norm_matmul_body(x_ref, g_ref, w_ref, o_ref, h_ref):
    @pl.when(pl.program_id(1) == 0)
    def _():
        h_ref[...] = _rms(x_ref[...], g_ref[...])

    o_ref[...] = _dot(h_ref[...], w_ref[...])


def _norm_matmul(x, g, w, tm, tn):
    n, d = x.shape
    nout = w.shape[1]
    assert n % tm == 0 and nout % tn == 0
    return pl.pallas_call(
        _norm_matmul_body,
        grid=(n // tm, nout // tn),
        in_specs=[pl.BlockSpec((tm, d), lambda i, j: (i, 0)),
                  pl.BlockSpec((1, d), lambda i, j: (0, 0)),
                  pl.BlockSpec((d, tn), lambda i, j: (0, j))],
        out_specs=pl.BlockSpec((tm, tn), lambda i, j: (i, j)),
        out_shape=jax.ShapeDtypeStruct((n, nout), F32),
        scratch_shapes=[pltpu.VMEM((tm, d), F32)],
        compiler_params=_cparams(("arbitrary", "arbitrary")),
        name="norm_matmul",
    )(x, g.reshape(1, d), w)


def _mixer_body(p_ref, gla0_ref, ssm0_ref, conv0_ref, wup_ref, bgate_ref, ggla_ref, convw_ref,
                convb_ref, dtb_ref, aneg_ref, dskip_ref, gssd_ref, tri_ref, e64_ref, ec_ref,
                eye_ref, caus_ref, oprev_ref,
                o_ref, gla_ref, ssm_ref, ctail_ref,
                sg_ref, ss_ref, ext_ref, b_ref, *, rb, dk, dv):
    del oprev_ref
    C = REC_ROWS
    c = pl.program_id(1)
    nc = pl.num_programs(1)
    nh = GLA_HEADS
    qk_w = nh * dk
    vw = nh * dv
    inner = vw
    gs = SSD_GROUPS * SSD_STATE
    o_q, o_k, o_v = 0, qk_w, 2 * qk_w
    o_r = o_v + vw
    o_z = o_r + vw
    o_x = o_z + inner
    o_g = o_x + inner + 2 * gs
    o_dt = o_g + LANES

    @pl.when(c == 0)
    def _init():
        sg_ref[...] = gla0_ref[...]
        ss_ref[...] = ssm0_ref[...]
        ext_ref[0:CONV_PAD, :] = conv0_ref[...]

    p = p_ref[...]
    if rb < C:
        p = jnp.concatenate([p, jnp.zeros((C - rb, p.shape[1]), F32)], axis=0)

    def rowmask(width):
        return lax.broadcasted_iota(jnp.int32, (C, width), 0) < rb

    q = p[:, o_q:o_q + qk_w] * (dk ** -0.5)
    k = p[:, o_k:o_k + qk_w]
    v = p[:, o_v:o_v + vw]
    r = p[:, o_r:o_r + vw]
    z = p[:, o_z:o_z + inner]
    xbc = p[:, o_x:o_x + inner + 2 * gs]
    glr = p[:, o_g:o_g + LANES]
    dtp = p[:, o_dt:o_dt + LANES]

    tri = tri_ref[...]

    zg = _dot(glr, wup_ref[...]) + bgate_ref[...]
    la = (jnp.minimum(zg, 0.0) - _log1p_exp_neg_abs(zg)) * (1.0 / GLA_TAU)
    if rb < C:
        la = jnp.where(rowmask(qk_w), la, 0.0)
    bcum = _exact_left(tri, la)
    b_ref[...] = bcum

    row_i = lax.broadcasted_iota(jnp.int32, (C, dk), 0)
    prow = lax.broadcasted_iota(jnp.int32, (STRIP, C), 0)
    pcol = lax.broadcasted_iota(jnp.int32, (STRIP, C), 1)
    o_heads = []
    for h in range(nh):
        hs = slice(h * dk, (h + 1) * dk)
        vs = slice(h * dv, (h + 1) * dv)
        bh = bcum[:, hs]
        qh = q[:, hs]
        kh = k[:, hs]
        vh = v[:, vs]
        strips = []
        for i in range(C // STRIP):
            r0 = i * STRIP
            if i == 0:
                ref_row = jnp.zeros((1, dk), F32)
            else:
                ref_row = b_ref[pl.ds(r0 - 1, 1), hs]
            q_i = qh[r0:r0 + STRIP] * jnp.exp(bh[r0:r0 + STRIP] - ref_row)
            e = jnp.where(row_i < r0 + STRIP, ref_row - bh, 0.0)
            k_i = kh * jnp.exp(e)
            s_i = _dot_nt(q_i, k_i)
            strips.append(jnp.where(pcol <= prow + r0, s_i, 0.0))
        pmat = jnp.concatenate(strips, axis=0)
        b_last = b_ref[pl.ds(C - 1, 1), hs]
        st = sg_ref[h]
        o_h = _dot(pmat, vh) + _dot_nt(qh * jnp.exp(bh), st)
        k_st = kh * jnp.exp(b_last - bh)
        sg_ref[h] = st * jnp.exp(b_last) + _dot_tn(vh, k_st)
        o_h = _rms(o_h, ggla_ref[pl.ds(h, 1), :]) * _silu(r[:, vs])
        o_heads.append(o_h)
    o_a = jnp.concatenate(o_heads, axis=1)

    cw = inner + 2 * gs
    ext_ref[CONV_PAD:CONV_PAD + C, :] = xbc
    conv = convb_ref[...]
    for j in range(SSD_CONV):
        conv = conv + convw_ref[pl.ds(j, 1), :] * ext_ref[pl.ds(CONV_PAD - (SSD_CONV - 1) + j, C), :]
    xc = _silu(conv)
    xs = xc[:, :inner]
    bm = xc[:, inner:inner + gs]
    cm = xc[:, inner + gs:cw]
    dtv = dtp + dtb_ref[...]
    dt = jnp.maximum(dtv, 0.0) + _log1p_exp_neg_abs(dtv)
    ld = dt * aneg_ref[...]
    if rb < C:
        dt = jnp.where(rowmask(LANES), dt, 0.0)
        ld = jnp.where(rowmask(LANES), ld, 0.0)
    bs = _exact_left(tri, ld)
    e64 = e64_ref[...]
    dt64 = _exact_right(dt, e64)
    b64 = _exact_right(bs, e64)
    blast64 = b64[C - 1:C, :]
    xdt = xs * dt64
    bc = _exact_right(bs, ec_ref[...])
    br = jnp.sum(eye_ref[...] * bc, axis=0, keepdims=True)
    vis = caus_ref[...] > 0.0
    lf = jnp.where(vis, jnp.exp(jnp.where(vis, bc - br, 0.0)), 0.0)
    hpg = inner // SSD_HEADDIM // SSD_GROUPS
    gw = hpg * SSD_HEADDIM
    lane_head = lax.broadcasted_iota(jnp.int32, (C, gw), 1) // SSD_HEADDIM
    y_groups = []
    for g in range(SSD_GROUPS):
        cm_g = cm[:, g * SSD_STATE:(g + 1) * SSD_STATE]
        bm_g = bm[:, g * SSD_STATE:(g + 1) * SSD_STATE]
        gl = slice(g * gw, (g + 1) * gw)
        gmat = _dot_nt(cm_g, bm_g)
        a4 = jnp.concatenate([gmat] * hpg, axis=1) * lf[:, g * hpg * C:(g + 1) * hpg * C]
        xg = xdt[:, gl]
        x4 = jnp.concatenate([jnp.where(lane_head == hh, xg, 0.0) for hh in range(hpg)], axis=0)
        s_g = ss_ref[g]
        y_g = _dot(a4, x4) + _dot(cm_g, s_g) * jnp.exp(b64[:, gl])
        xw = xg * jnp.exp(blast64[:, gl] - b64[:, gl])
        ss_ref[g] = s_g * jnp.exp(blast64[:, gl]) + _dot_tn(bm_g, xw)
        y_groups.append(y_g)
    y = jnp.concatenate(y_groups, axis=1) + xs * dskip_ref[...]
    y = _rms(y * _silu(z), gssd_ref[...])

    o_full = jnp.concatenate([o_a, y], axis=1)
    o_ref[...] = o_full[:rb]

    @pl.when(c == nc - 1)
    def _fin():
        gla_ref[...] = sg_ref[...]
        ssm_ref[...] = ss_ref[...]
        ctail_ref[...] = ext_ref[pl.ds(rb, CONV_PAD), :]

    ext_ref[0:CONV_PAD, :] = ext_ref[pl.ds(rb, CONV_PAD), :]


def _mixer_call(proj, o_prev, row_off, nb, t, gla0, ssm0, conv0, consts, n_rows, dk, dv):
    C = REC_ROWS
    rb = min(t, C)
    assert t % rb == 0 and row_off % rb == 0
    steps = t // rb
    width = proj.shape[1]
    nh = GLA_HEADS
    vw = nh * dv
    cw = conv0.shape[-1]
    blk0 = row_off // rb

    def full(a):
        nd = a.ndim
        return pl.BlockSpec(a.shape, lambda b, c, _n=nd: (0,) * _n)

    in_specs = [pl.BlockSpec((rb, width), lambda b, c: (blk0 + b * steps + c, 0)),
                pl.BlockSpec((None, nh, dv, dk), lambda b, c: (b, 0, 0, 0)),
                pl.BlockSpec((None, SSD_GROUPS, SSD_STATE, ssm0.shape[-1]), lambda b, c: (b, 0, 0, 0)),
                pl.BlockSpec((None, CONV_PAD, cw), lambda b, c: (b, 0, 0))]
    in_specs += [full(a) for a in consts]
    in_specs += [pl.BlockSpec(memory_space=pl.ANY)]
    n_in = len(in_specs)
    out_specs = [pl.BlockSpec((rb, 2 * vw), lambda b, c: (blk0 + b * steps + c, 0)),
                 pl.BlockSpec((None, nh, dv, dk), lambda b, c: (b, 0, 0, 0)),
                 pl.BlockSpec((None, SSD_GROUPS, SSD_STATE, ssm0.shape[-1]), lambda b, c: (b, 0, 0, 0)),
                 pl.BlockSpec((None, CONV_PAD, cw), lambda b, c: (b, 0, 0))]
    out_shape = [jax.ShapeDtypeStruct((n_rows, 2 * vw), F32),
                 jax.ShapeDtypeStruct(gla0.shape, F32),
                 jax.ShapeDtypeStruct(ssm0.shape, F32),
                 jax.ShapeDtypeStruct((nb, CONV_PAD, cw), F32)]
    return pl.pallas_call(
        functools.partial(_mixer_body, rb=rb, dk=dk, dv=dv),
        grid=(nb, steps),
        in_specs=in_specs,
        out_specs=out_specs,
        out_shape=out_shape,
        scratch_shapes=[pltpu.VMEM((nh, dv, dk), F32),
                        pltpu.VMEM((SSD_GROUPS, SSD_STATE, ssm0.shape[-1]), F32),
                        pltpu.VMEM((C + CONV_PAD, cw), F32),
                        pltpu.VMEM((C, nh * dk), F32)],
        input_output_aliases={n_in - 1: 0},
        compiler_params=_cparams(("arbitrary", "arbitrary")),
        name="gla_ssd_mixer",
    )(proj, gla0, ssm0, conv0, *consts, o_prev)


def _proj_resid_body(o_ref, w_ref, x_ref, g_ref, wr_ref, x1_ref, h_ref, lg_ref):
    x1 = x_ref[...] + _dot(o_ref[...], w_ref[...])
    x1_ref[...] = x1
    h = _rms(x1, g_ref[...])
    h_ref[...] = h
    lg_ref[...] = _dot(h, wr_ref[...])


def _proj_resid(o, w, x, g, wr, tm):
    n, d = x.shape
    kin = o.shape[1]
    return pl.pallas_call(
        _proj_resid_body,
        grid=(n // tm,),
        in_specs=[pl.BlockSpec((tm, kin), lambda i: (i, 0)),
                  pl.BlockSpec((kin, d), lambda i: (0, 0), pipeline_mode=pl.Buffered(1)),
                  pl.BlockSpec((tm, d), lambda i: (i, 0)),
                  pl.BlockSpec((1, d), lambda i: (0, 0)),
                  pl.BlockSpec((d, LANES), lambda i: (0, 0))],
        out_specs=[pl.BlockSpec((tm, d), lambda i: (i, 0)),
                   pl.BlockSpec((tm, d), lambda i: (i, 0)),
                   pl.BlockSpec((tm, LANES), lambda i: (i, 0))],
        out_shape=[jax.ShapeDtypeStruct((n, d), F32),
                   jax.ShapeDtypeStruct((n, d), F32),
                   jax.ShapeDtypeStruct((n, LANES), F32)],
        compiler_params=_cparams(("arbitrary",)),
        name="proj_resid_norm_router",
    )(o, w, x, g.reshape(1, d), wr)


_META_E0, _META_E1, _META_G0, _META_G1, _META_R0, _META_R1 = range(6)
_EXP_LANE0 = MOE_GROUPS


def _route_body(lg_ref, bias_ref, ltri_ref, meta_ref, cnt_ref, base_ref):
    i = pl.program_id(0)

    @pl.when(i == 0)
    def _():
        base_ref[...] = jnp.zeros_like(base_ref)

    lg = lg_ref[...] + bias_ref[...]
    tm = lg.shape[0]
    lane_i = lax.broadcasted_iota(jnp.int32, (tm, LANES), 1)
    lane = lane_i.astype(F32)
    neg = -jnp.inf
    glog = jnp.where(lane_i < MOE_GROUPS, lg, neg)
    gmax = jnp.max(glog, axis=1, keepdims=True)
    gsel = jnp.min(jnp.where(glog == gmax, lane, float(LANES)), axis=1, keepdims=True)
    pg = 1.0 / jnp.sum(jnp.exp(glog - gmax), axis=1, keepdims=True)
    lo = _EXP_LANE0 + MOE_PER_GROUP * gsel
    el = jnp.where((lane >= lo) & (lane < lo + MOE_PER_GROUP), lg, neg)
    v1 = jnp.max(el, axis=1, keepdims=True)
    i1 = jnp.min(jnp.where(el == v1, lane, float(LANES)), axis=1, keepdims=True)
    el2 = jnp.where(lane == i1, neg, el)
    v2 = jnp.max(el2, axis=1, keepdims=True)
    i2 = jnp.min(jnp.where(el2 == v2, lane, float(LANES)), axis=1, keepdims=True)
    e = jnp.exp(v2 - v1)
    g1 = pg / (1.0 + e)
    g2 = pg * e / (1.0 + e)
    hot1 = lane == i1
    hot2 = lane == i2
    onehot = jnp.where(hot1 | hot2, 1.0, 0.0)
    before = _dot(ltri_ref[...], onehot.astype(BF16)) + base_ref[0:1, :]
    r1 = jnp.sum(jnp.where(hot1, before, 0.0), axis=1, keepdims=True)
    r2 = jnp.sum(jnp.where(hot2, before, 0.0), axis=1, keepdims=True)
    base_ref[0:1, :] = base_ref[0:1, :] + jnp.sum(onehot, axis=0, keepdims=True)
    meta = jnp.zeros((tm, LANES), F32)
    for idx, val in ((_META_E0, i1 - _EXP_LANE0), (_META_E1, i2 - _EXP_LANE0),
                     (_META_G0, g1), (_META_G1, g2), (_META_R0, r1), (_META_R1, r2)):
        meta = jnp.where(lane_i == idx, val, meta)
    meta_ref[...] = meta

    @pl.when(i == pl.num_programs(0) - 1)
    def _():
        cnt_ref[...] = base_ref[...]


def _route(logits, bias, tm):
    n = logits.shape[0]
    ltri = jnp.asarray(np.tril(np.ones((tm, tm), np.float32), -1), BF16)
    return pl.pallas_call(
        _route_body,
        grid=(n // tm,),
        in_specs=[pl.BlockSpec((tm, LANES), lambda i: (i, 0)),
                  pl.BlockSpec((1, LANES), lambda i: (0, 0)),
                  pl.BlockSpec((tm, tm), lambda i: (0, 0))],
        out_specs=[pl.BlockSpec((tm, LANES), lambda i: (i, 0)),
                   pl.BlockSpec((8, LANES), lambda i: (0, 0))],
        out_shape=[jax.ShapeDtypeStruct((n, LANES), F32),
                   jax.ShapeDtypeStruct((8, LANES), F32)],
        scratch_shapes=[pltpu.VMEM((8, LANES), F32)],
        compiler_params=_cparams(("arbitrary",)),
        name="moe_route",
    )(logits, bias, ltri)


def _dispatch_body(dest_ref, h_ref, xs_in_ref, xs_ref, sem, *, tm):
    del xs_in_ref
    i = pl.program_id(0)

    def row_copy(r, kk):
        d = dest_ref[(i * tm + r) * 2 + kk]
        return pltpu.make_async_copy(h_ref.at[pl.ds(r, 1), :], xs_ref.at[pl.ds(d, 1), :], sem)

    def start(r, carry):
        row_copy(r, 0).start()
        row_copy(r, 1).start()
        return carry

    def wait(r, carry):
        row_copy(r, 0).wait()
        row_copy(r, 1).wait()
        return carry

    lax.fori_loop(0, tm, start, 0)
    lax.fori_loop(0, tm, wait, 0)


def _dispatch(h, dest_flat, xs_init, tm):
    n, d = h.shape
    return pl.pallas_call(
        functools.partial(_dispatch_body, tm=tm),
        grid_spec=pltpu.PrefetchScalarGridSpec(
            num_scalar_prefetch=1,
            grid=(n // tm,),
            in_specs=[pl.BlockSpec((tm, d), lambda i, dest: (i, 0)),
                      pl.BlockSpec(memory_space=pl.ANY)],
            out_specs=pl.BlockSpec(memory_space=pl.ANY),
            scratch_shapes=[pltpu.SemaphoreType.DMA(())]),
        out_shape=jax.ShapeDtypeStruct(xs_init.shape, xs_init.dtype),
        input_output_aliases={2: 0},
        compiler_params=_cparams(("arbitrary",)),
        name="moe_dispatch",
    )(dest_flat, h, xs_init)


def _experts_body(be_ref, nb_ref, x_ref, w1_ref, w3_ref, w2_ref, y_ref):
    used = pl.program_id(0) < nb_ref[0]

    @pl.when(used)
    def _():
        x = x_ref[...]
        a = _dot(x, w1_ref[...])
        b = _dot(x, w3_ref[...])
        y_ref[...] = _dot(_silu(a) * b, w2_ref[...])

    @pl.when(jnp.logical_not(used))
    def _():
        y_ref[...] = jnp.zeros(y_ref.shape, F32)


def _experts(xs, block_e, nb_used, w1, w3, w2, layer, tm):
    s, d = xs.shape
    f = w1.shape[-1]
    nblk = s // tm

    def row_map(b, be, nb):
        return (jnp.minimum(b, nb[0] - 1), 0)

    return pl.pallas_call(
        _experts_body,
        grid_spec=pltpu.PrefetchScalarGridSpec(
            num_scalar_prefetch=2,
            grid=(nblk,),
            in_specs=[pl.BlockSpec((tm, d), row_map),
                      pl.BlockSpec((None, None, d, f), lambda b, be, nb: (layer, be[b], 0, 0)),
                      pl.BlockSpec((None, None, d, f), lambda b, be, nb: (layer, be[b], 0, 0)),
                      pl.BlockSpec((None, None, f, d), lambda b, be, nb: (layer, be[b], 0, 0))],
            out_specs=pl.BlockSpec((tm, d), lambda b, be, nb: (b, 0))),
        out_shape=jax.ShapeDtypeStruct((s, d), F32),
        compiler_params=_cparams(("arbitrary",)),
        name="moe_experts",
    )(block_e, nb_used, xs, w1, w3, w2)


def _combine_body(dest_ref, x_ref, meta_ref, gfin_ref, ys_ref, *rest, tm, n_first, final):
    if final:
        o_a_ref, o_b_ref, buf, sem = rest
    else:
        o_a_ref, buf, sem = rest
        o_b_ref = None
    i = pl.program_id(0)
    nsteps = pl.num_programs(0)

    def row_copy(step, slot, r, kk):
        d = dest_ref[(step * tm + r) * 2 + kk]
        return pltpu.make_async_copy(ys_ref.at[pl.ds(d, 1), :], buf.at[slot, kk, pl.ds(r, 1), :],
                                     sem.at[slot])

    def issue(step, slot):
        def body(r, carry):
            row_copy(step, slot, r, 0).start()
            row_copy(step, slot, r, 1).start()
            return carry
        lax.fori_loop(0, tm, body, 0)

    @pl.when(i == 0)
    def _():
        issue(0, 0)

    @pl.when(i + 1 < nsteps)
    def _():
        issue(i + 1, (i + 1) % 2)

    slot = i % 2

    def wbody(r, carry):
        row_copy(i, slot, r, 0).wait()
        row_copy(i, slot, r, 1).wait()
        return carry
    lax.fori_loop(0, tm, wbody, 0)

    meta = meta_ref[...]
    g0 = meta[:, _META_G0:_META_G0 + 1]
    g1 = meta[:, _META_G1:_META_G1 + 1]
    out = x_ref[...] + (buf[slot, 0] * g0 + buf[slot, 1] * g1)
    if not final:
        o_a_ref[...] = out
    else:
        out = _rms(out, gfin_ref[...])

        @pl.when(i < n_first)
        def _():
            o_a_ref[...] = out

        @pl.when(i >= n_first)
        def _():
            o_b_ref[...] = out


def _combine(x, meta, dest_flat, ys, gfin, tm, n_first_rows, final):
    n, d = x.shape
    n_first = n_first_rows // tm
    if final:
        out_specs = [pl.BlockSpec((tm, d), lambda i, dest: (jnp.minimum(i, n_first - 1), 0)),
                     pl.BlockSpec((tm, d), lambda i, dest: (jnp.maximum(i - n_first, 0), 0))]
        out_shape = [jax.ShapeDtypeStruct((n_first_rows, d), F32),
                     jax.ShapeDtypeStruct((n - n_first_rows, d), F32)]
    else:
        out_specs = [pl.BlockSpec((tm, d), lambda i, dest: (i, 0))]
        out_shape = [jax.ShapeDtypeStruct((n, d), F32)]
    return pl.pallas_call(
        functools.partial(_combine_body, tm=tm, n_first=n_first, final=final),
        grid_spec=pltpu.PrefetchScalarGridSpec(
            num_scalar_prefetch=1,
            grid=(n // tm,),
            in_specs=[pl.BlockSpec((tm, d), lambda i, dest: (i, 0)),
                      pl.BlockSpec((tm, LANES), lambda i, dest: (i, 0)),
                      pl.BlockSpec((1, d), lambda i, dest: (0, 0)),
                      pl.BlockSpec(memory_space=pl.ANY)],
            out_specs=out_specs,
            scratch_shapes=[pltpu.VMEM((2, 2, tm, d), F32),
                            pltpu.SemaphoreType.DMA((2,))]),
        out_shape=out_shape,
        compiler_params=_cparams(("arbitrary",)),
        name="moe_combine_final" if final else "moe_combine",
    )(dest_flat, x, meta, gfin.reshape(1, d), ys)


def _moe(x1, h, logits, rbias, w1, w3, w2, layer, gfin, n_first_rows, final, tm_e=256, tm_t=128):
    n, d = h.shape
    meta, cnt = _route(logits, rbias, 256)
    eid = meta[:, _META_E0:_META_E1 + 1].astype(jnp.int32)
    rank = meta[:, _META_R0:_META_R1 + 1].astype(jnp.int32)
    counts = cnt[0, _EXP_LANE0:_EXP_LANE0 + N_EXPERTS].astype(jnp.int32)
    padded = (counts + tm_e - 1) // tm_e * tm_e
    p_end = jnp.cumsum(padded)
    p_start = p_end - padded
    dest = (p_start[eid] + rank).reshape(-1)
    s = n * 2
    nblk = (s + N_EXPERTS * (tm_e - 1) + tm_e - 1) // tm_e
    block_e = jnp.minimum(jnp.searchsorted(p_end, jnp.arange(nblk, dtype=jnp.int32) * tm_e, side='right'),
                          N_EXPERTS - 1).astype(jnp.int32)
    nb_used = (p_end[-1:] // tm_e).astype(jnp.int32)
    xs = _dispatch(h, dest, jnp.zeros((nblk * tm_e, d), F32), tm_t)
    ys = _experts(xs, block_e, nb_used, w1, w3, w2, layer, tm_e)
    return _combine(x1, meta, dest, ys, gfin, tm_t, n_first_rows, final)


def _lambda_value(lam_ref, lambda_init):
    lam = lam_ref[...]
    s1 = jnp.sum(lam[0:1] * lam[1:2], axis=1, keepdims=True)
    s2 = jnp.sum(lam[2:3] * lam[3:4], axis=1, keepdims=True)
    return jnp.exp(s1) - jnp.exp(s2) + lambda_init


def _softmax_step(j, s, v, m_ref, l_ref, acc_ref):
    m_prev = m_ref[j]
    m_new = jnp.maximum(m_prev, jnp.max(s, axis=1, keepdims=True))
    alpha = jnp.exp(m_prev - m_new)
    p = jnp.exp(s - m_new[:, 0:1])
    l_ref[j] = alpha * l_ref[j] + jnp.sum(p, axis=1, keepdims=True)
    acc_ref[j] = acc_ref[j] * alpha[:, 0:1] + _dot(p, v)
    m_ref[j] = m_new


def _attn_finish(lam_ref, g_ref, l_ref, acc_ref, lambda_init):
    lam = _lambda_value(lam_ref, lambda_init)
    o = acc_ref[0] / l_ref[0][:, 0:1] - lam * (acc_ref[1] / l_ref[1][:, 0:1])
    return _rms(o, g_ref[...]) * (1.0 - lambda_init)


def _attn_prompt_body(iq_ref, ik_ref, slope_ref, q_ref, k_ref, v_ref, lam_ref, g_ref, oprev_ref,
                      o_ref, m_ref, l_ref, acc_ref, *, tq, tk, lambda_init):
    del oprev_ref
    h = pl.program_id(1)
    t = pl.program_id(2)
    iq = iq_ref[t]
    ik = ik_ref[t]
    dh = DIFF_DH

    @pl.when(ik == 0)
    def _():
        m_ref[...] = jnp.full(m_ref.shape, -jnp.inf, F32)
        l_ref[...] = jnp.zeros(l_ref.shape, F32)
        acc_ref[...] = jnp.zeros(acc_ref.shape, F32)

    q = q_ref[...]
    k = k_ref[...]
    v = v_ref[...]
    qpos = iq * tq + lax.broadcasted_iota(jnp.int32, (tq, tk), 0)
    kpos = ik * tk + lax.broadcasted_iota(jnp.int32, (tq, tk), 1)
    bias = jnp.abs(qpos - kpos).astype(F32) * (-slope_ref[h])
    vis = (kpos // CHUNK) <= (qpos // CHUNK)
    scale = dh ** -0.5
    for j in range(2):
        s = _dot_nt(q[:, j * dh:(j + 1) * dh], k[:, j * dh:(j + 1) * dh]) * scale + bias
        s = jnp.where(vis, s, -jnp.inf)
        _softmax_step(j, s, v, m_ref, l_ref, acc_ref)

    last = ((iq + 1) * tq - 1) // tk

    @pl.when(ik == last)
    def _():
        o_ref[...] = _attn_finish(lam_ref, g_ref, l_ref, acc_ref, lambda_init)


def _attn_prompt(qkv, o_prev, nb, t, lam, g_subln, lambda_init, tq, tk):
    n = qkv.shape[0]
    hw = 2 * DIFF_DH
    nq, nk = t // tq, t // tk
    pairs = [(iq, ik) for iq in range(nq) for ik in range(((iq + 1) * tq - 1) // tk + 1)]
    iq_tab = jnp.asarray(np.array([p[0] for p in pairs], np.int32))
    ik_tab = jnp.asarray(np.array([p[1] for p in pairs], np.int32))
    slopes = jnp.exp2(-8.0 * jnp.arange(1, DIFF_HEADS + 1, dtype=F32) / DIFF_HEADS)
    g3 = g_subln.reshape(DIFF_HEADS, 1, hw)
    return pl.pallas_call(
        functools.partial(_attn_prompt_body, tq=tq, tk=tk, lambda_init=lambda_init),
        grid_spec=pltpu.PrefetchScalarGridSpec(
            num_scalar_prefetch=3,
            grid=(nb, DIFF_HEADS, len(pairs)),
            in_specs=[pl.BlockSpec((tq, hw), lambda b, h, p, iqt, ikt, sl: (b * nq + iqt[p], h)),
                      pl.BlockSpec((tk, hw), lambda b, h, p, iqt, ikt, sl: (b * nk + ikt[p], DIFF_HEADS + h)),
                      pl.BlockSpec((tk, hw), lambda b, h, p, iqt, ikt, sl: (b * nk + ikt[p], 2 * DIFF_HEADS + h)),
                      pl.BlockSpec((4, DIFF_DH), lambda b, h, p, iqt, ikt, sl: (0, 0)),
                      pl.BlockSpec((None, 1, hw), lambda b, h, p, iqt, ikt, sl: (h, 0, 0)),
                      pl.BlockSpec(memory_space=pl.ANY)],
            out_specs=pl.BlockSpec((tq, hw), lambda b, h, p, iqt, ikt, sl: (b * nq + iqt[p], h)),
            scratch_shapes=[pltpu.VMEM((2, tq, LANES), F32),
                            pltpu.VMEM((2, tq, LANES), F32),
                            pltpu.VMEM((2, tq, hw), F32)]),
        out_shape=jax.ShapeDtypeStruct((n, DIFF_HEADS * hw), F32),
        input_output_aliases={8: 0},
        compiler_params=_cparams(("arbitrary", "arbitrary", "arbitrary")),
        name="diff_attn_prompt",
    )(iq_tab, ik_tab, slopes, qkv, qkv, qkv, lam, g3, o_prev)


def _attn_sample_body(q_ref, kc_ref, vc_ref, kn_ref, vn_ref, lam_ref, g_ref, oprev_ref,
                      o_ref, m_ref, l_ref, acc_ref, *, tk, past, lambda_init):
    del oprev_ref
    ik = pl.program_id(1)
    nkb = pl.num_programs(1) - 1
    dh = DIFF_DH
    hw = 2 * dh
    tq = q_ref.shape[0]
    scale = dh ** -0.5

    @pl.when(ik == 0)
    def _():
        m_ref[...] = jnp.full(m_ref.shape, -jnp.inf, F32)
        l_ref[...] = jnp.zeros(l_ref.shape, F32)
        acc_ref[...] = jnp.zeros(acc_ref.shape, F32)

    def attend(h, k, v, k0, width):
        slope = 2.0 ** (-8.0 * (h + 1) / DIFF_HEADS)
        q = q_ref[:, h * hw:(h + 1) * hw]
        qpos = past + lax.broadcasted_iota(jnp.int32, (tq, width), 0)
        kpos = k0 + lax.broadcasted_iota(jnp.int32, (tq, width), 1)
        bias = jnp.abs(qpos - kpos).astype(F32) * (-slope)
        vis = (kpos // CHUNK) <= (qpos // CHUNK)
        for j in range(2):
            s = _dot_nt(q[:, j * dh:(j + 1) * dh], k[:, j * dh:(j + 1) * dh]) * scale + bias
            s = jnp.where(vis, s, -jnp.inf)
            _softmax_step(j, s, v, m_ref.at[h], l_ref.at[h], acc_ref.at[h])

    @pl.when(ik < nkb)
    def _():
        for h in range(DIFF_HEADS):
            attend(h, kc_ref[:, h, :], vc_ref[:, h, :], ik * tk, tk)

    @pl.when(ik == nkb)
    def _():
        for h in range(DIFF_HEADS):
            hs = slice(h * hw, (h + 1) * hw)
            attend(h, kn_ref[:, hs], vn_ref[:, hs], past, tq)
            o_ref[:, hs] = _attn_finish(lam_ref, g_ref.at[h], l_ref.at[h], acc_ref.at[h], lambda_init)


def _attn_sample(qkv, o_prev, row_off, cache_k, cache_v, li, lam, g_subln, lambda_init, tk):
    n = qkv.shape[0]
    _, nb, past, nh, hw = cache_k.shape
    aw = nh * hw
    tq = (n - row_off) // nb
    tk = min(tk, past)
    assert row_off % tq == 0 and past % tk == 0
    nkb = past // tk
    blk0 = row_off // tq
    g3 = g_subln.reshape(nh, 1, hw)
    cache_spec = pl.BlockSpec((None, None, tk, nh, hw),
                              lambda b, ik: (li, b, jnp.minimum(ik, nkb - 1), 0, 0))
    return pl.pallas_call(
        functools.partial(_attn_sample_body, tk=tk, past=past, lambda_init=lambda_init),
        grid=(nb, nkb + 1),
        in_specs=[pl.BlockSpec((tq, aw), lambda b, ik: (blk0 + b, 0)),
                  cache_spec, cache_spec,
                  pl.BlockSpec((tq, aw), lambda b, ik: (blk0 + b, 1)),
                  pl.BlockSpec((tq, aw), lambda b, ik: (blk0 + b, 2)),
                  pl.BlockSpec((4, DIFF_DH), lambda b, ik: (0, 0)),
                  pl.BlockSpec((nh, 1, hw), lambda b, ik: (0, 0, 0)),
                  pl.BlockSpec(memory_space=pl.ANY)],
        out_specs=pl.BlockSpec((tq, aw), lambda b, ik: (blk0 + b, 0)),
        scratch_shapes=[pltpu.VMEM((nh, 2, tq, LANES), F32),
                        pltpu.VMEM((nh, 2, tq, LANES), F32),
                        pltpu.VMEM((nh, 2, tq, hw), F32)],
        out_shape=jax.ShapeDtypeStruct((n, aw), F32),
        input_output_aliases={7: 0},
        compiler_params=_cparams(("arbitrary", "arbitrary")),
        name="diff_attn_sample",
    )(qkv, cache_k, cache_v, qkv, qkv, lam, g3, o_prev)


def _pad_cols(a, width):
    return jnp.pad(a, ((0, 0), (0, width - a.shape[1])))


def _mixer_constants(dk, dv, w_gate_up, b_gate, g_gla, conv_w, conv_b, dt_bias, a_log, d_skip, g_ssd):
    C = REC_ROWS
    nh = GLA_HEADS
    inner = g_ssd.shape[0]
    n_ssd = inner // SSD_HEADDIM
    wup = jnp.pad(w_gate_up, ((0, LANES - w_gate_up.shape[0]), (0, 0)))
    tri = jnp.asarray(np.tril(np.ones((C, C), np.float32)), BF16)
    e64 = np.zeros((LANES, inner), np.float32)
    ec = np.zeros((LANES, n_ssd * C), np.float32)
    for hh in range(n_ssd):
        e64[hh, hh * SSD_HEADDIM:(hh + 1) * SSD_HEADDIM] = 1.0
        ec[hh, hh * C:(hh + 1) * C] = 1.0
    eye = np.tile(np.eye(C, dtype=np.float32), (1, n_ssd))
    caus = np.tile(np.tril(np.ones((C, C), np.float32)), (1, n_ssd))
    return [wup, b_gate.reshape(1, -1), g_gla.reshape(nh, dv), conv_w, conv_b.reshape(1, -1),
            _pad_cols(dt_bias.reshape(1, -1), LANES),
            _pad_cols(-jnp.exp(a_log.astype(F32)).reshape(1, -1), LANES),
            jnp.repeat(d_skip, SSD_HEADDIM).reshape(1, -1), g_ssd.reshape(1, -1),
            tri, jnp.asarray(e64, BF16), jnp.asarray(ec, BF16), jnp.asarray(eye), jnp.asarray(caus)]


def kernel(x_prompt, x_sample, state_gla, state_ssm, state_conv, cache_k, cache_v, norm_mix, norm_ffn, norm_final, w_in, w_gate_up, b_gate, g_gla, conv_w, conv_b, dt_bias, a_log, d_skip, g_ssd, w_out_mix, w_qkv, lam_q1, lam_k1, lam_q2, lam_k2, g_subln, w_o, router_group_w, router_group_b, router_expert_w, router_expert_b, w1, w3, w2):
    bp, tp, d = x_prompt.shape
    bs, ts, _ = x_sample.shape
    n_p, n_s = bp * tp, bs * ts
    n = n_p + n_s
    depth = norm_mix.shape[0]
    nh = GLA_HEADS
    dk, dv = state_gla.shape[-2], state_gla.shape[-1]
    qk_w, vw = nh * dk, nh * dv
    n_ssd = state_ssm.shape[2]
    inner = n_ssd * SSD_HEADDIM
    hpg = n_ssd // SSD_GROUPS
    gs = SSD_GROUPS * SSD_STATE
    cw = inner + 2 * gs
    rank = w_gate_up.shape[1]
    assert w_gate_up.shape[2] == qk_w and rank <= LANES and n_ssd <= LANES

    x = jnp.concatenate([x_prompt.reshape(n_p, d), x_sample.reshape(n_s, d)], axis=0)
    tm_row = 768 if n % 768 == 0 else 256

    outs = {}
    i_even = i_odd = 0
    gla_p, ssm_p, conv_p, gla_s, ssm_s, conv_s = [], [], [], [], [], []
    k_p, v_p, k_s, v_s = [], [], [], []
    y_p = y_s = None
    for layer in range(depth):
        final = layer == depth - 1
        if layer % 2 == 0:
            i = i_even
            i_even += 1
            offs = np.cumsum([0, qk_w, qk_w, vw, rank, vw, inner, cw, n_ssd])
            seg = lambda j: w_in[i][:, offs[j]:offs[j + 1]]
            w_cat = jnp.concatenate([seg(0), seg(1), seg(2), seg(4), seg(5), seg(6),
                                     _pad_cols(seg(3), LANES), _pad_cols(seg(7), LANES)], axis=1)
            tn = w_cat.shape[1] // 5 if (w_cat.shape[1] // 5) % LANES == 0 and w_cat.shape[1] % 5 == 0 else LANES
            proj = _norm_matmul(x, norm_mix[layer], w_cat, tm_row, tn)
            consts = _mixer_constants(dk, dv, w_gate_up[i], b_gate[i], g_gla[i], conv_w[i], conv_b[i],
                                      dt_bias[i], a_log[i], d_skip[i], g_ssd[i])

            def to_group_state(sm):
                b_ = sm.shape[0]
                return sm.reshape(b_, SSD_GROUPS, hpg, SSD_STATE, SSD_HEADDIM).transpose(0, 1, 3, 2, 4) \
                         .reshape(b_, SSD_GROUPS, SSD_STATE, hpg * SSD_HEADDIM)

            def from_group_state(sg):
                b_ = sg.shape[0]
                return sg.reshape(b_, SSD_GROUPS, SSD_STATE, hpg, SSD_HEADDIM).transpose(0, 1, 3, 2, 4) \
                         .reshape(b_, n_ssd, SSD_STATE, SSD_HEADDIM)

            def pad_conv(cv):
                return jnp.pad(cv, ((0, 0), (CONV_PAD - cv.shape[1], 0), (0, 0)))

            o_mix = jnp.zeros((n, 2 * vw), F32)
            o_mix, g_fin, s_fin, c_fin = _mixer_call(
                proj, o_mix, 0, bp, tp, jnp.zeros((bp, nh, dv, dk), F32),
                jnp.zeros((bp, SSD_GROUPS, SSD_STATE, hpg * SSD_HEADDIM), F32),
                jnp.zeros((bp, CONV_PAD, cw), F32), consts, n, dk, dv)
            gla_p.append(g_fin.transpose(0, 1, 3, 2))
            ssm_p.append(from_group_state(s_fin))
            conv_p.append(c_fin[:, CONV_PAD - (SSD_CONV - 1):])
            o_mix, g_fin, s_fin, c_fin = _mixer_call(
                proj, o_mix, n_p, bs, ts, state_gla[i].transpose(0, 1, 3, 2),
                to_group_state(state_ssm[i]), pad_conv(state_conv[i]), consts, n, dk, dv)
            gla_s.append(g_fin.transpose(0, 1, 3, 2))
            ssm_s.append(from_group_state(s_fin))
            conv_s.append(c_fin[:, CONV_PAD - (SSD_CONV - 1):])
            mix_in, w_proj = o_mix, w_out_mix[i]
        else:
            i = i_odd
            i_odd += 1
            lambda_init = 0.8 - 0.6 * math.exp(-0.3 * layer)
            qkv = _norm_matmul(x, norm_mix[layer], w_qkv[i], tm_row, 1024)
            aw = DIFF_HEADS * 2 * DIFF_DH
            lam = jnp.stack([lam_q1[i], lam_k1[i], lam_q2[i], lam_k2[i]])
            attn = jnp.zeros((n, aw), F32)
            tqk = min(512, tp)
            attn = _attn_prompt(qkv, attn, bp, tp, lam, g_subln[i], lambda_init, tqk, tqk)
            attn = _attn_sample(qkv, attn, n_p, cache_k, cache_v, i, lam, g_subln[i], lambda_init, 1024)
            k_all = qkv[:, aw:2 * aw]
            v_all = qkv[:, 2 * aw:3 * aw]
            k_p.append(k_all[:n_p].reshape(bp, tp, DIFF_HEADS, 2 * DIFF_DH))
            v_p.append(v_all[:n_p].reshape(bp, tp, DIFF_HEADS, 2 * DIFF_DH))
            k_s.append(k_all[n_p:].reshape(bs, ts, DIFF_HEADS, 2 * DIFF_DH))
            v_s.append(v_all[n_p:].reshape(bs, ts, DIFF_HEADS, 2 * DIFF_DH))
            mix_in, w_proj = attn, w_o[i]

        wr = _pad_cols(jnp.concatenate([router_group_w[layer], router_expert_w[layer]], axis=1), LANES)
        rbias = _pad_cols(jnp.concatenate([router_group_b[layer], router_expert_b[layer]]).reshape(1, -1), LANES)
        x1, h2, logits = _proj_resid(mix_in, w_proj, x, norm_ffn[layer], wr, 256)
        res = _moe(x1, h2, logits, rbias, w1, w3, w2, layer, norm_final, n_p, final)
        if final:
            y_p, y_s = res
        else:
            x = res[0]

    return (y_p.reshape(bp, tp, d), y_s.reshape(bs, ts, d),
            jnp.stack(gla_p), jnp.stack(ssm_p), jnp.stack(conv_p), jnp.stack(k_p), jnp.stack(v_p),
            jnp.stack(gla_s), jnp.stack(ssm_s), jnp.stack(conv_s), jnp.stack(k_s), jnp.stack(v_s))
```

```python
import functools
import math

import numpy as np
import jax
import jax.numpy as jnp
from jax import lax
from jax.experimental import pallas as pl
from jax.experimental.pallas import tpu as pltpu

F32 = jnp.float32
BF16 = jnp.bfloat16

EPS = 1e-6
CHUNK = 64
GLA_HEADS = 4
GLA_TAU = 16.0
SSD_HEADDIM = 64
SSD_STATE = 128
SSD_GROUPS = 4
SSD_CONV = 4
DIFF_HEADS = 8
DIFF_DH = 128
MOE_GROUPS = 4
MOE_PER_GROUP = 8
N_EXPERTS = MOE_GROUPS * MOE_PER_GROUP

LANES = 128
SUBLANES = 8
REC_ROWS = 128
STRIP = 16
CONV_PAD = 8
VMEM_LIMIT = 56 * 1024 * 1024


def _cparams(sem, vmem=VMEM_LIMIT):
    return pltpu.CompilerParams(dimension_semantics=sem, vmem_limit_bytes=vmem)


def _dot(a, b):
    return jnp.dot(a, b, preferred_element_type=F32)


def _dot_nt(a, b):
    return lax.dot_general(a, b, (((1,), (1,)), ((), ())), preferred_element_type=F32)


def _dot_tn(a, b):
    return lax.dot_general(a, b, (((0,), (0,)), ((), ())), preferred_element_type=F32)


def _split_hi_lo(a):
    hi = a.astype(BF16)
    lo = (a - hi.astype(F32)).astype(BF16)
    return hi, lo


def _exact_left(m, a):
    hi, lo = _split_hi_lo(a)
    return _dot(m, hi) + _dot(m, lo)


def _exact_right(a, m):
    hi, lo = _split_hi_lo(a)
    return _dot(hi, m) + _dot(lo, m)


def _silu(x):
    return x / (1.0 + jnp.exp(-x))


def _log1p_exp_neg_abs(x):
    e = jnp.exp(-jnp.abs(x))
    u = 1.0 + e
    return jnp.where(u == 1.0, e, jnp.log(u) * (e / (u - 1.0)))


def _rms(x, g):
    return x * lax.rsqrt(jnp.mean(x * x, axis=-1, keepdims=True) + EPS) * g


def _norm_matmul_body(x_ref, g_ref, w_ref, o_ref, h_ref):
    @pl.when(pl.program_id(1) == 0)
    def _():
        h_ref[...] = _rms(x_ref[...], g_ref[...])

    o_ref[...] = _dot(h_ref[...], w_ref[...])


def _norm_matmul(x, g, w, tm, tn):
    n, d = x.shape
    nout = w.shape[1]
    assert n % tm == 0 and nout % tn == 0
    return pl.pallas_call(
        _norm_matmul_body,
        grid=(n // tm, nout // tn),
        in_specs=[pl.BlockSpec((tm, d), lambda i, j: (i, 0), pipeline_mode=pl.Buffered(1)),
                  pl.BlockSpec((1, d), lambda i, j: (0, 0)),
                  pl.BlockSpec((d, tn), lambda i, j: (0, j))],
        out_specs=pl.BlockSpec((tm, tn), lambda i, j: (i, j)),
        out_shape=jax.ShapeDtypeStruct((n, nout), F32),
        scratch_shapes=[pltpu.VMEM((tm, d), F32)],
        compiler_params=_cparams(("arbitrary", "arbitrary")),
        name="norm_matmul",
    )(x, g.reshape(1, d), w)


def _mixer_body(p_ref, gla0_ref, ssm0_ref, conv0_ref, wup_ref, bgate_ref, ggla_ref, convw_ref,
                convb_ref, dtb_ref, aneg_ref, dskip_ref, gssd_ref, tri_ref, e64_ref, ec_ref,
                eye_ref, caus_ref,
                o_ref, gla_ref, ssm_ref, ctail_ref,
                sg_ref, ss_ref, ext_ref, b_ref, *, rb, dk, dv):
    C = REC_ROWS
    c = pl.program_id(1)
    nc = pl.num_programs(1)
    nh = GLA_HEADS
    qk_w = nh * dk
    vw = nh * dv
    inner = vw
    gs = SSD_GROUPS * SSD_STATE
    o_q, o_k, o_v = 0, qk_w, 2 * qk_w
    o_r = o_v + vw
    o_z = o_r + vw
    o_x = o_z + inner
    o_g = o_x + inner + 2 * gs
    o_dt = o_g + LANES

    @pl.when(c == 0)
    def _init():
        sg_ref[...] = gla0_ref[...]
        ss_ref[...] = ssm0_ref[...]
        ext_ref[0:CONV_PAD, :] = conv0_ref[...]

    p = p_ref[...]
    if rb < C:
        p = jnp.concatenate([p, jnp.zeros((C - rb, p.shape[1]), F32)], axis=0)

    def rowmask(width):
        return lax.broadcasted_iota(jnp.int32, (C, width), 0) < rb

    q = p[:, o_q:o_q + qk_w] * (dk ** -0.5)
    k = p[:, o_k:o_k + qk_w]
    v = p[:, o_v:o_v + vw]
    r = p[:, o_r:o_r + vw]
    z = p[:, o_z:o_z + inner]
    xbc = p[:, o_x:o_x + inner + 2 * gs]
    glr = p[:, o_g:o_g + LANES]
    dtp = p[:, o_dt:o_dt + LANES]

    tri = tri_ref[...]

    zg = _dot(glr, wup_ref[...]) + bgate_ref[...]
    la = (jnp.minimum(zg, 0.0) - _log1p_exp_neg_abs(zg)) * (1.0 / GLA_TAU)
    if rb < C:
        la = jnp.where(rowmask(qk_w), la, 0.0)
    bcum = _exact_left(tri, la)
    b_ref[...] = bcum

    row_i = lax.broadcasted_iota(jnp.int32, (C, dk), 0)
    prow = lax.broadcasted_iota(jnp.int32, (STRIP, C), 0)
    pcol = lax.broadcasted_iota(jnp.int32, (STRIP, C), 1)
    o_heads = []
    for h in range(nh):
        hs = slice(h * dk, (h + 1) * dk)
        vs = slice(h * dv, (h + 1) * dv)
        bh = bcum[:, hs]
        qh = q[:, hs]
        kh = k[:, hs]
        vh = v[:, vs]
        strips = []
        for i in range(C // STRIP):
            r0 = i * STRIP
            if i == 0:
                ref_row = jnp.zeros((1, dk), F32)
            else:
                ref_row = b_ref[pl.ds(r0 - 1, 1), hs]
            q_i = qh[r0:r0 + STRIP] * jnp.exp(bh[r0:r0 + STRIP] - ref_row)
            e = jnp.where(row_i < r0 + STRIP, ref_row - bh, 0.0)
            k_i = kh * jnp.exp(e)
            s_i = _dot_nt(q_i, k_i)
            strips.append(jnp.where(pcol <= prow + r0, s_i, 0.0))
        pmat = jnp.concatenate(strips, axis=0)
        b_last = b_ref[pl.ds(C - 1, 1), hs]
        st = sg_ref[h]
        o_h = _dot(pmat, vh) + _dot_nt(qh * jnp.exp(bh), st)
        k_st = kh * jnp.exp(b_last - bh)
        sg_ref[h] = st * jnp.exp(b_last) + _dot_tn(vh, k_st)
        o_h = _rms(o_h, ggla_ref[pl.ds(h, 1), :]) * _silu(r[:, vs])
        o_heads.append(o_h)
    o_a = jnp.concatenate(o_heads, axis=1)

    cw = inner + 2 * gs
    ext_ref[CONV_PAD:CONV_PAD + C, :] = xbc
    conv = convb_ref[...]
    for j in range(SSD_CONV):
        conv = conv + convw_ref[pl.ds(j, 1), :] * ext_ref[pl.ds(CONV_PAD - (SSD_CONV - 1) + j, C), :]
    xc = _silu(conv)
    xs = xc[:, :inner]
    bm = xc[:, inner:inner + gs]
    cm = xc[:, inner + gs:cw]
    dtv = dtp + dtb_ref[...]
    dt = jnp.maximum(dtv, 0.0) + _log1p_exp_neg_abs(dtv)
    ld = dt * aneg_ref[...]
    if rb < C:
        dt = jnp.where(rowmask(LANES), dt, 0.0)
        ld = jnp.where(rowmask(LANES), ld, 0.0)
    bs = _exact_left(tri, ld)
    e64 = e64_ref[...]
    dt64 = _exact_right(dt, e64)
    b64 = _exact_right(bs, e64)
    blast64 = b64[C - 1:C, :]
    xdt = xs * dt64
    bc = _exact_right(bs, ec_ref[...])
    br = jnp.sum(eye_ref[...] * bc, axis=0, keepdims=True)
    vis = caus_ref[...] > 0.0
    lf = jnp.where(vis, jnp.exp(jnp.where(vis, bc - br, 0.0)), 0.0)
    hpg = inner // SSD_HEADDIM // SSD_GROUPS
    gw = hpg * SSD_HEADDIM
    lane_head = lax.broadcasted_iota(jnp.int32, (C, gw), 1) // SSD_HEADDIM
    y_groups = []
    for g in range(SSD_GROUPS):
        cm_g = cm[:, g * SSD_STATE:(g + 1) * SSD_STATE]
        bm_g = bm[:, g * SSD_STATE:(g + 1) * SSD_STATE]
        gl = slice(g * gw, (g + 1) * gw)
        gmat = _dot_nt(cm_g, bm_g)
        a4 = jnp.concatenate([gmat] * hpg, axis=1) * lf[:, g * hpg * C:(g + 1) * hpg * C]
        xg = xdt[:, gl]
        x4 = jnp.concatenate([jnp.where(lane_head == hh, xg, 0.0) for hh in range(hpg)], axis=0)
        s_g = ss_ref[g]
        y_g = _dot(a4, x4) + _dot(cm_g, s_g) * jnp.exp(b64[:, gl])
        xw = xg * jnp.exp(blast64[:, gl] - b64[:, gl])
        ss_ref[g] = s_g * jnp.exp(blast64[:, gl]) + _dot_tn(bm_g, xw)
        y_groups.append(y_g)
    y = jnp.concatenate(y_groups, axis=1) + xs * dskip_ref[...]
    y = _rms(y * _silu(z), gssd_ref[...])

    o_full = jnp.concatenate([o_a, y], axis=1)
    o_ref[...] = o_full[:rb]

    @pl.when(c == nc - 1)
    def _fin():
        gla_ref[...] = sg_ref[...]
        ssm_ref[...] = ss_ref[...]
        ctail_ref[...] = ext_ref[pl.ds(rb, CONV_PAD), :]

    ext_ref[0:CONV_PAD, :] = ext_ref[pl.ds(rb, CONV_PAD), :]


def _mixer_call(proj, row_off, nb, t, gla0, ssm0, conv0, consts, dk, dv):
    C = REC_ROWS
    rb = min(t, C)
    assert t % rb == 0 and row_off % rb == 0
    steps = t // rb
    width = proj.shape[1]
    nh = GLA_HEADS
    vw = nh * dv
    cw = conv0.shape[-1]
    blk0 = row_off // rb

    def full(a):
        nd = a.ndim
        return pl.BlockSpec(a.shape, lambda b, c, _n=nd: (0,) * _n)

    in_specs = [pl.BlockSpec((rb, width), lambda b, c: (blk0 + b * steps + c, 0)),
                pl.BlockSpec((None, nh, dv, dk), lambda b, c: (b, 0, 0, 0)),
                pl.BlockSpec((None, SSD_GROUPS, SSD_STATE, ssm0.shape[-1]), lambda b, c: (b, 0, 0, 0)),
                pl.BlockSpec((None, CONV_PAD, cw), lambda b, c: (b, 0, 0))]
    in_specs += [full(a) for a in consts]
    operands = [proj, gla0, ssm0, conv0, *consts]
    out_specs = [pl.BlockSpec((rb, 2 * vw), lambda b, c: (b * steps + c, 0)),
                 pl.BlockSpec((None, nh, dv, dk), lambda b, c: (b, 0, 0, 0)),
                 pl.BlockSpec((None, SSD_GROUPS, SSD_STATE, ssm0.shape[-1]), lambda b, c: (b, 0, 0, 0)),
                 pl.BlockSpec((None, CONV_PAD, cw), lambda b, c: (b, 0, 0))]
    out_shape = [jax.ShapeDtypeStruct((nb * t, 2 * vw), F32),
                 jax.ShapeDtypeStruct(gla0.shape, F32),
                 jax.ShapeDtypeStruct(ssm0.shape, F32),
                 jax.ShapeDtypeStruct((nb, CONV_PAD, cw), F32)]
    return pl.pallas_call(
        functools.partial(_mixer_body, rb=rb, dk=dk, dv=dv),
        grid=(nb, steps),
        in_specs=in_specs,
        out_specs=out_specs,
        out_shape=out_shape,
        scratch_shapes=[pltpu.VMEM((nh, dv, dk), F32),
                        pltpu.VMEM((SSD_GROUPS, SSD_STATE, ssm0.shape[-1]), F32),
                        pltpu.VMEM((C + CONV_PAD, cw), F32),
                        pltpu.VMEM((C, nh * dk), F32)],
        compiler_params=_cparams(("arbitrary", "arbitrary")),
        name="gla_ssd_mixer",
    )(*operands)


def _proj_resid_body(oa_ref, ob_ref, w_ref, x_ref, g_ref, wr_ref, x1_ref, h_ref, lg_ref, *, n_first):
    o = jnp.where(pl.program_id(0) < n_first, oa_ref[...], ob_ref[...])
    x1 = x_ref[...] + _dot(o, w_ref[...])
    x1_ref[...] = x1
    h = _rms(x1, g_ref[...])
    h_ref[...] = h
    lg_ref[...] = _dot(h, wr_ref[...])


def _proj_resid(o_a, o_b, w, x, g, wr, tm):
    n, d = x.shape
    kin = o_a.shape[1]
    assert o_a.shape[0] % tm == 0 and o_b.shape[0] % tm == 0
    n_first = o_a.shape[0] // tm
    return pl.pallas_call(
        functools.partial(_proj_resid_body, n_first=n_first),
        grid=(n // tm,),
        in_specs=[pl.BlockSpec((tm, kin), lambda i: (jnp.minimum(i, n_first - 1), 0)),
                  pl.BlockSpec((tm, kin), lambda i: (jnp.maximum(i - n_first, 0), 0)),
                  pl.BlockSpec((kin, d), lambda i: (0, 0), pipeline_mode=pl.Buffered(1)),
                  pl.BlockSpec((tm, d), lambda i: (i, 0)),
                  pl.BlockSpec((1, d), lambda i: (0, 0)),
                  pl.BlockSpec((d, LANES), lambda i: (0, 0))],
        out_specs=[pl.BlockSpec((tm, d), lambda i: (i, 0)),
                   pl.BlockSpec((tm, d), lambda i: (i, 0)),
                   pl.BlockSpec((tm, LANES), lambda i: (i, 0))],
        out_shape=[jax.ShapeDtypeStruct((n, d), F32),
                   jax.ShapeDtypeStruct((n, d), F32),
                   jax.ShapeDtypeStruct((n, LANES), F32)],
        compiler_params=_cparams(("arbitrary",)),
        name="proj_resid_norm_router",
    )(o_a, o_b, w, x, g.reshape(1, d), wr)


_META_E0, _META_E1, _META_G0, _META_G1, _META_R0, _META_R1 = range(6)
_EXP_LANE0 = MOE_GROUPS


def _route_body(lg_ref, bias_ref, ltri_ref, meta_ref, cnt_ref, base_ref):
    i = pl.program_id(0)

    @pl.when(i == 0)
    def _():
        base_ref[...] = jnp.zeros_like(base_ref)

    lg = lg_ref[...] + bias_ref[...]
    tm = lg.shape[0]
    lane_i = lax.broadcasted_iota(jnp.int32, (tm, LANES), 1)
    lane = lane_i.astype(F32)
    neg = -jnp.inf
    glog = jnp.where(lane_i < MOE_GROUPS, lg, neg)
    gmax = jnp.max(glog, axis=1, keepdims=True)
    gsel = jnp.min(jnp.where(glog == gmax, lane, float(LANES)), axis=1, keepdims=True)
    pg = 1.0 / jnp.sum(jnp.exp(glog - gmax), axis=1, keepdims=True)
    lo = _EXP_LANE0 + MOE_PER_GROUP * gsel
    el = jnp.where((lane >= lo) & (lane < lo + MOE_PER_GROUP), lg, neg)
    v1 = jnp.max(el, axis=1, keepdims=True)
    i1 = jnp.min(jnp.where(el == v1, lane, float(LANES)), axis=1, keepdims=True)
    el2 = jnp.where(lane == i1, neg, el)
    v2 = jnp.max(el2, axis=1, keepdims=True)
    i2 = jnp.min(jnp.where(el2 == v2, lane, float(LANES)), axis=1, keepdims=True)
    e = jnp.exp(v2 - v1)
    g1 = pg / (1.0 + e)
    g2 = pg * e / (1.0 + e)
    hot1 = lane == i1
    hot2 = lane == i2
    onehot = jnp.where(hot1 | hot2, 1.0, 0.0)
    before = _dot(ltri_ref[...], onehot.astype(BF16)) + base_ref[0:1, :]
    r1 = jnp.sum(jnp.where(hot1, before, 0.0), axis=1, keepdims=True)
    r2 = jnp.sum(jnp.where(hot2, before, 0.0), axis=1, keepdims=True)
    base_ref[0:1, :] = base_ref[0:1, :] + jnp.sum(onehot, axis=0, keepdims=True)
    meta = jnp.zeros((tm, LANES), F32)
    for idx, val in ((_META_E0, i1 - _EXP_LANE0), (_META_E1, i2 - _EXP_LANE0),
                     (_META_G0, g1), (_META_G1, g2), (_META_R0, r1), (_META_R1, r2)):
        meta = jnp.where(lane_i == idx, val, meta)
    meta_ref[...] = meta

    @pl.when(i == pl.num_programs(0) - 1)
    def _():
        cnt_ref[...] = base_ref[...]


def _route(logits, bias, tm):
    n = logits.shape[0]
    ltri = jnp.asarray(np.tril(np.ones((tm, tm), np.float32), -1), BF16)
    return pl.pallas_call(
        _route_body,
        grid=(n // tm,),
        in_specs=[pl.BlockSpec((tm, LANES), lambda i: (i, 0)),
                  pl.BlockSpec((1, LANES), lambda i: (0, 0)),
                  pl.BlockSpec((tm, tm), lambda i: (0, 0))],
        out_specs=[pl.BlockSpec((tm, LANES), lambda i: (i, 0)),
                   pl.BlockSpec((8, LANES), lambda i: (0, 0))],
        out_shape=[jax.ShapeDtypeStruct((n, LANES), F32),
                   jax.ShapeDtypeStruct((8, LANES), F32)],
        scratch_shapes=[pltpu.VMEM((8, LANES), F32)],
        compiler_params=_cparams(("arbitrary",)),
        name="moe_route",
    )(logits, bias, ltri)


def _dispatch_body(dest_ref, pad0_ref, npad_ref, nb_ref, h_ref, xs_ref, zbuf, sem, zsem, *, tm, tm_e, nblk):
    i = pl.program_id(0)

    def pad_fill(wait):
        def go(cp):
            if wait:
                cp.wait()
            else:
                cp.start()

        def body(e, carry):
            off = pad0_ref[e]
            npad = npad_ref[e]
            head = (SUBLANES - off % SUBLANES) % SUBLANES
            for j in range(SUBLANES - 1):
                @pl.when(j < head)
                def _(j=j):
                    go(pltpu.make_async_copy(zbuf.at[pl.ds(0, 1), :], xs_ref.at[pl.ds(off + j, 1), :], zsem))
            off = pl.multiple_of(off + head, SUBLANES)
            rem = npad - head
            bit = tm_e // 2
            while bit >= SUBLANES:
                on = (rem & bit) != 0

                @pl.when(on)
                def _(off=off, bit=bit):
                    go(pltpu.make_async_copy(zbuf.at[pl.ds(0, bit), :], xs_ref.at[pl.ds(off, bit), :], zsem))

                off = pl.multiple_of(off + jnp.where(on, bit, 0), SUBLANES)
                bit //= 2
            return carry
        lax.fori_loop(0, N_EXPERTS, body, 0)

        def tail(tb, carry):
            go(pltpu.make_async_copy(zbuf, xs_ref.at[pl.ds(pl.multiple_of(tb * tm_e, tm_e), tm_e), :], zsem))
            return carry
        lax.fori_loop(nb_ref[0], nblk, tail, 0)

    @pl.when(i == 0)
    def _():
        zbuf[...] = jnp.zeros(zbuf.shape, F32)
        pad_fill(False)

    for r in range(tm):
        for kk in range(2):
            d = dest_ref[(i * tm + r) * 2 + kk]
            pltpu.make_async_copy(h_ref.at[pl.ds(r, 1), :], xs_ref.at[pl.ds(d, 1), :], sem).start(
                priority=(r + kk) % 2)

    @pl.when(i == 0)
    def _():
        pad_fill(True)

    for _ in range(2 * tm):
        pltpu.make_async_copy(h_ref.at[pl.ds(0, 1), :], xs_ref.at[pl.ds(0, 1), :], sem).wait()


def _dispatch(h, dest_flat, pad0, npad, nb_used, nblk, tm, tm_e):
    n, d = h.shape
    return pl.pallas_call(
        functools.partial(_dispatch_body, tm=tm, tm_e=tm_e, nblk=nblk),
        grid_spec=pltpu.PrefetchScalarGridSpec(
            num_scalar_prefetch=4,
            grid=(n // tm,),
            in_specs=[pl.BlockSpec((tm, d), lambda i, *_: (i, 0))],
            out_specs=pl.BlockSpec(memory_space=pl.ANY),
            scratch_shapes=[pltpu.VMEM((tm_e, d), F32),
                            pltpu.SemaphoreType.DMA(()),
                            pltpu.SemaphoreType.DMA(())]),
        out_shape=jax.ShapeDtypeStruct((nblk * tm_e, d), F32),
        compiler_params=_cparams(("arbitrary",)),
        name="moe_dispatch",
    )(dest_flat, pad0, npad, nb_used, h)


def _experts_body(be_ref, nb_ref, first_ref, nxt_ref, slot_ref, x_ref, w1_hbm, w3_hbm, w2_hbm, y_ref,
                  wb1, wb3, wb2, sem, *, layer):
    b = pl.program_id(0)
    used = b < nb_ref[0]
    s = slot_ref[b]

    def weight_copies(e, slot):
        return (pltpu.make_async_copy(w1_hbm.at[layer, e], wb1.at[slot], sem.at[slot, 0]),
                pltpu.make_async_copy(w3_hbm.at[layer, e], wb3.at[slot], sem.at[slot, 1]),
                pltpu.make_async_copy(w2_hbm.at[layer, e], wb2.at[slot], sem.at[slot, 2]))

    @pl.when(b == 0)
    def _():
        for cp in weight_copies(be_ref[0], 0):
            cp.start()

    @pl.when(jnp.logical_and(used, first_ref[b] == 1))
    def _():
        for cp in weight_copies(be_ref[b], s):
            cp.wait()

        @pl.when(nxt_ref[b] >= 0)
        def _():
            for cp in weight_copies(nxt_ref[b], 1 - s):
                cp.start()

    @pl.when(used)
    def _():
        x = x_ref[...]
        a = _dot(x, wb1[s])
        g = _dot(x, wb3[s])
        y_ref[...] = _dot(_silu(a) * g, wb2[s])

    @pl.when(jnp.logical_not(used))
    def _():
        y_ref[...] = jnp.zeros(y_ref.shape, F32)


def _experts(xs, block_e, nb_used, first, nxt_e, slot, w1, w3, w2, layer, tm):
    s, d = xs.shape
    f = w1.shape[-1]
    nblk = s // tm
    return pl.pallas_call(
        functools.partial(_experts_body, layer=layer),
        grid_spec=pltpu.PrefetchScalarGridSpec(
            num_scalar_prefetch=5,
            grid=(nblk,),
            in_specs=[pl.BlockSpec((tm, d), lambda b, be, nb, *_: (jnp.minimum(b, nb[0] - 1), 0)),
                      pl.BlockSpec(memory_space=pl.ANY),
                      pl.BlockSpec(memory_space=pl.ANY),
                      pl.BlockSpec(memory_space=pl.ANY)],
            out_specs=pl.BlockSpec((tm, d), lambda b, *_: (b, 0)),
            scratch_shapes=[pltpu.VMEM((2, d, f), F32),
                            pltpu.VMEM((2, d, f), F32),
                            pltpu.VMEM((2, f, d), F32),
                            pltpu.SemaphoreType.DMA((2, 3))]),
        out_shape=jax.ShapeDtypeStruct((s, d), F32),
        compiler_params=_cparams(("arbitrary",)),
        name="moe_experts",
    )(block_e, nb_used, first, nxt_e, slot, xs, w1, w3, w2)


def _combine_body(dest_ref, x_ref, meta_ref, gfin_ref, ys_ref, *rest, tm, n_first, final):
    if final:
        o_a_ref, o_b_ref, buf, sem = rest
    else:
        o_a_ref, buf, sem = rest
        o_b_ref = None
    i = pl.program_id(0)
    nsteps = pl.num_programs(0)

    def issue(step, slot):
        for r in range(tm):
            for kk in range(2):
                d = dest_ref[(step * tm + r) * 2 + kk]
                pltpu.make_async_copy(ys_ref.at[pl.ds(d, 1), :], buf.at[slot, kk, pl.ds(r, 1), :],
                                      sem.at[slot]).start(priority=(r + kk) % 2)

    @pl.when(i == 0)
    def _():
        issue(0, 0)

    @pl.when(i + 1 < nsteps)
    def _():
        issue(i + 1, (i + 1) % 2)

    slot = i % 2
    for _ in range(2 * tm):
        pltpu.make_async_copy(ys_ref.at[pl.ds(0, 1), :], buf.at[slot, 0, pl.ds(0, 1), :], sem.at[slot]).wait()

    meta = meta_ref[...]
    g0 = meta[:, _META_G0:_META_G0 + 1]
    g1 = meta[:, _META_G1:_META_G1 + 1]
    out = x_ref[...] + (buf[slot, 0] * g0 + buf[slot, 1] * g1)
    if not final:
        o_a_ref[...] = out
    else:
        out = _rms(out, gfin_ref[...])

        @pl.when(i < n_first)
        def _():
            o_a_ref[...] = out

        @pl.when(i >= n_first)
        def _():
            o_b_ref[...] = out


def _combine(x, meta, dest_flat, ys, gfin, tm, n_first_rows, final):
    n, d = x.shape
    n_first = n_first_rows // tm
    if final:
        out_specs = [pl.BlockSpec((tm, d), lambda i, dest: (jnp.minimum(i, n_first - 1), 0)),
                     pl.BlockSpec((tm, d), lambda i, dest: (jnp.maximum(i - n_first, 0), 0))]
        out_shape = [jax.ShapeDtypeStruct((n_first_rows, d), F32),
                     jax.ShapeDtypeStruct((n - n_first_rows, d), F32)]
    else:
        out_specs = [pl.BlockSpec((tm, d), lambda i, dest: (i, 0))]
        out_shape = [jax.ShapeDtypeStruct((n, d), F32)]
    return pl.pallas_call(
        functools.partial(_combine_body, tm=tm, n_first=n_first, final=final),
        grid_spec=pltpu.PrefetchScalarGridSpec(
            num_scalar_prefetch=1,
            grid=(n // tm,),
            in_specs=[pl.BlockSpec((tm, d), lambda i, dest: (i, 0)),
                      pl.BlockSpec((tm, LANES), lambda i, dest: (i, 0)),
                      pl.BlockSpec((1, d), lambda i, dest: (0, 0)),
                      pl.BlockSpec(memory_space=pl.ANY)],
            out_specs=out_specs,
            scratch_shapes=[pltpu.VMEM((2, 2, tm, d), F32),
                            pltpu.SemaphoreType.DMA((2,))]),
        out_shape=out_shape,
        compiler_params=_cparams(("arbitrary",)),
        name="moe_combine_final" if final else "moe_combine",
    )(dest_flat, x, meta, gfin.reshape(1, d), ys)


def _moe(x1, h, logits, rbias, w1, w3, w2, layer, gfin, n_first_rows, final, tm_e=256, tm_t=128):
    n, d = h.shape
    meta, cnt = _route(logits, rbias, 256)
    eid = meta[:, _META_E0:_META_E1 + 1].astype(jnp.int32)
    rank = meta[:, _META_R0:_META_R1 + 1].astype(jnp.int32)
    counts = cnt[0, _EXP_LANE0:_EXP_LANE0 + N_EXPERTS].astype(jnp.int32)
    padded = (counts + tm_e - 1) // tm_e * tm_e
    p_end = jnp.cumsum(padded)
    p_start = p_end - padded
    dest = (p_start[eid] + rank).reshape(-1)
    s = n * 2
    nblk = (s + N_EXPERTS * (tm_e - 1) + tm_e - 1) // tm_e
    block_e = jnp.minimum(jnp.searchsorted(p_end, jnp.arange(nblk, dtype=jnp.int32) * tm_e, side='right'),
                          N_EXPERTS - 1).astype(jnp.int32)
    nb_used = (p_end[-1:] // tm_e).astype(jnp.int32)
    bidx = jnp.arange(nblk, dtype=jnp.int32)
    prev_e = jnp.concatenate([jnp.full((1,), -1, jnp.int32), block_e[:-1]])
    first = ((bidx < nb_used[0]) & (block_e != prev_e)).astype(jnp.int32)
    slot = ((jnp.cumsum(first) - 1) % 2).astype(jnp.int32)
    eidx = jnp.arange(N_EXPERTS, dtype=jnp.int32)
    live = jnp.where(counts > 0, eidx, N_EXPERTS)
    next_live = jnp.concatenate([lax.cummin(live[::-1])[::-1][1:], jnp.full((1,), N_EXPERTS, jnp.int32)])
    nxt_e = next_live[block_e]
    nxt_e = jnp.where(nxt_e < N_EXPERTS, nxt_e, -1).astype(jnp.int32)
    xs = _dispatch(h, dest, (p_start + counts).astype(jnp.int32), (padded - counts).astype(jnp.int32),
                   nb_used, nblk, tm_t, tm_e)
    ys = _experts(xs, block_e, nb_used, first, nxt_e, slot, w1, w3, w2, layer, tm_e)
    return _combine(x1, meta, dest, ys, gfin, tm_t, n_first_rows, final)


def _lambda_value(lam_ref, lambda_init):
    lam = lam_ref[...]
    s1 = jnp.sum(lam[0:1] * lam[1:2], axis=1, keepdims=True)
    s2 = jnp.sum(lam[2:3] * lam[3:4], axis=1, keepdims=True)
    return jnp.exp(s1) - jnp.exp(s2) + lambda_init


def _lane_tile(x, width):
    if width % LANES == 0:
        return jnp.concatenate([x] * (width // LANES), axis=1)
    return x[:, :width]


def _softmax_step(j, s, v, m_ref, l_ref, acc_ref):
    m_prev = m_ref[j]
    m_new = jnp.maximum(m_prev, jnp.max(s, axis=1, keepdims=True))
    alpha = jnp.exp(m_prev - m_new)
    p = jnp.exp(s - _lane_tile(m_new, s.shape[1]))
    l_ref[j] = alpha * l_ref[j] + jnp.sum(p, axis=1, keepdims=True)
    acc = acc_ref[j]
    acc_ref[j] = acc * _lane_tile(alpha, acc.shape[1]) + _dot(p, v)
    m_ref[j] = m_new


def _attn_finish(lam_ref, g_ref, l_ref, acc_ref, lambda_init):
    lam = _lambda_value(lam_ref, lambda_init)
    width = acc_ref.shape[-1]
    o = acc_ref[0] / _lane_tile(l_ref[0], width) - lam * (acc_ref[1] / _lane_tile(l_ref[1], width))
    return _rms(o, g_ref[...]) * (1.0 - lambda_init)


def _alibi_slopes():
    slopes = [2.0 ** (-8.0 * (h + 1) / DIFF_HEADS) for h in range(DIFF_HEADS)]
    assert all(math.frexp(s)[0] == 0.5 for s in slopes)
    return slopes


def _attn_prompt_body(iq_ref, ik_ref, slope_ref, q_ref, k_ref, v_ref, kt_ref, lam_ref, g_ref,
                      o_ref, qa_ref, m_ref, l_ref, acc_ref, *, tq, lambda_init):
    h = pl.program_id(1)
    t = pl.program_id(2)
    iq = iq_ref[t]
    ik = ik_ref[t]
    dh = DIFF_DH
    slope = slope_ref[h]

    @pl.when(ik == 0)
    def _():
        m_ref[...] = jnp.full(m_ref.shape, -jnp.inf, F32)
        l_ref[...] = jnp.zeros(l_ref.shape, F32)
        acc_ref[...] = jnp.zeros(acc_ref.shape, F32)
        qpos = iq * tq + lax.broadcasted_iota(jnp.int32, (tq, LANES), 0)
        lane = lax.broadcasted_iota(jnp.int32, (tq, LANES), 1)
        qa = (qpos // CHUNK).astype(F32)
        qb = (qpos % CHUNK).astype(F32)
        feat = jnp.where(lane == 0, qa * (-slope * CHUNK),
                         jnp.where(lane == 1, qb * (-slope), jnp.where(lane < 4, slope, 0.0)))
        q = q_ref[...]
        for j in range(2):
            qa_ref[j] = jnp.concatenate([q[:, j * dh:(j + 1) * dh] * (dh ** -0.5), feat], axis=1)

    k = k_ref[...]
    v = v_ref[...]
    kt = kt_ref[...]

    def scores(j):
        return _dot_nt(qa_ref[j], jnp.concatenate([k[:, j * dh:(j + 1) * dh], kt], axis=1))

    @pl.when(ik < iq)
    def _():
        for j in range(2):
            _softmax_step(j, scores(j), v, m_ref, l_ref, acc_ref)

    @pl.when(ik == iq)
    def _():
        row = lax.broadcasted_iota(jnp.int32, (tq, tq), 0)
        col = lax.broadcasted_iota(jnp.int32, (tq, tq), 1)
        fix = jnp.maximum(col - row, 0).astype(F32) * (-2.0 * slope)
        vis = (col // CHUNK) <= (row // CHUNK)
        for j in range(2):
            _softmax_step(j, jnp.where(vis, scores(j) + fix, -jnp.inf), v, m_ref, l_ref, acc_ref)
        o_ref[...] = _attn_finish(lam_ref, g_ref, l_ref, acc_ref, lambda_init)


def _attn_prompt(qkv, nb, t, lam, g_subln, lambda_init, tq):
    hw = 2 * DIFF_DH
    assert t % tq == 0 and tq % CHUNK == 0 and t // CHUNK <= 256
    nq = t // tq
    pairs = [(iq, ik) for iq in range(nq) for ik in range(iq + 1)]
    iq_tab = jnp.asarray(np.array([p[0] for p in pairs], np.int32))
    ik_tab = jnp.asarray(np.array([p[1] for p in pairs], np.int32))
    slopes = jnp.asarray(np.array(_alibi_slopes(), np.float32))
    pos = np.arange(t)
    ktab = np.zeros((t, LANES), np.float32)
    ktab[:, 0] = 1.0
    ktab[:, 1] = 1.0
    ktab[:, 2] = (pos // CHUNK) * CHUNK
    ktab[:, 3] = pos % CHUNK
    g3 = g_subln.reshape(DIFF_HEADS, 1, hw)
    return pl.pallas_call(
        functools.partial(_attn_prompt_body, tq=tq, lambda_init=lambda_init),
        grid_spec=pltpu.PrefetchScalarGridSpec(
            num_scalar_prefetch=3,
            grid=(nb, DIFF_HEADS, len(pairs)),
            in_specs=[pl.BlockSpec((tq, hw), lambda b, h, p, iqt, ikt, sl: (b * nq + iqt[p], h)),
                      pl.BlockSpec((tq, hw), lambda b, h, p, iqt, ikt, sl: (b * nq + ikt[p], DIFF_HEADS + h)),
                      pl.BlockSpec((tq, hw), lambda b, h, p, iqt, ikt, sl: (b * nq + ikt[p], 2 * DIFF_HEADS + h)),
                      pl.BlockSpec((tq, LANES), lambda b, h, p, iqt, ikt, sl: (ikt[p], 0)),
                      pl.BlockSpec((4, DIFF_DH), lambda b, h, p, iqt, ikt, sl: (0, 0)),
                      pl.BlockSpec((None, 1, hw), lambda b, h, p, iqt, ikt, sl: (h, 0, 0))],
            out_specs=pl.BlockSpec((tq, hw), lambda b, h, p, iqt, ikt, sl: (b * nq + iqt[p], h)),
            scratch_shapes=[pltpu.VMEM((2, tq, hw), F32),
                            pltpu.VMEM((2, tq, LANES), F32),
                            pltpu.VMEM((2, tq, LANES), F32),
                            pltpu.VMEM((2, tq, hw), F32)]),
        out_shape=jax.ShapeDtypeStruct((nb * t, DIFF_HEADS * hw), F32),
        compiler_params=_cparams(("arbitrary", "arbitrary", "arbitrary")),
        name="diff_attn_prompt",
    )(iq_tab, ik_tab, slopes, qkv, qkv, qkv, jnp.asarray(ktab), lam, g3)


def _attn_sample_body(q_ref, kc_ref, vc_ref, kn_ref, vn_ref, lam_ref, g_ref,
                      o_ref, m_ref, l_ref, acc_ref, *, tk, past, lambda_init):
    ik = pl.program_id(1)
    nkb = pl.num_programs(1) - 1
    dh = DIFF_DH
    hw = 2 * dh
    tq = q_ref.shape[0]
    scale = dh ** -0.5

    @pl.when(ik == 0)
    def _():
        m_ref[...] = jnp.full(m_ref.shape, -jnp.inf, F32)
        l_ref[...] = jnp.zeros(l_ref.shape, F32)
        acc_ref[...] = jnp.zeros(acc_ref.shape, F32)

    def attend(h, k, v, k0, width):
        slope = _alibi_slopes()[h]
        q = q_ref[:, h * hw:(h + 1) * hw]
        qpos = past + lax.broadcasted_iota(jnp.int32, (tq, width), 0)
        kpos = k0 + lax.broadcasted_iota(jnp.int32, (tq, width), 1)
        bias = jnp.abs(qpos - kpos).astype(F32) * (-slope)
        vis = (kpos // CHUNK) <= (qpos // CHUNK)
        for j in range(2):
            s = _dot_nt(q[:, j * dh:(j + 1) * dh], k[:, j * dh:(j + 1) * dh]) * scale + bias
            s = jnp.where(vis, s, -jnp.inf)
            _softmax_step(j, s, v, m_ref.at[h], l_ref.at[h], acc_ref.at[h])

    @pl.when(ik < nkb)
    def _():
        for h in range(DIFF_HEADS):
            attend(h, kc_ref[:, h, :], vc_ref[:, h, :], ik * tk, tk)

    @pl.when(ik == nkb)
    def _():
        for h in range(DIFF_HEADS):
            hs = slice(h * hw, (h + 1) * hw)
            attend(h, kn_ref[:, hs], vn_ref[:, hs], past, tq)
            o_ref[:, hs] = _attn_finish(lam_ref, g_ref.at[h], l_ref.at[h], acc_ref.at[h], lambda_init)


def _attn_sample(qkv, row_off, cache_k, cache_v, li, lam, g_subln, lambda_init, tk):
    n = qkv.shape[0]
    _, nb, past, nh, hw = cache_k.shape
    aw = nh * hw
    tq = (n - row_off) // nb
    tk = min(tk, past)
    assert row_off % tq == 0 and past % tk == 0
    nkb = past // tk
    blk0 = row_off // tq
    g3 = g_subln.reshape(nh, 1, hw)
    cache_spec = pl.BlockSpec((None, None, tk, nh, hw),
                              lambda b, ik: (li, b, jnp.minimum(ik, nkb - 1), 0, 0))
    return pl.pallas_call(
        functools.partial(_attn_sample_body, tk=tk, past=past, lambda_init=lambda_init),
        grid=(nb, nkb + 1),
        in_specs=[pl.BlockSpec((tq, aw), lambda b, ik: (blk0 + b, 0)),
                  cache_spec, cache_spec,
                  pl.BlockSpec((tq, aw), lambda b, ik: (blk0 + b, 1)),
                  pl.BlockSpec((tq, aw), lambda b, ik: (blk0 + b, 2)),
                  pl.BlockSpec((4, DIFF_DH), lambda b, ik: (0, 0)),
                  pl.BlockSpec((nh, 1, hw), lambda b, ik: (0, 0, 0))],
        out_specs=pl.BlockSpec((tq, aw), lambda b, ik: (b, 0)),
        scratch_shapes=[pltpu.VMEM((nh, 2, tq, LANES), F32),
                        pltpu.VMEM((nh, 2, tq, LANES), F32),
                        pltpu.VMEM((nh, 2, tq, hw), F32)],
        out_shape=jax.ShapeDtypeStruct((n - row_off, aw), F32),
        compiler_params=_cparams(("arbitrary", "arbitrary")),
        name="diff_attn_sample",
    )(qkv, cache_k, cache_v, qkv, qkv, lam, g3)


def _pad_cols(a, width):
    return jnp.pad(a, ((0, 0), (0, width - a.shape[1])))


def _mixer_constants(dk, dv, w_gate_up, b_gate, g_gla, conv_w, conv_b, dt_bias, a_log, d_skip, g_ssd):
    C = REC_ROWS
    nh = GLA_HEADS
    inner = g_ssd.shape[0]
    n_ssd = inner // SSD_HEADDIM
    wup = jnp.pad(w_gate_up, ((0, LANES - w_gate_up.shape[0]), (0, 0)))
    tri = jnp.asarray(np.tril(np.ones((C, C), np.float32)), BF16)
    e64 = np.zeros((LANES, inner), np.float32)
    ec = np.zeros((LANES, n_ssd * C), np.float32)
    for hh in range(n_ssd):
        e64[hh, hh * SSD_HEADDIM:(hh + 1) * SSD_HEADDIM] = 1.0
        ec[hh, hh * C:(hh + 1) * C] = 1.0
    eye = np.tile(np.eye(C, dtype=np.float32), (1, n_ssd))
    caus = np.tile(np.tril(np.ones((C, C), np.float32)), (1, n_ssd))
    return [wup, b_gate.reshape(1, -1), g_gla.reshape(nh, dv), conv_w, conv_b.reshape(1, -1),
            _pad_cols(dt_bias.reshape(1, -1), LANES),
            _pad_cols(-jnp.exp(a_log.astype(F32)).reshape(1, -1), LANES),
            jnp.repeat(d_skip, SSD_HEADDIM).reshape(1, -1), g_ssd.reshape(1, -1),
            tri, jnp.asarray(e64, BF16), jnp.asarray(ec, BF16), jnp.asarray(eye), jnp.asarray(caus)]


def kernel(x_prompt, x_sample, state_gla, state_ssm, state_conv, cache_k, cache_v, norm_mix, norm_ffn, norm_final, w_in, w_gate_up, b_gate, g_gla, conv_w, conv_b, dt_bias, a_log, d_skip, g_ssd, w_out_mix, w_qkv, lam_q1, lam_k1, lam_q2, lam_k2, g_subln, w_o, router_group_w, router_group_b, router_expert_w, router_expert_b, w1, w3, w2):
    bp, tp, d = x_prompt.shape
    bs, ts, _ = x_sample.shape
    n_p, n_s = bp * tp, bs * ts
    n = n_p + n_s
    depth = norm_mix.shape[0]
    nh = GLA_HEADS
    dk, dv = state_gla.shape[-2], state_gla.shape[-1]
    qk_w, vw = nh * dk, nh * dv
    n_ssd = state_ssm.shape[2]
    inner = n_ssd * SSD_HEADDIM
    hpg = n_ssd // SSD_GROUPS
    gs = SSD_GROUPS * SSD_STATE
    cw = inner + 2 * gs
    rank = w_gate_up.shape[1]
    assert w_gate_up.shape[2] == qk_w and rank <= LANES and n_ssd <= LANES

    x = jnp.concatenate([x_prompt.reshape(n_p, d), x_sample.reshape(n_s, d)], axis=0)
    tm_row = next(tm for tm in (1408, 768, 256) if n % tm == 0)

    i_even = i_odd = 0
    gla_p, ssm_p, conv_p, gla_s, ssm_s, conv_s = [], [], [], [], [], []
    k_p, v_p, k_s, v_s = [], [], [], []
    y_p = y_s = None
    for layer in range(depth):
        final = layer == depth - 1
        if layer % 2 == 0:
            i = i_even
            i_even += 1
            offs = np.cumsum([0, qk_w, qk_w, vw, rank, vw, inner, cw, n_ssd])
            seg = lambda j: w_in[i][:, offs[j]:offs[j + 1]]
            w_cat = jnp.concatenate([seg(0), seg(1), seg(2), seg(4), seg(5), seg(6),
                                     _pad_cols(seg(3), LANES), _pad_cols(seg(7), LANES)], axis=1)
            proj = _norm_matmul(x, norm_mix[layer], w_cat, tm_row, 640)
            consts = _mixer_constants(dk, dv, w_gate_up[i], b_gate[i], g_gla[i], conv_w[i], conv_b[i],
                                      dt_bias[i], a_log[i], d_skip[i], g_ssd[i])

            def to_group_state(sm):
                b_ = sm.shape[0]
                return sm.reshape(b_, SSD_GROUPS, hpg, SSD_STATE, SSD_HEADDIM).transpose(0, 1, 3, 2, 4) \
                         .reshape(b_, SSD_GROUPS, SSD_STATE, hpg * SSD_HEADDIM)

            def from_group_state(sg):
                b_ = sg.shape[0]
                return sg.reshape(b_, SSD_GROUPS, SSD_STATE, hpg, SSD_HEADDIM).transpose(0, 1, 3, 2, 4) \
                         .reshape(b_, n_ssd, SSD_STATE, SSD_HEADDIM)

            def pad_conv(cv):
                return jnp.pad(cv, ((0, 0), (CONV_PAD - cv.shape[1], 0), (0, 0)))

            mix_a, g_fin, s_fin, c_fin = _mixer_call(
                proj, 0, bp, tp, jnp.zeros((bp, nh, dv, dk), F32),
                jnp.zeros((bp, SSD_GROUPS, SSD_STATE, hpg * SSD_HEADDIM), F32),
                jnp.zeros((bp, CONV_PAD, cw), F32), consts, dk, dv)
            gla_p.append(g_fin.transpose(0, 1, 3, 2))
            ssm_p.append(from_group_state(s_fin))
            conv_p.append(c_fin[:, CONV_PAD - (SSD_CONV - 1):])
            mix_b, g_fin, s_fin, c_fin = _mixer_call(
                proj, n_p, bs, ts, state_gla[i].transpose(0, 1, 3, 2),
                to_group_state(state_ssm[i]), pad_conv(state_conv[i]), consts, dk, dv)
            gla_s.append(g_fin.transpose(0, 1, 3, 2))
            ssm_s.append(from_group_state(s_fin))
            conv_s.append(c_fin[:, CONV_PAD - (SSD_CONV - 1):])
            w_proj = w_out_mix[i]
        else:
            i = i_odd
            i_odd += 1
            lambda_init = 0.8 - 0.6 * math.exp(-0.3 * layer)
            qkv = _norm_matmul(x, norm_mix[layer], w_qkv[i], tm_row, 768)
            aw = DIFF_HEADS * 2 * DIFF_DH
            lam = jnp.stack([lam_q1[i], lam_k1[i], lam_q2[i], lam_k2[i]])
            mix_a = _attn_prompt(qkv, bp, tp, lam, g_subln[i], lambda_init, min(512, tp))
            mix_b = _attn_sample(qkv, n_p, cache_k, cache_v, i, lam, g_subln[i], lambda_init, 1024)
            k_all = qkv[:, aw:2 * aw]
            v_all = qkv[:, 2 * aw:3 * aw]
            k_p.append(k_all[:n_p].reshape(bp, tp, DIFF_HEADS, 2 * DIFF_DH))
            v_p.append(v_all[:n_p].reshape(bp, tp, DIFF_HEADS, 2 * DIFF_DH))
            k_s.append(k_all[n_p:].reshape(bs, ts, DIFF_HEADS, 2 * DIFF_DH))
            v_s.append(v_all[n_p:].reshape(bs, ts, DIFF_HEADS, 2 * DIFF_DH))
            w_proj = w_o[i]

        wr = _pad_cols(jnp.concatenate([router_group_w[layer], router_expert_w[layer]], axis=1), LANES)
        rbias = _pad_cols(jnp.concatenate([router_group_b[layer], router_expert_b[layer]]).reshape(1, -1), LANES)
        x1, h2, logits = _proj_resid(mix_a, mix_b, w_proj, x, norm_ffn[layer], wr, 256)
        res = _moe(x1, h2, logits, rbias, w1, w3, w2, layer, norm_final, n_p, final)
        if final:
            y_p, y_s = res
        else:
            x = res[0]

    return (y_p.reshape(bp, tp, d), y_s.reshape(bs, ts, d),
            jnp.stack(gla_p), jnp.stack(ssm_p), jnp.stack(conv_p), jnp.stack(k_p), jnp.stack(v_p),
            jnp.stack(gla_s), jnp.stack(ssm_s), jnp.stack(conv_s), jnp.stack(k_s), jnp.stack(v_s))
```

```python
import functools
import math

import numpy as np
import jax
import jax.numpy as jnp
from jax import lax
from jax.experimental import pallas as pl
from jax.experimental.pallas import tpu as pltpu

F32 = jnp.float32
BF16 = jnp.bfloat16

EPS = 1e-6
CHUNK = 64
GLA_HEADS = 4
GLA_TAU = 16.0
SSD_HEADDIM = 64
SSD_STATE = 128
SSD_GROUPS = 4
SSD_CONV = 4
DIFF_HEADS = 8
DIFF_DH = 128
MOE_GROUPS = 4
MOE_PER_GROUP = 8
N_EXPERTS = MOE_GROUPS * MOE_PER_GROUP

LANES = 128
SUBLANES = 8
REC_ROWS = 128
STRIP = 16
CONV_PAD = 8
ATTN_ROW_SPLIT = 2
VMEM_LIMIT = 56 * 1024 * 1024


def _cparams(sem, vmem=VMEM_LIMIT):
    return pltpu.CompilerParams(dimension_semantics=sem, vmem_limit_bytes=vmem)


def _dot(a, b):
    return jnp.dot(a, b, preferred_element_type=F32)


def _dot_nt(a, b):
    return lax.dot_general(a, b, (((1,), (1,)), ((), ())), preferred_element_type=F32)


def _dot_tn(a, b):
    return lax.dot_general(a, b, (((0,), (0,)), ((), ())), preferred_element_type=F32)


def _split_hi_lo(a):
    hi = a.astype(BF16)
    lo = (a - hi.astype(F32)).astype(BF16)
    return hi, lo


def _exact_left(m, a):
    hi, lo = _split_hi_lo(a)
    return _dot(m, hi) + _dot(m, lo)


def _exact_right(a, m):
    hi, lo = _split_hi_lo(a)
    return _dot(hi, m) + _dot(lo, m)


def _silu(x):
    return x / (1.0 + jnp.exp(-x))


def _log1p_exp_neg_abs(x):
    e = jnp.exp(-jnp.abs(x))
    u = 1.0 + e
    return jnp.where(u == 1.0, e, jnp.log(u) * (e / (u - 1.0)))


def _rms(x, g):
    return x * lax.rsqrt(jnp.mean(x * x, axis=-1, keepdims=True) + EPS) * g


def _norm_matmul_body(x_ref, g_ref, w_ref, *refs, w_is_transposed, parts):
    o_refs, h_ref = refs[:parts], refs[parts]
    j = pl.program_id(1)

    @pl.when(j == 0)
    def _():
        h_ref[...] = _rms(x_ref[...], g_ref[...])

    res = (_dot_nt if w_is_transposed else _dot)(h_ref[...], w_ref[...])
    if parts == 1:
        o_refs[0][...] = res
    else:
        per = pl.num_programs(1) // parts
        for p in range(parts):
            @pl.when(j // per == p)
            def _(p=p):
                o_refs[p][...] = res


def _norm_matmul(x, g, w, tm, tn, *, row0=0, nrows=None, parts=1, w_is_transposed=False):
    n, d = x.shape
    nrows = n - row0 if nrows is None else nrows
    nout = w.shape[0] if w_is_transposed else w.shape[1]
    assert nrows % tm == 0 and row0 % tm == 0 and nout % (tn * parts) == 0
    blk0 = row0 // tm
    per = nout // tn // parts
    w_spec = (pl.BlockSpec((tn, d), lambda i, j: (j, 0)) if w_is_transposed
              else pl.BlockSpec((d, tn), lambda i, j: (0, j)))
    out_specs = [pl.BlockSpec((tm, tn), lambda i, j, p=p: (i, jnp.clip(j - p * per, 0, per - 1)))
                 for p in range(parts)]
    outs = pl.pallas_call(
        functools.partial(_norm_matmul_body, w_is_transposed=w_is_transposed, parts=parts),
        grid=(nrows // tm, nout // tn),
        in_specs=[pl.BlockSpec((tm, d), lambda i, j: (blk0 + i, 0), pipeline_mode=pl.Buffered(1)),
                  pl.BlockSpec((1, d), lambda i, j: (0, 0)),
                  w_spec],
        out_specs=out_specs,
        out_shape=[jax.ShapeDtypeStruct((nrows, nout // parts), F32)] * parts,
        scratch_shapes=[pltpu.VMEM((tm, d), F32)],
        compiler_params=_cparams(("arbitrary", "arbitrary")),
        name="norm_matmul",
    )(x, g.reshape(1, d), w)
    return outs[0] if parts == 1 else outs


def _mixer_body(p_ref, gla0_ref, ssm0_ref, conv0_ref, wup_ref, bgate_ref, ggla_ref, convw_ref,
                convb_ref, dtb_ref, aneg_ref, dskip_ref, gssd_ref, tri_ref, e64_ref, ec_ref,
                eye_ref, caus_ref,
                o_ref, gla_ref, ssm_ref, ctail_ref,
                sg_ref, ss_ref, ext_ref, b_ref, *, rb, dk, dv):
    C = REC_ROWS
    c = pl.program_id(1)
    nc = pl.num_programs(1)
    nh = GLA_HEADS
    qk_w = nh * dk
    vw = nh * dv
    inner = vw
    gs = SSD_GROUPS * SSD_STATE
    o_q, o_k, o_v = 0, qk_w, 2 * qk_w
    o_r = o_v + vw
    o_z = o_r + vw
    o_x = o_z + inner
    o_g = o_x + inner + 2 * gs
    o_dt = o_g + LANES

    @pl.when(c == 0)
    def _init():
        sg_ref[...] = gla0_ref[...]
        ss_ref[...] = ssm0_ref[...]
        ext_ref[0:CONV_PAD, :] = conv0_ref[...]

    p = p_ref[...]
    if rb < C:
        p = jnp.concatenate([p, jnp.zeros((C - rb, p.shape[1]), F32)], axis=0)

    def rowmask(width):
        return lax.broadcasted_iota(jnp.int32, (C, width), 0) < rb

    q = p[:, o_q:o_q + qk_w] * (dk ** -0.5)
    k = p[:, o_k:o_k + qk_w]
    v = p[:, o_v:o_v + vw]
    r = p[:, o_r:o_r + vw]
    z = p[:, o_z:o_z + inner]
    xbc = p[:, o_x:o_x + inner + 2 * gs]
    glr = p[:, o_g:o_g + LANES]
    dtp = p[:, o_dt:o_dt + LANES]

    tri = tri_ref[...]

    zg = _dot(glr, wup_ref[...]) + bgate_ref[...]
    la = (jnp.minimum(zg, 0.0) - _log1p_exp_neg_abs(zg)) * (1.0 / GLA_TAU)
    if rb < C:
        la = jnp.where(rowmask(qk_w), la, 0.0)
    bcum = _exact_left(tri, la)
    b_ref[...] = bcum

    row_i = lax.broadcasted_iota(jnp.int32, (C, dk), 0)
    prow = lax.broadcasted_iota(jnp.int32, (STRIP, C), 0)
    pcol = lax.broadcasted_iota(jnp.int32, (STRIP, C), 1)
    o_heads = []
    for h in range(nh):
        hs = slice(h * dk, (h + 1) * dk)
        vs = slice(h * dv, (h + 1) * dv)
        bh = bcum[:, hs]
        qh = q[:, hs]
        kh = k[:, hs]
        vh = v[:, vs]
        strips = []
        for i in range(C // STRIP):
            r0 = i * STRIP
            if i == 0:
                ref_row = jnp.zeros((1, dk), F32)
            else:
                ref_row = b_ref[pl.ds(r0 - 1, 1), hs]
            q_i = qh[r0:r0 + STRIP] * jnp.exp(bh[r0:r0 + STRIP] - ref_row)
            e = jnp.where(row_i < r0 + STRIP, ref_row - bh, 0.0)
            k_i = kh * jnp.exp(e)
            s_i = _dot_nt(q_i, k_i)
            strips.append(jnp.where(pcol <= prow + r0, s_i, 0.0))
        pmat = jnp.concatenate(strips, axis=0)
        b_last = b_ref[pl.ds(C - 1, 1), hs]
        st = sg_ref[h]
        o_h = _dot(pmat, vh) + _dot_nt(qh * jnp.exp(bh), st)
        k_st = kh * jnp.exp(b_last - bh)
        sg_ref[h] = st * jnp.exp(b_last) + _dot_tn(vh, k_st)
        o_h = _rms(o_h, ggla_ref[pl.ds(h, 1), :]) * _silu(r[:, vs])
        o_heads.append(o_h)
    o_a = jnp.concatenate(o_heads, axis=1)

    cw = inner + 2 * gs
    ext_ref[CONV_PAD:CONV_PAD + C, :] = xbc
    conv = convb_ref[...]
    for j in range(SSD_CONV):
        conv = conv + convw_ref[pl.ds(j, 1), :] * ext_ref[pl.ds(CONV_PAD - (SSD_CONV - 1) + j, C), :]
    xc = _silu(conv)
    xs = xc[:, :inner]
    bm = xc[:, inner:inner + gs]
    cm = xc[:, inner + gs:cw]
    dtv = dtp + dtb_ref[...]
    dt = jnp.maximum(dtv, 0.0) + _log1p_exp_neg_abs(dtv)
    ld = dt * aneg_ref[...]
    if rb < C:
        dt = jnp.where(rowmask(LANES), dt, 0.0)
        ld = jnp.where(rowmask(LANES), ld, 0.0)
    bs = _exact_left(tri, ld)
    e64 = e64_ref[...]
    dt64 = _exact_right(dt, e64)
    b64 = _exact_right(bs, e64)
    blast64 = b64[C - 1:C, :]
    xdt = xs * dt64
    bc = _exact_right(bs, ec_ref[...])
    br = jnp.sum(eye_ref[...] * bc, axis=0, keepdims=True)
    vis = caus_ref[...] > 0.0
    lf = jnp.where(vis, jnp.exp(jnp.where(vis, bc - br, 0.0)), 0.0)
    hpg = inner // SSD_HEADDIM // SSD_GROUPS
    gw = hpg * SSD_HEADDIM
    lane_head = lax.broadcasted_iota(jnp.int32, (C, gw), 1) // SSD_HEADDIM
    y_groups = []
    for g in range(SSD_GROUPS):
        cm_g = cm[:, g * SSD_STATE:(g + 1) * SSD_STATE]
        bm_g = bm[:, g * SSD_STATE:(g + 1) * SSD_STATE]
        gl = slice(g * gw, (g + 1) * gw)
        gmat = _dot_nt(cm_g, bm_g)
        a4 = jnp.concatenate([gmat] * hpg, axis=1) * lf[:, g * hpg * C:(g + 1) * hpg * C]
        xg = xdt[:, gl]
        x4 = jnp.concatenate([jnp.where(lane_head == hh, xg, 0.0) for hh in range(hpg)], axis=0)
        s_g = ss_ref[g]
        y_g = _dot(a4, x4) + _dot(cm_g, s_g) * jnp.exp(b64[:, gl])
        xw = xg * jnp.exp(blast64[:, gl] - b64[:, gl])
        ss_ref[g] = s_g * jnp.exp(blast64[:, gl]) + _dot_tn(bm_g, xw)
        y_groups.append(y_g)
    y = jnp.concatenate(y_groups, axis=1) + xs * dskip_ref[...]
    y = _rms(y * _silu(z), gssd_ref[...])

    o_full = jnp.concatenate([o_a, y], axis=1)
    o_ref[...] = o_full[:rb]

    @pl.when(c == nc - 1)
    def _fin():
        gla_ref[...] = sg_ref[...]
        ssm_ref[...] = ss_ref[...]
        ctail_ref[...] = ext_ref[pl.ds(rb, CONV_PAD), :]

    ext_ref[0:CONV_PAD, :] = ext_ref[pl.ds(rb, CONV_PAD), :]


def _mixer_call(proj, row_off, nb, t, gla0, ssm0, conv0, consts, dk, dv):
    C = REC_ROWS
    rb = min(t, C)
    assert t % rb == 0 and row_off % rb == 0
    steps = t // rb
    width = proj.shape[1]
    nh = GLA_HEADS
    vw = nh * dv
    cw = conv0.shape[-1]
    blk0 = row_off // rb

    def full(a):
        nd = a.ndim
        return pl.BlockSpec(a.shape, lambda b, c, _n=nd: (0,) * _n)

    in_specs = [pl.BlockSpec((rb, width), lambda b, c: (blk0 + b * steps + c, 0)),
                pl.BlockSpec((None, nh, dv, dk), lambda b, c: (b, 0, 0, 0)),
                pl.BlockSpec((None, SSD_GROUPS, SSD_STATE, ssm0.shape[-1]), lambda b, c: (b, 0, 0, 0)),
                pl.BlockSpec((None, CONV_PAD, cw), lambda b, c: (b, 0, 0))]
    in_specs += [full(a) for a in consts]
    operands = [proj, gla0, ssm0, conv0, *consts]
    out_specs = [pl.BlockSpec((rb, 2 * vw), lambda b, c: (b * steps + c, 0)),
                 pl.BlockSpec((None, nh, dv, dk), lambda b, c: (b, 0, 0, 0)),
                 pl.BlockSpec((None, SSD_GROUPS, SSD_STATE, ssm0.shape[-1]), lambda b, c: (b, 0, 0, 0)),
                 pl.BlockSpec((None, CONV_PAD, cw), lambda b, c: (b, 0, 0))]
    out_shape = [jax.ShapeDtypeStruct((nb * t, 2 * vw), F32),
                 jax.ShapeDtypeStruct(gla0.shape, F32),
                 jax.ShapeDtypeStruct(ssm0.shape, F32),
                 jax.ShapeDtypeStruct((nb, CONV_PAD, cw), F32)]
    return pl.pallas_call(
        functools.partial(_mixer_body, rb=rb, dk=dk, dv=dv),
        grid=(nb, steps),
        in_specs=in_specs,
        out_specs=out_specs,
        out_shape=out_shape,
        scratch_shapes=[pltpu.VMEM((nh, dv, dk), F32),
                        pltpu.VMEM((SSD_GROUPS, SSD_STATE, ssm0.shape[-1]), F32),
                        pltpu.VMEM((C + CONV_PAD, cw), F32),
                        pltpu.VMEM((C, nh * dk), F32)],
        compiler_params=_cparams(("arbitrary", "arbitrary")),
        name="gla_ssd_mixer",
    )(*operands)


def _proj_resid_body(oa_ref, ob_ref, w_ref, x_ref, g_ref, wr_ref, x1_ref, h_ref, lg_ref, *, n_first):
    o = jnp.where(pl.program_id(0) < n_first, oa_ref[...], ob_ref[...])
    x1 = x_ref[...] + _dot(o, w_ref[...])
    x1_ref[...] = x1
    h = _rms(x1, g_ref[...])
    h_ref[...] = h
    lg_ref[...] = _dot(h, wr_ref[...])


def _proj_resid(o_a, o_b, w, x, g, wr, tm):
    n, d = x.shape
    kin = o_a.shape[1]
    assert o_a.shape[0] % tm == 0 and o_b.shape[0] % tm == 0
    n_first = o_a.shape[0] // tm
    return pl.pallas_call(
        functools.partial(_proj_resid_body, n_first=n_first),
        grid=(n // tm,),
        in_specs=[pl.BlockSpec((tm, kin), lambda i: (jnp.minimum(i, n_first - 1), 0)),
                  pl.BlockSpec((tm, kin), lambda i: (jnp.maximum(i - n_first, 0), 0)),
                  pl.BlockSpec((kin, d), lambda i: (0, 0), pipeline_mode=pl.Buffered(1)),
                  pl.BlockSpec((tm, d), lambda i: (i, 0)),
                  pl.BlockSpec((1, d), lambda i: (0, 0)),
                  pl.BlockSpec((d, LANES), lambda i: (0, 0))],
        out_specs=[pl.BlockSpec((tm, d), lambda i: (i, 0)),
                   pl.BlockSpec((tm, d), lambda i: (i, 0)),
                   pl.BlockSpec((tm, LANES), lambda i: (i, 0))],
        out_shape=[jax.ShapeDtypeStruct((n, d), F32),
                   jax.ShapeDtypeStruct((n, d), F32),
                   jax.ShapeDtypeStruct((n, LANES), F32)],
        compiler_params=_cparams(("arbitrary",)),
        name="proj_resid_norm_router",
    )(o_a, o_b, w, x, g.reshape(1, d), wr)


_META_E0, _META_E1, _META_G0, _META_G1, _META_R0, _META_R1 = range(6)
_EXP_LANE0 = MOE_GROUPS


def _route_body(lg_ref, bias_ref, ltri_ref, meta_ref, cnt_ref, base_ref):
    i = pl.program_id(0)

    @pl.when(i == 0)
    def _():
        base_ref[...] = jnp.zeros_like(base_ref)

    lg = lg_ref[...] + bias_ref[...]
    tm = lg.shape[0]
    lane_i = lax.broadcasted_iota(jnp.int32, (tm, LANES), 1)
    lane = lane_i.astype(F32)
    neg = -jnp.inf
    glog = jnp.where(lane_i < MOE_GROUPS, lg, neg)
    gmax = jnp.max(glog, axis=1, keepdims=True)
    gsel = jnp.min(jnp.where(glog == gmax, lane, float(LANES)), axis=1, keepdims=True)
    pg = 1.0 / jnp.sum(jnp.exp(glog - gmax), axis=1, keepdims=True)
    lo = _EXP_LANE0 + MOE_PER_GROUP * gsel
    el = jnp.where((lane >= lo) & (lane < lo + MOE_PER_GROUP), lg, neg)
    v1 = jnp.max(el, axis=1, keepdims=True)
    i1 = jnp.min(jnp.where(el == v1, lane, float(LANES)), axis=1, keepdims=True)
    el2 = jnp.where(lane == i1, neg, el)
    v2 = jnp.max(el2, axis=1, keepdims=True)
    i2 = jnp.min(jnp.where(el2 == v2, lane, float(LANES)), axis=1, keepdims=True)
    e = jnp.exp(v2 - v1)
    g1 = pg / (1.0 + e)
    g2 = pg * e / (1.0 + e)
    hot1 = lane == i1
    hot2 = lane == i2
    onehot = jnp.where(hot1 | hot2, 1.0, 0.0)
    before = _dot(ltri_ref[...], onehot.astype(BF16)) + base_ref[0:1, :]
    r1 = jnp.sum(jnp.where(hot1, before, 0.0), axis=1, keepdims=True)
    r2 = jnp.sum(jnp.where(hot2, before, 0.0), axis=1, keepdims=True)
    base_ref[0:1, :] = base_ref[0:1, :] + jnp.sum(onehot, axis=0, keepdims=True)
    meta = jnp.zeros((tm, LANES), F32)
    for idx, val in ((_META_E0, i1 - _EXP_LANE0), (_META_E1, i2 - _EXP_LANE0),
                     (_META_G0, g1), (_META_G1, g2), (_META_R0, r1), (_META_R1, r2)):
        meta = jnp.where(lane_i == idx, val, meta)
    meta_ref[...] = meta

    @pl.when(i == pl.num_programs(0) - 1)
    def _():
        cnt_ref[...] = base_ref[...]


def _route(logits, bias, tm):
    n = logits.shape[0]
    ltri = jnp.asarray(np.tril(np.ones((tm, tm), np.float32), -1), BF16)
    return pl.pallas_call(
        _route_body,
        grid=(n // tm,),
        in_specs=[pl.BlockSpec((tm, LANES), lambda i: (i, 0)),
                  pl.BlockSpec((1, LANES), lambda i: (0, 0)),
                  pl.BlockSpec((tm, tm), lambda i: (0, 0))],
        out_specs=[pl.BlockSpec((tm, LANES), lambda i: (i, 0)),
                   pl.BlockSpec((8, LANES), lambda i: (0, 0))],
        out_shape=[jax.ShapeDtypeStruct((n, LANES), F32),
                   jax.ShapeDtypeStruct((8, LANES), F32)],
        scratch_shapes=[pltpu.VMEM((8, LANES), F32)],
        compiler_params=_cparams(("arbitrary",)),
        name="moe_route",
    )(logits, bias, ltri)


def _dispatch_body(dest_ref, pad0_ref, npad_ref, nb_ref, h_ref, xs_ref, zbuf, sem, zsem, *, tm, tm_e, nblk):
    i = pl.program_id(0)
    n_tok = pl.num_programs(0) * tm

    def pad_fill(wait):
        def go(cp):
            if wait:
                cp.wait()
            else:
                cp.start()

        def body(e, carry):
            off = pad0_ref[e]
            npad = npad_ref[e]
            head = (SUBLANES - off % SUBLANES) % SUBLANES
            for j in range(SUBLANES - 1):
                @pl.when(j < head)
                def _(j=j):
                    go(pltpu.make_async_copy(zbuf.at[pl.ds(0, 1), :], xs_ref.at[pl.ds(off + j, 1), :], zsem))
            off = pl.multiple_of(off + head, SUBLANES)
            rem = npad - head
            bit = tm_e // 2
            while bit >= SUBLANES:
                on = (rem & bit) != 0

                @pl.when(on)
                def _(off=off, bit=bit):
                    go(pltpu.make_async_copy(zbuf.at[pl.ds(0, bit), :], xs_ref.at[pl.ds(off, bit), :], zsem))

                off = pl.multiple_of(off + jnp.where(on, bit, 0), SUBLANES)
                bit //= 2
            return carry
        lax.fori_loop(0, N_EXPERTS, body, 0)

        def tail(tb, carry):
            go(pltpu.make_async_copy(zbuf, xs_ref.at[pl.ds(pl.multiple_of(tb * tm_e, tm_e), tm_e), :], zsem))
            return carry
        lax.fori_loop(nb_ref[0], nblk, tail, 0)

    @pl.when(i == 0)
    def _():
        zbuf[...] = jnp.zeros(zbuf.shape, F32)
        pad_fill(False)

    for r in range(tm):
        for kk in range(2):
            d = dest_ref[kk * n_tok + i * tm + r]
            pltpu.make_async_copy(h_ref.at[pl.ds(r, 1), :], xs_ref.at[pl.ds(d, 1), :], sem).start(
                priority=(r + kk) % 2)

    @pl.when(i == 0)
    def _():
        pad_fill(True)

    for _ in range(2 * tm):
        pltpu.make_async_copy(h_ref.at[pl.ds(0, 1), :], xs_ref.at[pl.ds(0, 1), :], sem).wait()


def _dispatch(h, dest_flat, pad0, npad, nb_used, nblk, tm, tm_e):
    n, d = h.shape
    return pl.pallas_call(
        functools.partial(_dispatch_body, tm=tm, tm_e=tm_e, nblk=nblk),
        grid_spec=pltpu.PrefetchScalarGridSpec(
            num_scalar_prefetch=4,
            grid=(n // tm,),
            in_specs=[pl.BlockSpec((tm, d), lambda i, *_: (i, 0))],
            out_specs=pl.BlockSpec(memory_space=pl.ANY),
            scratch_shapes=[pltpu.VMEM((tm_e, d), F32),
                            pltpu.SemaphoreType.DMA(()),
                            pltpu.SemaphoreType.DMA(())]),
        out_shape=jax.ShapeDtypeStruct((nblk * tm_e, d), F32),
        compiler_params=_cparams(("arbitrary",)),
        name="moe_dispatch",
    )(dest_flat, pad0, npad, nb_used, h)


def _experts_body(be_ref, nb_ref, first_ref, nxt_ref, slot_ref, x_ref, w1_hbm, w3_hbm, w2_hbm, y_ref,
                  wb1, wb3, wb2, sem, *, layer):
    b = pl.program_id(0)
    used = b < nb_ref[0]
    s = slot_ref[b]

    def weight_copies(e, slot):
        return (pltpu.make_async_copy(w1_hbm.at[layer, e], wb1.at[slot], sem.at[slot, 0]),
                pltpu.make_async_copy(w3_hbm.at[layer, e], wb3.at[slot], sem.at[slot, 1]),
                pltpu.make_async_copy(w2_hbm.at[layer, e], wb2.at[slot], sem.at[slot, 2]))

    @pl.when(b == 0)
    def _():
        for cp in weight_copies(be_ref[0], 0):
            cp.start()

    @pl.when(jnp.logical_and(used, first_ref[b] == 1))
    def _():
        for cp in weight_copies(be_ref[b], s):
            cp.wait()

        @pl.when(nxt_ref[b] >= 0)
        def _():
            for cp in weight_copies(nxt_ref[b], 1 - s):
                cp.start()

    @pl.when(used)
    def _():
        x = x_ref[...]
        a = _dot(x, wb1[s])
        g = _dot(x, wb3[s])
        y_ref[...] = _dot(_silu(a) * g, wb2[s])

    @pl.when(jnp.logical_not(used))
    def _():
        y_ref[...] = jnp.zeros(y_ref.shape, F32)


def _experts(xs, block_e, nb_used, first, nxt_e, slot, w1, w3, w2, layer, tm):
    s, d = xs.shape
    f = w1.shape[-1]
    nblk = s // tm
    return pl.pallas_call(
        functools.partial(_experts_body, layer=layer),
        grid_spec=pltpu.PrefetchScalarGridSpec(
            num_scalar_prefetch=5,
            grid=(nblk,),
            in_specs=[pl.BlockSpec((tm, d), lambda b, be, nb, *_: (jnp.minimum(b, nb[0] - 1), 0)),
                      pl.BlockSpec(memory_space=pl.ANY),
                      pl.BlockSpec(memory_space=pl.ANY),
                      pl.BlockSpec(memory_space=pl.ANY)],
            out_specs=pl.BlockSpec((tm, d), lambda b, *_: (b, 0)),
            scratch_shapes=[pltpu.VMEM((2, d, f), F32),
                            pltpu.VMEM((2, d, f), F32),
                            pltpu.VMEM((2, f, d), F32),
                            pltpu.SemaphoreType.DMA((2, 3))]),
        out_shape=jax.ShapeDtypeStruct((s, d), F32),
        compiler_params=_cparams(("arbitrary",)),
        name="moe_experts",
    )(block_e, nb_used, first, nxt_e, slot, xs, w1, w3, w2)


def _combine_body(dest_ref, x_ref, meta_ref, gfin_ref, ys_ref, *rest, tm, n_first, final):
    if final:
        o_a_ref, o_b_ref, buf, sem = rest
    else:
        o_a_ref, buf, sem = rest
        o_b_ref = None
    i = pl.program_id(0)
    nsteps = pl.num_programs(0)

    def issue(step, slot):
        for r in range(tm):
            for kk in range(2):
                d = dest_ref[kk * (nsteps * tm) + step * tm + r]
                pltpu.make_async_copy(ys_ref.at[pl.ds(d, 1), :], buf.at[slot, kk, pl.ds(r, 1), :],
                                      sem.at[slot]).start(priority=(r + kk) % 2)

    @pl.when(i == 0)
    def _():
        issue(0, 0)

    @pl.when(i + 1 < nsteps)
    def _():
        issue(i + 1, (i + 1) % 2)

    slot = i % 2
    for _ in range(2 * tm):
        pltpu.make_async_copy(ys_ref.at[pl.ds(0, 1), :], buf.at[slot, 0, pl.ds(0, 1), :], sem.at[slot]).wait()

    meta = meta_ref[...]
    g0 = meta[:, _META_G0:_META_G0 + 1]
    g1 = meta[:, _META_G1:_META_G1 + 1]
    out = x_ref[...] + (buf[slot, 0] * g0 + buf[slot, 1] * g1)
    if not final:
        o_a_ref[...] = out
    else:
        out = _rms(out, gfin_ref[...])

        @pl.when(i < n_first)
        def _():
            o_a_ref[...] = out

        @pl.when(i >= n_first)
        def _():
            o_b_ref[...] = out


def _combine(x, meta, dest_flat, ys, gfin, tm, n_first_rows, final):
    n, d = x.shape
    n_first = n_first_rows // tm
    if final:
        out_specs = [pl.BlockSpec((tm, d), lambda i, dest: (jnp.minimum(i, n_first - 1), 0)),
                     pl.BlockSpec((tm, d), lambda i, dest: (jnp.maximum(i - n_first, 0), 0))]
        out_shape = [jax.ShapeDtypeStruct((n_first_rows, d), F32),
                     jax.ShapeDtypeStruct((n - n_first_rows, d), F32)]
    else:
        out_specs = [pl.BlockSpec((tm, d), lambda i, dest: (i, 0))]
        out_shape = [jax.ShapeDtypeStruct((n, d), F32)]
    return pl.pallas_call(
        functools.partial(_combine_body, tm=tm, n_first=n_first, final=final),
        grid_spec=pltpu.PrefetchScalarGridSpec(
            num_scalar_prefetch=1,
            grid=(n // tm,),
            in_specs=[pl.BlockSpec((tm, d), lambda i, dest: (i, 0)),
                      pl.BlockSpec((tm, LANES), lambda i, dest: (i, 0)),
                      pl.BlockSpec((1, d), lambda i, dest: (0, 0)),
                      pl.BlockSpec(memory_space=pl.ANY)],
            out_specs=out_specs,
            scratch_shapes=[pltpu.VMEM((2, 2, tm, d), F32),
                            pltpu.SemaphoreType.DMA((2,))]),
        out_shape=out_shape,
        compiler_params=_cparams(("arbitrary",)),
        name="moe_combine_final" if final else "moe_combine",
    )(dest_flat, x, meta, gfin.reshape(1, d), ys)


def _moe(x1, h, logits, rbias, w1, w3, w2, layer, gfin, n_first_rows, final, tm_e=128, tm_t=128):
    n, d = h.shape
    meta, cnt = _route(logits, rbias, 256)
    meta_t = meta[:, :SUBLANES].T
    eid = meta_t[_META_E0:_META_E1 + 1].astype(jnp.int32)
    rank = meta_t[_META_R0:_META_R1 + 1].astype(jnp.int32)
    counts = cnt[0, _EXP_LANE0:_EXP_LANE0 + N_EXPERTS].astype(jnp.int32)
    padded = (counts + tm_e - 1) // tm_e * tm_e
    p_end = jnp.cumsum(padded)
    p_start = p_end - padded
    dest = (p_start[eid] + rank).reshape(-1)
    s = n * 2
    nblk = (s + N_EXPERTS * (tm_e - 1) + tm_e - 1) // tm_e
    block_e = jnp.minimum(jnp.searchsorted(p_end, jnp.arange(nblk, dtype=jnp.int32) * tm_e, side='right'),
                          N_EXPERTS - 1).astype(jnp.int32)
    nb_used = (p_end[-1:] // tm_e).astype(jnp.int32)
    bidx = jnp.arange(nblk, dtype=jnp.int32)
    prev_e = jnp.concatenate([jnp.full((1,), -1, jnp.int32), block_e[:-1]])
    first = ((bidx < nb_used[0]) & (block_e != prev_e)).astype(jnp.int32)
    slot = ((jnp.cumsum(first) - 1) % 2).astype(jnp.int32)
    eidx = jnp.arange(N_EXPERTS, dtype=jnp.int32)
    live = jnp.where(counts > 0, eidx, N_EXPERTS)
    next_live = jnp.concatenate([lax.cummin(live[::-1])[::-1][1:], jnp.full((1,), N_EXPERTS, jnp.int32)])
    nxt_e = next_live[block_e]
    nxt_e = jnp.where(nxt_e < N_EXPERTS, nxt_e, -1).astype(jnp.int32)
    xs = _dispatch(h, dest, (p_start + counts).astype(jnp.int32), (padded - counts).astype(jnp.int32),
                   nb_used, nblk, tm_t, tm_e)
    ys = _experts(xs, block_e, nb_used, first, nxt_e, slot, w1, w3, w2, layer, tm_e)
    return _combine(x1, meta, dest, ys, gfin, tm_t, n_first_rows, final)


def _lambda_value(lam_ref, lambda_init):
    lam = lam_ref[...]
    s1 = jnp.sum(lam[0:1] * lam[1:2], axis=1, keepdims=True)
    s2 = jnp.sum(lam[2:3] * lam[3:4], axis=1, keepdims=True)
    return jnp.exp(s1) - jnp.exp(s2) + lambda_init


def _lane_tile(x, width):
    if width % LANES == 0:
        return jnp.concatenate([x] * (width // LANES), axis=1)
    return x[:, :width]


def _softmax_step(j, s, v, m_ref, l_ref, acc_ref, rows=slice(None)):
    m_prev = m_ref[j, rows]
    m_new = jnp.maximum(m_prev, jnp.max(s, axis=1, keepdims=True))
    alpha = jnp.exp(m_prev - m_new)
    p = jnp.exp(s - _lane_tile(m_new, s.shape[1]))
    l_ref[j, rows] = alpha * l_ref[j, rows] + jnp.sum(p, axis=1, keepdims=True)
    acc = acc_ref[j, rows]
    acc_ref[j, rows] = acc * _lane_tile(alpha, acc.shape[1]) + _dot(p, v)
    m_ref[j, rows] = m_new


def _attn_finish(lam_ref, g_ref, l_ref, acc_ref, lambda_init):
    lam = _lambda_value(lam_ref, lambda_init)
    width = acc_ref.shape[-1]
    o = acc_ref[0] / _lane_tile(l_ref[0], width) - lam * (acc_ref[1] / _lane_tile(l_ref[1], width))
    return _rms(o, g_ref[...]) * (1.0 - lambda_init)


def _alibi_slopes():
    slopes = [2.0 ** (-8.0 * (h + 1) / DIFF_HEADS) for h in range(DIFF_HEADS)]
    assert all(math.frexp(s)[0] == 0.5 for s in slopes)
    return slopes


def _attn_prompt_body(iq_ref, ik_ref, slope_ref, q_ref, k_ref, v_ref, kt_ref, lam_ref, g_ref,
                      o_ref, qa_ref, m_ref, l_ref, acc_ref, *, tq, lambda_init):
    h = pl.program_id(1)
    t = pl.program_id(2)
    iq = iq_ref[t]
    ik = ik_ref[t]
    dh = DIFF_DH
    slope = slope_ref[h]

    @pl.when(ik == 0)
    def _():
        m_ref[...] = jnp.full(m_ref.shape, -jnp.inf, F32)
        l_ref[...] = jnp.zeros(l_ref.shape, F32)
        acc_ref[...] = jnp.zeros(acc_ref.shape, F32)
        qpos = iq * tq + lax.broadcasted_iota(jnp.int32, (tq, LANES), 0)
        lane = lax.broadcasted_iota(jnp.int32, (tq, LANES), 1)
        qa = (qpos // CHUNK).astype(F32)
        qb = (qpos % CHUNK).astype(F32)
        feat = jnp.where(lane == 0, qa * (-slope * CHUNK),
                         jnp.where(lane == 1, qb * (-slope), jnp.where(lane < 4, slope, 0.0)))
        q = q_ref[...]
        for j in range(2):
            qa_ref[j] = jnp.concatenate([q[:, j * dh:(j + 1) * dh] * (dh ** -0.5), feat], axis=1)

    k = k_ref[...]
    v = v_ref[...]
    kt = kt_ref[...]

    ts = tq // ATTN_ROW_SPLIT
    ka = [jnp.concatenate([k[:, j * dh:(j + 1) * dh], kt], axis=1) for j in range(2)]

    @pl.when(ik < iq)
    def _():
        for r in range(ATTN_ROW_SPLIT):
            rows = slice(r * ts, (r + 1) * ts)
            for j in range(2):
                _softmax_step(j, _dot_nt(qa_ref[j, rows], ka[j]), v, m_ref, l_ref, acc_ref, rows)

    @pl.when(ik == iq)
    def _():
        for r in range(ATTN_ROW_SPLIT):
            rows = slice(r * ts, (r + 1) * ts)
            row = r * ts + lax.broadcasted_iota(jnp.int32, (ts, tq), 0)
            col = lax.broadcasted_iota(jnp.int32, (ts, tq), 1)
            fix = jnp.maximum(col - row, 0).astype(F32) * (-2.0 * slope)
            vis = (col // CHUNK) <= (row // CHUNK)
            for j in range(2):
                s = _dot_nt(qa_ref[j, rows], ka[j])
                _softmax_step(j, jnp.where(vis, s + fix, -jnp.inf), v, m_ref, l_ref, acc_ref, rows)
        o_ref[...] = _attn_finish(lam_ref, g_ref, l_ref, acc_ref, lambda_init)


def _attn_prompt(q, k, v, nb, t, lam, g_subln, lambda_init, tq):
    hw = 2 * DIFF_DH
    assert t % tq == 0 and tq % CHUNK == 0 and t // CHUNK <= 256
    nq = t // tq
    pairs = [(iq, ik) for iq in range(nq) for ik in range(iq + 1)]
    iq_tab = jnp.asarray(np.array([p[0] for p in pairs], np.int32))
    ik_tab = jnp.asarray(np.array([p[1] for p in pairs], np.int32))
    slopes = jnp.asarray(np.array(_alibi_slopes(), np.float32))
    pos = np.arange(t)
    ktab = np.zeros((t, LANES), np.float32)
    ktab[:, 0] = 1.0
    ktab[:, 1] = 1.0
    ktab[:, 2] = (pos // CHUNK) * CHUNK
    ktab[:, 3] = pos % CHUNK
    g3 = g_subln.reshape(DIFF_HEADS, 1, hw)
    return pl.pallas_call(
        functools.partial(_attn_prompt_body, tq=tq, lambda_init=lambda_init),
        grid_spec=pltpu.PrefetchScalarGridSpec(
            num_scalar_prefetch=3,
            grid=(nb, DIFF_HEADS, len(pairs)),
            in_specs=[pl.BlockSpec((tq, hw), lambda b, h, p, iqt, ikt, sl: (b * nq + iqt[p], h)),
                      pl.BlockSpec((tq, hw), lambda b, h, p, iqt, ikt, sl: (b * nq + ikt[p], h)),
                      pl.BlockSpec((tq, hw), lambda b, h, p, iqt, ikt, sl: (b * nq + ikt[p], h)),
                      pl.BlockSpec((tq, LANES), lambda b, h, p, iqt, ikt, sl: (ikt[p], 0)),
                      pl.BlockSpec((4, DIFF_DH), lambda b, h, p, iqt, ikt, sl: (0, 0)),
                      pl.BlockSpec((None, 1, hw), lambda b, h, p, iqt, ikt, sl: (h, 0, 0))],
            out_specs=pl.BlockSpec((tq, hw), lambda b, h, p, iqt, ikt, sl: (b * nq + iqt[p], h)),
            scratch_shapes=[pltpu.VMEM((2, tq, hw), F32),
                            pltpu.VMEM((2, tq, LANES), F32),
                            pltpu.VMEM((2, tq, LANES), F32),
                            pltpu.VMEM((2, tq, hw), F32)]),
        out_shape=jax.ShapeDtypeStruct((nb * t, DIFF_HEADS * hw), F32),
        compiler_params=_cparams(("arbitrary", "arbitrary", "arbitrary")),
        name="diff_attn_prompt",
    )(iq_tab, ik_tab, slopes, q, k, v, jnp.asarray(ktab), lam, g3)


def _attn_sample_body(q_ref, kc_ref, vc_ref, kn_ref, vn_ref, lam_ref, g_ref,
                      o_ref, m_ref, l_ref, acc_ref, *, tk, past, lambda_init):
    ik = pl.program_id(1)
    nkb = pl.num_programs(1) - 1
    dh = DIFF_DH
    hw = 2 * dh
    tq = q_ref.shape[0]
    scale = dh ** -0.5

    @pl.when(ik == 0)
    def _():
        m_ref[...] = jnp.full(m_ref.shape, -jnp.inf, F32)
        l_ref[...] = jnp.zeros(l_ref.shape, F32)
        acc_ref[...] = jnp.zeros(acc_ref.shape, F32)

    def attend(h, k, v, k0, width):
        slope = _alibi_slopes()[h]
        q = q_ref[:, h * hw:(h + 1) * hw]
        qpos = past + lax.broadcasted_iota(jnp.int32, (tq, width), 0)
        kpos = k0 + lax.broadcasted_iota(jnp.int32, (tq, width), 1)
        bias = jnp.abs(qpos - kpos).astype(F32) * (-slope)
        vis = (kpos // CHUNK) <= (qpos // CHUNK)
        for j in range(2):
            s = _dot_nt(q[:, j * dh:(j + 1) * dh], k[:, j * dh:(j + 1) * dh]) * scale + bias
            s = jnp.where(vis, s, -jnp.inf)
            _softmax_step(j, s, v, m_ref.at[h], l_ref.at[h], acc_ref.at[h])

    @pl.when(ik < nkb)
    def _():
        for h in range(DIFF_HEADS):
            attend(h, kc_ref[:, h, :], vc_ref[:, h, :], ik * tk, tk)

    @pl.when(ik == nkb)
    def _():
        for h in range(DIFF_HEADS):
            hs = slice(h * hw, (h + 1) * hw)
            attend(h, kn_ref[:, hs], vn_ref[:, hs], past, tq)
            o_ref[:, hs] = _attn_finish(lam_ref, g_ref.at[h], l_ref.at[h], acc_ref.at[h], lambda_init)


def _attn_sample(q, k, v, cache_k, cache_v, li, lam, g_subln, lambda_init, tk):
    n_rows = q.shape[0]
    _, nb, past, nh, hw = cache_k.shape
    aw = nh * hw
    tq = n_rows // nb
    tk = min(tk, past)
    assert past % tk == 0
    nkb = past // tk
    g3 = g_subln.reshape(nh, 1, hw)
    cache_spec = pl.BlockSpec((None, None, tk, nh, hw),
                              lambda b, ik: (li, b, jnp.minimum(ik, nkb - 1), 0, 0))
    row_spec = pl.BlockSpec((tq, aw), lambda b, ik: (b, 0))
    return pl.pallas_call(
        functools.partial(_attn_sample_body, tk=tk, past=past, lambda_init=lambda_init),
        grid=(nb, nkb + 1),
        in_specs=[row_spec,
                  cache_spec, cache_spec,
                  row_spec,
                  row_spec,
                  pl.BlockSpec((4, DIFF_DH), lambda b, ik: (0, 0)),
                  pl.BlockSpec((nh, 1, hw), lambda b, ik: (0, 0, 0))],
        out_specs=pl.BlockSpec((tq, aw), lambda b, ik: (b, 0)),
        scratch_shapes=[pltpu.VMEM((nh, 2, tq, LANES), F32),
                        pltpu.VMEM((nh, 2, tq, LANES), F32),
                        pltpu.VMEM((nh, 2, tq, hw), F32)],
        out_shape=jax.ShapeDtypeStruct((n_rows, aw), F32),
        compiler_params=_cparams(("arbitrary", "arbitrary")),
        name="diff_attn_sample",
    )(q, cache_k, cache_v, k, v, lam, g3)


def _pad_cols(a, width):
    return jnp.pad(a, ((0, 0), (0, width - a.shape[1])))


def _mixer_constants(dk, dv, w_gate_up, b_gate, g_gla, conv_w, conv_b, dt_bias, a_log, d_skip, g_ssd):
    C = REC_ROWS
    nh = GLA_HEADS
    inner = g_ssd.shape[0]
    n_ssd = inner // SSD_HEADDIM
    wup = jnp.pad(w_gate_up, ((0, LANES - w_gate_up.shape[0]), (0, 0)))
    tri = jnp.asarray(np.tril(np.ones((C, C), np.float32)), BF16)
    e64 = np.zeros((LANES, inner), np.float32)
    ec = np.zeros((LANES, n_ssd * C), np.float32)
    for hh in range(n_ssd):
        e64[hh, hh * SSD_HEADDIM:(hh + 1) * SSD_HEADDIM] = 1.0
        ec[hh, hh * C:(hh + 1) * C] = 1.0
    eye = np.tile(np.eye(C, dtype=np.float32), (1, n_ssd))
    caus = np.tile(np.tril(np.ones((C, C), np.float32)), (1, n_ssd))
    return [wup, b_gate.reshape(1, -1), g_gla.reshape(nh, dv), conv_w, conv_b.reshape(1, -1),
            _pad_cols(dt_bias.reshape(1, -1), LANES),
            _pad_cols(-jnp.exp(a_log.astype(F32)).reshape(1, -1), LANES),
            jnp.repeat(d_skip, SSD_HEADDIM).reshape(1, -1), g_ssd.reshape(1, -1),
            tri, jnp.asarray(e64, BF16), jnp.asarray(ec, BF16), jnp.asarray(eye), jnp.asarray(caus)]


def kernel(x_prompt, x_sample, state_gla, state_ssm, state_conv, cache_k, cache_v, norm_mix, norm_ffn, norm_final, w_in, w_gate_up, b_gate, g_gla, conv_w, conv_b, dt_bias, a_log, d_skip, g_ssd, w_out_mix, w_qkv, lam_q1, lam_k1, lam_q2, lam_k2, g_subln, w_o, router_group_w, router_group_b, router_expert_w, router_expert_b, w1, w3, w2):
    bp, tp, d = x_prompt.shape
    bs, ts, _ = x_sample.shape
    n_p, n_s = bp * tp, bs * ts
    n = n_p + n_s
    depth = norm_mix.shape[0]
    nh = GLA_HEADS
    dk, dv = state_gla.shape[-2], state_gla.shape[-1]
    qk_w, vw = nh * dk, nh * dv
    n_ssd = state_ssm.shape[2]
    inner = n_ssd * SSD_HEADDIM
    hpg = n_ssd // SSD_GROUPS
    gs = SSD_GROUPS * SSD_STATE
    cw = inner + 2 * gs
    rank = w_gate_up.shape[1]
    assert w_gate_up.shape[2] == qk_w and rank <= LANES and n_ssd <= LANES

    x = jnp.concatenate([x_prompt.reshape(n_p, d), x_sample.reshape(n_s, d)], axis=0)
    tm_row = next(tm for tm in (1056, 768, 256) if n % tm == 0)

    i_even = i_odd = 0
    gla_p, ssm_p, conv_p, gla_s, ssm_s, conv_s = [], [], [], [], [], []
    k_p, v_p, k_s, v_s = [], [], [], []
    y_p = y_s = None
    for layer in range(depth):
        final = layer == depth - 1
        if layer % 2 == 0:
            i = i_even
            i_even += 1
            offs = np.cumsum([0, qk_w, qk_w, vw, rank, vw, inner, cw, n_ssd])
            w_t = w_in[i].T
            seg = lambda j: w_t[offs[j]:offs[j + 1]]
            pad_rows = lambda a: jnp.pad(a, ((0, LANES - a.shape[0]), (0, 0)))
            w_cat = jnp.concatenate([seg(0), seg(1), seg(2), seg(4), seg(5), seg(6),
                                     pad_rows(seg(3)), pad_rows(seg(7))], axis=0)
            proj = _norm_matmul(x, norm_mix[layer], w_cat, tm_row, 1280, w_is_transposed=True)
            consts = _mixer_constants(dk, dv, w_gate_up[i], b_gate[i], g_gla[i], conv_w[i], conv_b[i],
                                      dt_bias[i], a_log[i], d_skip[i], g_ssd[i])

            def to_group_state(sm):
                b_ = sm.shape[0]
                return sm.reshape(b_, SSD_GROUPS, hpg, SSD_STATE, SSD_HEADDIM).transpose(0, 1, 3, 2, 4) \
                         .reshape(b_, SSD_GROUPS, SSD_STATE, hpg * SSD_HEADDIM)

            def from_group_state(sg):
                b_ = sg.shape[0]
                return sg.reshape(b_, SSD_GROUPS, SSD_STATE, hpg, SSD_HEADDIM).transpose(0, 1, 3, 2, 4) \
                         .reshape(b_, n_ssd, SSD_STATE, SSD_HEADDIM)

            def pad_conv(cv):
                return jnp.pad(cv, ((0, 0), (CONV_PAD - cv.shape[1], 0), (0, 0)))

            mix_a, g_fin, s_fin, c_fin = _mixer_call(
                proj, 0, bp, tp, jnp.zeros((bp, nh, dv, dk), F32),
                jnp.zeros((bp, SSD_GROUPS, SSD_STATE, hpg * SSD_HEADDIM), F32),
                jnp.zeros((bp, CONV_PAD, cw), F32), consts, dk, dv)
            gla_p.append(g_fin.transpose(0, 1, 3, 2))
            ssm_p.append(from_group_state(s_fin))
            conv_p.append(c_fin[:, CONV_PAD - (SSD_CONV - 1):])
            mix_b, g_fin, s_fin, c_fin = _mixer_call(
                proj, n_p, bs, ts, state_gla[i].transpose(0, 1, 3, 2),
                to_group_state(state_ssm[i]), pad_conv(state_conv[i]), consts, dk, dv)
            gla_s.append(g_fin.transpose(0, 1, 3, 2))
            ssm_s.append(from_group_state(s_fin))
            conv_s.append(c_fin[:, CONV_PAD - (SSD_CONV - 1):])
            w_proj = w_out_mix[i]
        else:
            i = i_odd
            i_odd += 1
            lambda_init = 0.8 - 0.6 * math.exp(-0.3 * layer)
            tm_p = next(tm for tm in (1024, 512, 256) if n_p % tm == 0)
            q_a, k_a, v_a = _norm_matmul(x, norm_mix[layer], w_qkv[i], tm_p, 512, nrows=n_p, parts=3)
            q_b, k_b, v_b = _norm_matmul(x, norm_mix[layer], w_qkv[i], n_s, 512, row0=n_p, parts=3)
            lam = jnp.stack([lam_q1[i], lam_k1[i], lam_q2[i], lam_k2[i]])
            mix_a = _attn_prompt(q_a, k_a, v_a, bp, tp, lam, g_subln[i], lambda_init, min(512, tp))
            mix_b = _attn_sample(q_b, k_b, v_b, cache_k, cache_v, i, lam, g_subln[i], lambda_init, 1024)
            k_p.append(k_a.reshape(bp, tp, DIFF_HEADS, 2 * DIFF_DH))
            v_p.append(v_a.reshape(bp, tp, DIFF_HEADS, 2 * DIFF_DH))
            k_s.append(k_b.reshape(bs, ts, DIFF_HEADS, 2 * DIFF_DH))
            v_s.append(v_b.reshape(bs, ts, DIFF_HEADS, 2 * DIFF_DH))
            w_proj = w_o[i]

        wr = _pad_cols(jnp.concatenate([router_group_w[layer], router_expert_w[layer]], axis=1), LANES)
        rbias = _pad_cols(jnp.concatenate([router_group_b[layer], router_expert_b[layer]]).reshape(1, -1), LANES)
        x1, h2, logits = _proj_resid(mix_a, mix_b, w_proj, x, norm_ffn[layer], wr, 256)
        res = _moe(x1, h2, logits, rbias, w1, w3, w2, layer, norm_final, n_p, final)
        if final:
            y_p, y_s = res
        else:
            x = res[0]

    return (y_p.reshape(bp, tp, d), y_s.reshape(bs, ts, d),
            jnp.stack(gla_p), jnp.stack(ssm_p), jnp.stack(conv_p), jnp.stack(k_p), jnp.stack(v_p),
            jnp.stack(gla_s), jnp.stack(ssm_s), jnp.stack(conv_s), jnp.stack(k_s), jnp.stack(v_s))
```

```python
import functools
import math

import numpy as np
import jax
import jax.numpy as jnp
from jax import lax
from jax.experimental import pallas as pl
from jax.experimental.pallas import tpu as pltpu

F32 = jnp.float32
BF16 = jnp.bfloat16

EPS = 1e-6
CHUNK = 64
GLA_HEADS = 4
GLA_TAU = 16.0
SSD_HEADDIM = 64
SSD_STATE = 128
SSD_GROUPS = 4
SSD_CONV = 4
DIFF_HEADS = 8
DIFF_DH = 128
MOE_GROUPS = 4
MOE_PER_GROUP = 8
N_EXPERTS = MOE_GROUPS * MOE_PER_GROUP

LANES = 128
SUBLANES = 8
REC_ROWS = 128
STRIP = 16
CONV_PAD = 8
ATTN_ROW_SPLIT = 2
VMEM_LIMIT = 56 * 1024 * 1024


def _cparams(sem, vmem=VMEM_LIMIT):
    return pltpu.CompilerParams(dimension_semantics=sem, vmem_limit_bytes=vmem)


def _dot(a, b):
    return jnp.dot(a, b, preferred_element_type=F32)


def _dot_nt(a, b):
    return lax.dot_general(a, b, (((1,), (1,)), ((), ())), preferred_element_type=F32)


def _dot_tn(a, b):
    return lax.dot_general(a, b, (((0,), (0,)), ((), ())), preferred_element_type=F32)


def _split_hi_lo(a):
    hi = a.astype(BF16)
    lo = (a - hi.astype(F32)).astype(BF16)
    return hi, lo


def _exact_left(m, a):
    hi, lo = _split_hi_lo(a)
    return _dot(m, hi) + _dot(m, lo)


def _exact_right(a, m):
    hi, lo = _split_hi_lo(a)
    return _dot(hi, m) + _dot(lo, m)


def _silu(x):
    return x / (1.0 + jnp.exp(-x))


def _log1p_exp_neg_abs(x):
    e = jnp.exp(-jnp.abs(x))
    u = 1.0 + e
    return jnp.where(u == 1.0, e, jnp.log(u) * (e / (u - 1.0)))


def _rms(x, g):
    return x * lax.rsqrt(jnp.mean(x * x, axis=-1, keepdims=True) + EPS) * g


def _norm_matmul_body(x_ref, g_ref, w_ref, *refs, w_is_transposed, parts):
    o_refs, h_ref = refs[:parts], refs[parts]
    j = pl.program_id(1)

    @pl.when(j == 0)
    def _():
        h_ref[...] = _rms(x_ref[...], g_ref[...]).astype(h_ref.dtype)

    res = (_dot_nt if w_is_transposed else _dot)(h_ref[...], w_ref[...])
    if parts == 1:
        o_refs[0][...] = res
    else:
        per = pl.num_programs(1) // parts
        for p in range(parts):
            @pl.when(j // per == p)
            def _(p=p):
                o_refs[p][...] = res


def _norm_matmul(x, g, w, tm, tn, *, row0=0, nrows=None, parts=1, w_is_transposed=False):
    n, d = x.shape
    nrows = n - row0 if nrows is None else nrows
    nout = w.shape[0] if w_is_transposed else w.shape[1]
    assert nrows % tm == 0 and row0 % tm == 0 and nout % (tn * parts) == 0
    blk0 = row0 // tm
    per = nout // tn // parts
    w_spec = (pl.BlockSpec((tn, d), lambda i, j: (j, 0)) if w_is_transposed
              else pl.BlockSpec((d, tn), lambda i, j: (0, j)))
    out_specs = [pl.BlockSpec((tm, tn), lambda i, j, p=p: (i, jnp.clip(j - p * per, 0, per - 1)))
                 for p in range(parts)]
    outs = pl.pallas_call(
        functools.partial(_norm_matmul_body, w_is_transposed=w_is_transposed, parts=parts),
        grid=(nrows // tm, nout // tn),
        in_specs=[pl.BlockSpec((tm, d), lambda i, j: (blk0 + i, 0), pipeline_mode=pl.Buffered(1)),
                  pl.BlockSpec((1, d), lambda i, j: (0, 0)),
                  w_spec],
        out_specs=out_specs,
        out_shape=[jax.ShapeDtypeStruct((nrows, nout // parts), F32)] * parts,
        scratch_shapes=[pltpu.VMEM((tm, d), w.dtype)],
        compiler_params=_cparams(("arbitrary", "arbitrary")),
        name="norm_matmul",
    )(x, g.reshape(1, d), w)
    return outs[0] if parts == 1 else outs


def _mixer_body(p_ref, gla0_ref, ssm0_ref, conv0_ref, wup_ref, bgate_ref, ggla_ref, convw_ref,
                convb_ref, dtb_ref, aneg_ref, dskip_ref, gssd_ref, tri_ref, e64_ref, ec_ref,
                eye_ref, caus_ref,
                o_ref, gla_ref, ssm_ref, ctail_ref,
                sg_ref, ss_ref, ext_ref, b_ref, *, rb, dk, dv):
    C = REC_ROWS
    c = pl.program_id(1)
    nc = pl.num_programs(1)
    nh = GLA_HEADS
    qk_w = nh * dk
    vw = nh * dv
    inner = vw
    gs = SSD_GROUPS * SSD_STATE
    o_q, o_k, o_v = 0, qk_w, 2 * qk_w
    o_r = o_v + vw
    o_z = o_r + vw
    o_x = o_z + inner
    o_g = o_x + inner + 2 * gs
    o_dt = o_g + LANES

    @pl.when(c == 0)
    def _init():
        sg_ref[...] = gla0_ref[...]
        ss_ref[...] = ssm0_ref[...]
        ext_ref[0:CONV_PAD, :] = conv0_ref[...]

    p = p_ref[...]
    if rb < C:
        p = jnp.concatenate([p, jnp.zeros((C - rb, p.shape[1]), F32)], axis=0)

    def rowmask(width):
        return lax.broadcasted_iota(jnp.int32, (C, width), 0) < rb

    q = p[:, o_q:o_q + qk_w] * (dk ** -0.5)
    k = p[:, o_k:o_k + qk_w]
    v = p[:, o_v:o_v + vw]
    r = p[:, o_r:o_r + vw]
    z = p[:, o_z:o_z + inner]
    xbc = p[:, o_x:o_x + inner + 2 * gs]
    glr = p[:, o_g:o_g + LANES]
    dtp = p[:, o_dt:o_dt + LANES]

    tri = tri_ref[...]

    zg = _dot(glr, wup_ref[...]) + bgate_ref[...]
    la = (jnp.minimum(zg, 0.0) - _log1p_exp_neg_abs(zg)) * (1.0 / GLA_TAU)
    if rb < C:
        la = jnp.where(rowmask(qk_w), la, 0.0)
    bcum = _exact_left(tri, la)
    b_ref[...] = bcum

    row_i = lax.broadcasted_iota(jnp.int32, (C, dk), 0)
    prow = lax.broadcasted_iota(jnp.int32, (STRIP, C), 0)
    pcol = lax.broadcasted_iota(jnp.int32, (STRIP, C), 1)
    o_heads = []
    for h in range(nh):
        hs = slice(h * dk, (h + 1) * dk)
        vs = slice(h * dv, (h + 1) * dv)
        bh = bcum[:, hs]
        qh = q[:, hs]
        kh = k[:, hs]
        vh = v[:, vs]
        strips = []
        for i in range(C // STRIP):
            r0 = i * STRIP
            if i == 0:
                ref_row = jnp.zeros((1, dk), F32)
            else:
                ref_row = b_ref[pl.ds(r0 - 1, 1), hs]
            q_i = qh[r0:r0 + STRIP] * jnp.exp(bh[r0:r0 + STRIP] - ref_row)
            e = jnp.where(row_i < r0 + STRIP, ref_row - bh, 0.0)
            k_i = kh * jnp.exp(e)
            s_i = _dot_nt(q_i, k_i)
            strips.append(jnp.where(pcol <= prow + r0, s_i, 0.0))
        pmat = jnp.concatenate(strips, axis=0)
        b_last = b_ref[pl.ds(C - 1, 1), hs]
        st = sg_ref[h]
        o_h = _dot(pmat, vh) + _dot_nt(qh * jnp.exp(bh), st)
        k_st = kh * jnp.exp(b_last - bh)
        sg_ref[h] = st * jnp.exp(b_last) + _dot_tn(vh, k_st)
        o_h = _rms(o_h, ggla_ref[pl.ds(h, 1), :]) * _silu(r[:, vs])
        o_heads.append(o_h)
    o_a = jnp.concatenate(o_heads, axis=1)

    cw = inner + 2 * gs
    ext_ref[CONV_PAD:CONV_PAD + C, :] = xbc
    conv = convb_ref[...]
    for j in range(SSD_CONV):
        conv = conv + convw_ref[pl.ds(j, 1), :] * ext_ref[pl.ds(CONV_PAD - (SSD_CONV - 1) + j, C), :]
    xc = _silu(conv)
    xs = xc[:, :inner]
    bm = xc[:, inner:inner + gs]
    cm = xc[:, inner + gs:cw]
    dtv = dtp + dtb_ref[...]
    dt = jnp.maximum(dtv, 0.0) + _log1p_exp_neg_abs(dtv)
    ld = dt * aneg_ref[...]
    if rb < C:
        dt = jnp.where(rowmask(LANES), dt, 0.0)
        ld = jnp.where(rowmask(LANES), ld, 0.0)
    bs = _exact_left(tri, ld)
    e64 = e64_ref[...]
    dt64 = _exact_right(dt, e64)
    b64 = _exact_right(bs, e64)
    blast64 = b64[C - 1:C, :]
    xdt = xs * dt64
    bc = _exact_right(bs, ec_ref[...])
    br = jnp.sum(eye_ref[...] * bc, axis=0, keepdims=True)
    vis = caus_ref[...] > 0.0
    lf = jnp.where(vis, jnp.exp(jnp.where(vis, bc - br, 0.0)), 0.0)
    hpg = inner // SSD_HEADDIM // SSD_GROUPS
    gw = hpg * SSD_HEADDIM
    lane_head = lax.broadcasted_iota(jnp.int32, (C, gw), 1) // SSD_HEADDIM
    y_groups = []
    for g in range(SSD_GROUPS):
        cm_g = cm[:, g * SSD_STATE:(g + 1) * SSD_STATE]
        bm_g = bm[:, g * SSD_STATE:(g + 1) * SSD_STATE]
        gl = slice(g * gw, (g + 1) * gw)
        gmat = _dot_nt(cm_g, bm_g)
        a4 = jnp.concatenate([gmat] * hpg, axis=1) * lf[:, g * hpg * C:(g + 1) * hpg * C]
        xg = xdt[:, gl]
        x4 = jnp.concatenate([jnp.where(lane_head == hh, xg, 0.0) for hh in range(hpg)], axis=0)
        s_g = ss_ref[g]
        y_g = _dot(a4, x4) + _dot(cm_g, s_g) * jnp.exp(b64[:, gl])
        xw = xg * jnp.exp(blast64[:, gl] - b64[:, gl])
        ss_ref[g] = s_g * jnp.exp(blast64[:, gl]) + _dot_tn(bm_g, xw)
        y_groups.append(y_g)
    y = jnp.concatenate(y_groups, axis=1) + xs * dskip_ref[...]
    y = _rms(y * _silu(z), gssd_ref[...])

    o_full = jnp.concatenate([o_a, y], axis=1)
    o_ref[...] = o_full[:rb]

    @pl.when(c == nc - 1)
    def _fin():
        gla_ref[...] = sg_ref[...]
        ssm_ref[...] = ss_ref[...]
        ctail_ref[...] = ext_ref[pl.ds(rb, CONV_PAD), :]

    ext_ref[0:CONV_PAD, :] = ext_ref[pl.ds(rb, CONV_PAD), :]


def _mixer_call(proj, row_off, nb, t, gla0, ssm0, conv0, consts, dk, dv):
    C = REC_ROWS
    rb = min(t, C)
    assert t % rb == 0 and row_off % rb == 0
    steps = t // rb
    width = proj.shape[1]
    nh = GLA_HEADS
    vw = nh * dv
    cw = conv0.shape[-1]
    blk0 = row_off // rb

    def full(a):
        nd = a.ndim
        return pl.BlockSpec(a.shape, lambda b, c, _n=nd: (0,) * _n)

    in_specs = [pl.BlockSpec((rb, width), lambda b, c: (blk0 + b * steps + c, 0)),
                pl.BlockSpec((None, nh, dv, dk), lambda b, c: (b, 0, 0, 0)),
                pl.BlockSpec((None, SSD_GROUPS, SSD_STATE, ssm0.shape[-1]), lambda b, c: (b, 0, 0, 0)),
                pl.BlockSpec((None, CONV_PAD, cw), lambda b, c: (b, 0, 0))]
    in_specs += [full(a) for a in consts]
    operands = [proj, gla0, ssm0, conv0, *consts]
    out_specs = [pl.BlockSpec((rb, 2 * vw), lambda b, c: (b * steps + c, 0)),
                 pl.BlockSpec((None, nh, dv, dk), lambda b, c: (b, 0, 0, 0)),
                 pl.BlockSpec((None, SSD_GROUPS, SSD_STATE, ssm0.shape[-1]), lambda b, c: (b, 0, 0, 0)),
                 pl.BlockSpec((None, CONV_PAD, cw), lambda b, c: (b, 0, 0))]
    out_shape = [jax.ShapeDtypeStruct((nb * t, 2 * vw), F32),
                 jax.ShapeDtypeStruct(gla0.shape, F32),
                 jax.ShapeDtypeStruct(ssm0.shape, F32),
                 jax.ShapeDtypeStruct((nb, CONV_PAD, cw), F32)]
    return pl.pallas_call(
        functools.partial(_mixer_body, rb=rb, dk=dk, dv=dv),
        grid=(nb, steps),
        in_specs=in_specs,
        out_specs=out_specs,
        out_shape=out_shape,
        scratch_shapes=[pltpu.VMEM((nh, dv, dk), F32),
                        pltpu.VMEM((SSD_GROUPS, SSD_STATE, ssm0.shape[-1]), F32),
                        pltpu.VMEM((C + CONV_PAD, cw), F32),
                        pltpu.VMEM((C, nh * dk), F32)],
        compiler_params=_cparams(("arbitrary", "arbitrary")),
        name="gla_ssd_mixer",
    )(*operands)


def _proj_resid_body(oa_ref, ob_ref, w_ref, x_ref, g_ref, wr_ref, x1_ref, h_ref, lg_ref, *, n_first):
    o = jnp.where(pl.program_id(0) < n_first, oa_ref[...], ob_ref[...])
    x1 = x_ref[...] + _dot(o, w_ref[...])
    x1_ref[...] = x1
    h = _rms(x1, g_ref[...])
    h_ref[...] = h
    lg_ref[...] = _dot(h, wr_ref[...])


def _proj_resid(o_a, o_b, w, x, g, wr, tm):
    n, d = x.shape
    kin = o_a.shape[1]
    assert o_a.shape[0] % tm == 0 and o_b.shape[0] % tm == 0
    n_first = o_a.shape[0] // tm
    return pl.pallas_call(
        functools.partial(_proj_resid_body, n_first=n_first),
        grid=(n // tm,),
        in_specs=[pl.BlockSpec((tm, kin), lambda i: (jnp.minimum(i, n_first - 1), 0)),
                  pl.BlockSpec((tm, kin), lambda i: (jnp.maximum(i - n_first, 0), 0)),
                  pl.BlockSpec((kin, d), lambda i: (0, 0), pipeline_mode=pl.Buffered(1)),
                  pl.BlockSpec((tm, d), lambda i: (i, 0)),
                  pl.BlockSpec((1, d), lambda i: (0, 0)),
                  pl.BlockSpec((d, LANES), lambda i: (0, 0))],
        out_specs=[pl.BlockSpec((tm, d), lambda i: (i, 0)),
                   pl.BlockSpec((tm, d), lambda i: (i, 0)),
                   pl.BlockSpec((tm, LANES), lambda i: (i, 0))],
        out_shape=[jax.ShapeDtypeStruct((n, d), F32),
                   jax.ShapeDtypeStruct((n, d), F32),
                   jax.ShapeDtypeStruct((n, LANES), F32)],
        compiler_params=_cparams(("arbitrary",)),
        name="proj_resid_norm_router",
    )(o_a, o_b, w, x, g.reshape(1, d), wr)


_META_E0, _META_E1, _META_G0, _META_G1, _META_R0, _META_R1 = range(6)
_EXP_LANE0 = MOE_GROUPS


def _route_body(lg_ref, bias_ref, ltri_ref, eye_ref, meta_ref, idx_ref, cnt_ref, base_ref):
    i = pl.program_id(0)

    @pl.when(i == 0)
    def _():
        base_ref[...] = jnp.zeros_like(base_ref)

    lg = lg_ref[...] + bias_ref[...]
    tm = lg.shape[0]
    lane_i = lax.broadcasted_iota(jnp.int32, (tm, LANES), 1)
    lane = lane_i.astype(F32)
    neg = -jnp.inf
    glog = jnp.where(lane_i < MOE_GROUPS, lg, neg)
    gmax = jnp.max(glog, axis=1, keepdims=True)
    gsel = jnp.min(jnp.where(glog == gmax, lane, float(LANES)), axis=1, keepdims=True)
    pg = 1.0 / jnp.sum(jnp.exp(glog - gmax), axis=1, keepdims=True)
    lo = _EXP_LANE0 + MOE_PER_GROUP * gsel
    el = jnp.where((lane >= lo) & (lane < lo + MOE_PER_GROUP), lg, neg)
    v1 = jnp.max(el, axis=1, keepdims=True)
    i1 = jnp.min(jnp.where(el == v1, lane, float(LANES)), axis=1, keepdims=True)
    el2 = jnp.where(lane == i1, neg, el)
    v2 = jnp.max(el2, axis=1, keepdims=True)
    i2 = jnp.min(jnp.where(el2 == v2, lane, float(LANES)), axis=1, keepdims=True)
    e = jnp.exp(v2 - v1)
    g1 = pg / (1.0 + e)
    g2 = pg * e / (1.0 + e)
    hot1 = lane == i1
    hot2 = lane == i2
    onehot = jnp.where(hot1 | hot2, 1.0, 0.0)
    before = _dot(ltri_ref[...], onehot.astype(BF16)) + base_ref[0:1, :]
    r1 = jnp.sum(jnp.where(hot1, before, 0.0), axis=1, keepdims=True)
    r2 = jnp.sum(jnp.where(hot2, before, 0.0), axis=1, keepdims=True)
    base_ref[0:1, :] = base_ref[0:1, :] + jnp.sum(onehot, axis=0, keepdims=True)
    meta = jnp.zeros((tm, LANES), F32)
    for idx, val in ((_META_E0, i1 - _EXP_LANE0), (_META_E1, i2 - _EXP_LANE0),
                     (_META_G0, g1), (_META_G1, g2), (_META_R0, r1), (_META_R1, r2)):
        meta = jnp.where(lane_i == idx, val, meta)
    meta_ref[...] = meta

    eye = eye_ref[...]
    rows = [jnp.sum(eye * col, axis=0, keepdims=True)
            for col in (i1 - _EXP_LANE0, i2 - _EXP_LANE0, r1, r2)]
    idx_ref[...] = jnp.concatenate(rows + [jnp.zeros((SUBLANES - len(rows), tm), F32)], axis=0)

    @pl.when(i == pl.num_programs(0) - 1)
    def _():
        cnt_ref[...] = base_ref[...]


def _route(logits, bias, tm):
    n = logits.shape[0]
    ltri = jnp.asarray(np.tril(np.ones((tm, tm), np.float32), -1), BF16)
    eye = jnp.asarray(np.eye(tm, dtype=np.float32))
    return pl.pallas_call(
        _route_body,
        grid=(n // tm,),
        in_specs=[pl.BlockSpec((tm, LANES), lambda i: (i, 0)),
                  pl.BlockSpec((1, LANES), lambda i: (0, 0)),
                  pl.BlockSpec((tm, tm), lambda i: (0, 0)),
                  pl.BlockSpec((tm, tm), lambda i: (0, 0))],
        out_specs=[pl.BlockSpec((tm, LANES), lambda i: (i, 0)),
                   pl.BlockSpec((None, SUBLANES, tm), lambda i: (i, 0, 0)),
                   pl.BlockSpec((8, LANES), lambda i: (0, 0))],
        out_shape=[jax.ShapeDtypeStruct((n, LANES), F32),
                   jax.ShapeDtypeStruct((n // tm, SUBLANES, tm), F32),
                   jax.ShapeDtypeStruct((8, LANES), F32)],
        scratch_shapes=[pltpu.VMEM((8, LANES), F32)],
        compiler_params=_cparams(("arbitrary",)),
        name="moe_route",
    )(logits, bias, ltri, eye)


def _slot_of(eid_ref, rank_ref, pstart_ref, idx):
    return pstart_ref[eid_ref[idx]] + rank_ref[idx]


def _dispatch_body(eid_ref, rank_ref, pstart_ref, pad0_ref, npad_ref, nb_ref, h_ref, xs_ref, zbuf, sem, zsem,
                   *, tm, tm_e, nblk):
    i = pl.program_id(0)
    n_tok = pl.num_programs(0) * tm

    def pad_fill(wait):
        def go(cp):
            if wait:
                cp.wait()
            else:
                cp.start()

        def body(e, carry):
            off = pad0_ref[e]
            npad = npad_ref[e]
            head = (SUBLANES - off % SUBLANES) % SUBLANES
            for j in range(SUBLANES - 1):
                @pl.when(j < head)
                def _(j=j):
                    go(pltpu.make_async_copy(zbuf.at[pl.ds(0, 1), :], xs_ref.at[pl.ds(off + j, 1), :], zsem))
            off = pl.multiple_of(off + head, SUBLANES)
            rem = npad - head
            bit = pl.next_power_of_2(tm_e) // 2
            while bit >= SUBLANES:
                on = (rem & bit) != 0

                @pl.when(on)
                def _(off=off, bit=bit):
                    go(pltpu.make_async_copy(zbuf.at[pl.ds(0, bit), :], xs_ref.at[pl.ds(off, bit), :], zsem))

                off = pl.multiple_of(off + jnp.where(on, bit, 0), SUBLANES)
                bit //= 2
            return carry
        lax.fori_loop(0, N_EXPERTS, body, 0)

        def tail(tb, carry):
            go(pltpu.make_async_copy(zbuf, xs_ref.at[pl.ds(pl.multiple_of(tb * tm_e, tm_e), tm_e), :], zsem))
            return carry
        lax.fori_loop(nb_ref[0], nblk, tail, 0)

    @pl.when(i == 0)
    def _():
        zbuf[...] = jnp.zeros(zbuf.shape, F32)
        pad_fill(False)

    for r in range(tm):
        for kk in range(2):
            d = _slot_of(eid_ref, rank_ref, pstart_ref, kk * n_tok + i * tm + r)
            pltpu.make_async_copy(h_ref.at[pl.ds(r, 1), :], xs_ref.at[pl.ds(d, 1), :], sem).start(
                priority=(r + kk) % 2)

    @pl.when(i == 0)
    def _():
        pad_fill(True)

    for _ in range(2 * tm):
        pltpu.make_async_copy(h_ref.at[pl.ds(0, 1), :], xs_ref.at[pl.ds(0, 1), :], sem).wait()


def _dispatch(h, eid, rank, p_start, pad0, npad, nb_used, nblk, tm, tm_e):
    n, d = h.shape
    return pl.pallas_call(
        functools.partial(_dispatch_body, tm=tm, tm_e=tm_e, nblk=nblk),
        grid_spec=pltpu.PrefetchScalarGridSpec(
            num_scalar_prefetch=6,
            grid=(n // tm,),
            in_specs=[pl.BlockSpec((tm, d), lambda i, *_: (i, 0))],
            out_specs=pl.BlockSpec(memory_space=pl.ANY),
            scratch_shapes=[pltpu.VMEM((tm_e, d), F32),
                            pltpu.SemaphoreType.DMA(()),
                            pltpu.SemaphoreType.DMA(())]),
        out_shape=jax.ShapeDtypeStruct((nblk * tm_e, d), F32),
        compiler_params=_cparams(("arbitrary",)),
        name="moe_dispatch",
    )(eid, rank, p_start, pad0, npad, nb_used, h)


def _experts_body(be_ref, nb_ref, first_ref, nxt_ref, slot_ref, x_ref, w1_hbm, w3_hbm, w2_hbm, y_ref,
                  wb1, wb3, wb2, sem, *, layer):
    b = pl.program_id(0)
    used = b < nb_ref[0]
    s = slot_ref[b]

    def weight_copies(e, slot):
        return (pltpu.make_async_copy(w1_hbm.at[layer, e], wb1.at[slot], sem.at[slot, 0]),
                pltpu.make_async_copy(w3_hbm.at[layer, e], wb3.at[slot], sem.at[slot, 1]),
                pltpu.make_async_copy(w2_hbm.at[layer, e], wb2.at[slot], sem.at[slot, 2]))

    @pl.when(b == 0)
    def _():
        for cp in weight_copies(be_ref[0], 0):
            cp.start()

    @pl.when(jnp.logical_and(used, first_ref[b] == 1))
    def _():
        for cp in weight_copies(be_ref[b], s):
            cp.wait()

        @pl.when(nxt_ref[b] >= 0)
        def _():
            for cp in weight_copies(nxt_ref[b], 1 - s):
                cp.start()

    @pl.when(used)
    def _():
        x = x_ref[...]
        a = _dot(x, wb1[s])
        g = _dot(x, wb3[s])
        y_ref[...] = _dot(_silu(a) * g, wb2[s])

    @pl.when(jnp.logical_not(used))
    def _():
        y_ref[...] = jnp.zeros(y_ref.shape, F32)


def _experts(xs, block_e, nb_used, first, nxt_e, slot, w1, w3, w2, layer, tm):
    s, d = xs.shape
    f = w1.shape[-1]
    nblk = s // tm
    return pl.pallas_call(
        functools.partial(_experts_body, layer=layer),
        grid_spec=pltpu.PrefetchScalarGridSpec(
            num_scalar_prefetch=5,
            grid=(nblk,),
            in_specs=[pl.BlockSpec((tm, d), lambda b, be, nb, *_: (jnp.minimum(b, nb[0] - 1), 0)),
                      pl.BlockSpec(memory_space=pl.ANY),
                      pl.BlockSpec(memory_space=pl.ANY),
                      pl.BlockSpec(memory_space=pl.ANY)],
            out_specs=pl.BlockSpec((tm, d), lambda b, *_: (b, 0)),
            scratch_shapes=[pltpu.VMEM((2, d, f), F32),
                            pltpu.VMEM((2, d, f), F32),
                            pltpu.VMEM((2, f, d), F32),
                            pltpu.SemaphoreType.DMA((2, 3))]),
        out_shape=jax.ShapeDtypeStruct((s, d), F32),
        compiler_params=_cparams(("arbitrary",)),
        name="moe_experts",
    )(block_e, nb_used, first, nxt_e, slot, xs, w1, w3, w2)


def _combine_body(eid_ref, rank_ref, pstart_ref, x_ref, meta_ref, gfin_ref, ys_ref, *rest, tm, n_first, final):
    if final:
        o_a_ref, o_b_ref, buf, sem = rest
    else:
        o_a_ref, buf, sem = rest
        o_b_ref = None
    i = pl.program_id(0)
    nsteps = pl.num_programs(0)

    def issue(step, slot):
        for r in range(tm):
            for kk in range(2):
                d = _slot_of(eid_ref, rank_ref, pstart_ref, kk * (nsteps * tm) + step * tm + r)
                pltpu.make_async_copy(ys_ref.at[pl.ds(d, 1), :], buf.at[slot, kk, pl.ds(r, 1), :],
                                      sem.at[slot]).start(priority=(r + kk) % 2)

    @pl.when(i == 0)
    def _():
        issue(0, 0)

    @pl.when(i + 1 < nsteps)
    def _():
        issue(i + 1, (i + 1) % 2)

    slot = i % 2
    for _ in range(2 * tm):
        pltpu.make_async_copy(ys_ref.at[pl.ds(0, 1), :], buf.at[slot, 0, pl.ds(0, 1), :], sem.at[slot]).wait()

    meta = meta_ref[...]
    g0 = meta[:, _META_G0:_META_G0 + 1]
    g1 = meta[:, _META_G1:_META_G1 + 1]
    out = x_ref[...] + (buf[slot, 0] * g0 + buf[slot, 1] * g1)
    if not final:
        o_a_ref[...] = out
    else:
        out = _rms(out, gfin_ref[...])

        @pl.when(i < n_first)
        def _():
            o_a_ref[...] = out

        @pl.when(i >= n_first)
        def _():
            o_b_ref[...] = out


def _combine(x, meta, eid, rank, p_start, ys, gfin, tm, n_first_rows, final):
    n, d = x.shape
    n_first = n_first_rows // tm
    if final:
        out_specs = [pl.BlockSpec((tm, d), lambda i, *_: (jnp.minimum(i, n_first - 1), 0)),
                     pl.BlockSpec((tm, d), lambda i, *_: (jnp.maximum(i - n_first, 0), 0))]
        out_shape = [jax.ShapeDtypeStruct((n_first_rows, d), F32),
                     jax.ShapeDtypeStruct((n - n_first_rows, d), F32)]
    else:
        out_specs = [pl.BlockSpec((tm, d), lambda i, *_: (i, 0))]
        out_shape = [jax.ShapeDtypeStruct((n, d), F32)]
    return pl.pallas_call(
        functools.partial(_combine_body, tm=tm, n_first=n_first, final=final),
        grid_spec=pltpu.PrefetchScalarGridSpec(
            num_scalar_prefetch=3,
            grid=(n // tm,),
            in_specs=[pl.BlockSpec((tm, d), lambda i, *_: (i, 0)),
                      pl.BlockSpec((tm, LANES), lambda i, *_: (i, 0)),
                      pl.BlockSpec((1, d), lambda i, *_: (0, 0)),
                      pl.BlockSpec(memory_space=pl.ANY)],
            out_specs=out_specs,
            scratch_shapes=[pltpu.VMEM((2, 2, tm, d), F32),
                            pltpu.SemaphoreType.DMA((2,))]),
        out_shape=out_shape,
        compiler_params=_cparams(("arbitrary",)),
        name="moe_combine_final" if final else "moe_combine",
    )(eid, rank, p_start, x, meta, gfin.reshape(1, d), ys)


def _moe(x1, h, logits, rbias, w1, w3, w2, layer, gfin, n_first_rows, final, tm_e=192, tm_t=128):
    n, d = h.shape
    meta, idx, cnt = _route(logits, rbias, 256)
    idx = idx.astype(jnp.int32)
    eid = idx[:, 0:2, :].transpose(1, 0, 2).reshape(-1)
    rank = idx[:, 2:4, :].transpose(1, 0, 2).reshape(-1)
    counts = cnt[0, _EXP_LANE0:_EXP_LANE0 + N_EXPERTS].astype(jnp.int32)
    padded = (counts + tm_e - 1) // tm_e * tm_e
    p_end = jnp.cumsum(padded)
    p_start = (p_end - padded).astype(jnp.int32)
    s = n * 2
    nblk = (s + N_EXPERTS * (tm_e - 1) + tm_e - 1) // tm_e
    bidx = jnp.arange(nblk, dtype=jnp.int32)
    block_e = jnp.minimum(jnp.sum((p_end[None, :] <= (bidx * tm_e)[:, None]).astype(jnp.int32), axis=1),
                          N_EXPERTS - 1).astype(jnp.int32)
    nb_used = (p_end[-1:] // tm_e).astype(jnp.int32)
    prev_e = jnp.concatenate([jnp.full((1,), -1, jnp.int32), block_e[:-1]])
    first = ((bidx < nb_used[0]) & (block_e != prev_e)).astype(jnp.int32)
    slot = ((jnp.cumsum(first) - 1) % 2).astype(jnp.int32)
    eidx = jnp.arange(N_EXPERTS, dtype=jnp.int32)
    live = jnp.where(counts > 0, eidx, N_EXPERTS)
    next_live = jnp.min(jnp.where(eidx[None, :] > eidx[:, None], live[None, :], N_EXPERTS), axis=1)
    nxt_e = jnp.sum(jnp.where(block_e[:, None] == eidx[None, :], next_live[None, :], 0), axis=1)
    nxt_e = jnp.where(nxt_e < N_EXPERTS, nxt_e, -1).astype(jnp.int32)
    xs = _dispatch(h, eid, rank, p_start, (p_start + counts).astype(jnp.int32),
                   (padded - counts).astype(jnp.int32), nb_used, nblk, tm_t, tm_e)
    ys = _experts(xs, block_e, nb_used, first, nxt_e, slot, w1, w3, w2, layer, tm_e)
    return _combine(x1, meta, eid, rank, p_start, ys, gfin, tm_t, n_first_rows, final)


def _lambda_value(lam_ref, lambda_init):
    lam = lam_ref[...]
    s1 = jnp.sum(lam[0:1] * lam[1:2], axis=1, keepdims=True)
    s2 = jnp.sum(lam[2:3] * lam[3:4], axis=1, keepdims=True)
    return jnp.exp(s1) - jnp.exp(s2) + lambda_init


def _lane_tile(x, width):
    if width % LANES == 0:
        return jnp.concatenate([x] * (width // LANES), axis=1)
    return x[:, :width]


def _softmax_step(j, s, v, m_ref, l_ref, acc_ref, rows=slice(None)):
    m_prev = m_ref[j, rows]
    m_new = jnp.maximum(m_prev, jnp.max(s, axis=1, keepdims=True))
    alpha = jnp.exp(m_prev - m_new)
    p = jnp.exp(s - _lane_tile(m_new, s.shape[1]))
    l_ref[j, rows] = alpha * l_ref[j, rows] + jnp.sum(p, axis=1, keepdims=True)
    acc = acc_ref[j, rows]
    acc_ref[j, rows] = acc * _lane_tile(alpha, acc.shape[1]) + _dot(p, v)
    m_ref[j, rows] = m_new


def _attn_finish(lam_ref, g_ref, l_ref, acc_ref, lambda_init):
    lam = _lambda_value(lam_ref, lambda_init)
    width = acc_ref.shape[-1]
    o = acc_ref[0] / _lane_tile(l_ref[0], width) - lam * (acc_ref[1] / _lane_tile(l_ref[1], width))
    return _rms(o, g_ref[...]) * (1.0 - lambda_init)


def _alibi_slopes():
    slopes = [2.0 ** (-8.0 * (h + 1) / DIFF_HEADS) for h in range(DIFF_HEADS)]
    assert all(math.frexp(s)[0] == 0.5 for s in slopes)
    return slopes


def _attn_prompt_body(iq_ref, ik_ref, slope_ref, q_ref, k_ref, v_ref, kt_ref, lam_ref, g_ref,
                      o_ref, qa_ref, m_ref, l_ref, acc_ref, *, tq, lambda_init):
    h = pl.program_id(1)
    t = pl.program_id(2)
    iq = iq_ref[t]
    ik = ik_ref[t]
    dh = DIFF_DH
    slope = slope_ref[h]

    @pl.when(ik == 0)
    def _():
        m_ref[...] = jnp.full(m_ref.shape, -jnp.inf, F32)
        l_ref[...] = jnp.zeros(l_ref.shape, F32)
        acc_ref[...] = jnp.zeros(acc_ref.shape, F32)
        qpos = iq * tq + lax.broadcasted_iota(jnp.int32, (tq, LANES), 0)
        lane = lax.broadcasted_iota(jnp.int32, (tq, LANES), 1)
        qa = (qpos // CHUNK).astype(F32)
        qb = (qpos % CHUNK).astype(F32)
        feat = jnp.where(lane == 0, qa * (-slope * CHUNK),
                         jnp.where(lane == 1, qb * (-slope), jnp.where(lane < 4, slope, 0.0)))
        q = q_ref[...]
        for j in range(2):
            qa_ref[j] = jnp.concatenate([q[:, j * dh:(j + 1) * dh] * (dh ** -0.5), feat], axis=1)

    k = k_ref[...]
    v = v_ref[...]
    kt = kt_ref[...]

    ts = tq // ATTN_ROW_SPLIT
    ka = [jnp.concatenate([k[:, j * dh:(j + 1) * dh], kt], axis=1) for j in range(2)]

    @pl.when(ik < iq)
    def _():
        for r in range(ATTN_ROW_SPLIT):
            rows = slice(r * ts, (r + 1) * ts)
            for j in range(2):
                _softmax_step(j, _dot_nt(qa_ref[j, rows], ka[j]), v, m_ref, l_ref, acc_ref, rows)

    @pl.when(ik == iq)
    def _():
        for r in range(ATTN_ROW_SPLIT):
            rows = slice(r * ts, (r + 1) * ts)
            row = r * ts + lax.broadcasted_iota(jnp.int32, (ts, tq), 0)
            col = lax.broadcasted_iota(jnp.int32, (ts, tq), 1)
            fix = jnp.maximum(col - row, 0).astype(F32) * (-2.0 * slope)
            vis = (col // CHUNK) <= (row // CHUNK)
            for j in range(2):
                s = _dot_nt(qa_ref[j, rows], ka[j])
                _softmax_step(j, jnp.where(vis, s + fix, -jnp.inf), v, m_ref, l_ref, acc_ref, rows)
        o_ref[...] = _attn_finish(lam_ref, g_ref, l_ref, acc_ref, lambda_init)


def _attn_prompt(q, k, v, nb, t, lam, g_subln, lambda_init, tq):
    hw = 2 * DIFF_DH
    assert t % tq == 0 and tq % CHUNK == 0 and t // CHUNK <= 256
    nq = t // tq
    pairs = [(iq, ik) for iq in range(nq) for ik in range(iq + 1)]
    iq_tab = jnp.asarray(np.array([p[0] for p in pairs], np.int32))
    ik_tab = jnp.asarray(np.array([p[1] for p in pairs], np.int32))
    slopes = jnp.asarray(np.array(_alibi_slopes(), np.float32))
    pos = np.arange(t)
    ktab = np.zeros((t, LANES), np.float32)
    ktab[:, 0] = 1.0
    ktab[:, 1] = 1.0
    ktab[:, 2] = (pos // CHUNK) * CHUNK
    ktab[:, 3] = pos % CHUNK
    g3 = g_subln.reshape(DIFF_HEADS, 1, hw)
    return pl.pallas_call(
        functools.partial(_attn_prompt_body, tq=tq, lambda_init=lambda_init),
        grid_spec=pltpu.PrefetchScalarGridSpec(
            num_scalar_prefetch=3,
            grid=(nb, DIFF_HEADS, len(pairs)),
            in_specs=[pl.BlockSpec((tq, hw), lambda b, h, p, iqt, ikt, sl: (b * nq + iqt[p], h)),
                      pl.BlockSpec((tq, hw), lambda b, h, p, iqt, ikt, sl: (b * nq + ikt[p], h)),
                      pl.BlockSpec((tq, hw), lambda b, h, p, iqt, ikt, sl: (b * nq + ikt[p], h)),
                      pl.BlockSpec((tq, LANES), lambda b, h, p, iqt, ikt, sl: (ikt[p], 0)),
                      pl.BlockSpec((4, DIFF_DH), lambda b, h, p, iqt, ikt, sl: (0, 0)),
                      pl.BlockSpec((None, 1, hw), lambda b, h, p, iqt, ikt, sl: (h, 0, 0))],
            out_specs=pl.BlockSpec((tq, hw), lambda b, h, p, iqt, ikt, sl: (b * nq + iqt[p], h)),
            scratch_shapes=[pltpu.VMEM((2, tq, hw), F32),
                            pltpu.VMEM((2, tq, LANES), F32),
                            pltpu.VMEM((2, tq, LANES), F32),
                            pltpu.VMEM((2, tq, hw), F32)]),
        out_shape=jax.ShapeDtypeStruct((nb * t, DIFF_HEADS * hw), F32),
        compiler_params=_cparams(("arbitrary", "arbitrary", "arbitrary")),
        name="diff_attn_prompt",
    )(iq_tab, ik_tab, slopes, q, k, v, jnp.asarray(ktab), lam, g3)


def _attn_sample_body(q_ref, kc_ref, vc_ref, kn_ref, vn_ref, lam_ref, g_ref,
                      o_ref, m_ref, l_ref, acc_ref, *, tk, past, lambda_init):
    ik = pl.program_id(1)
    nkb = pl.num_programs(1) - 1
    dh = DIFF_DH
    hw = 2 * dh
    tq = q_ref.shape[0]
    scale = dh ** -0.5

    @pl.when(ik == 0)
    def _():
        m_ref[...] = jnp.full(m_ref.shape, -jnp.inf, F32)
        l_ref[...] = jnp.zeros(l_ref.shape, F32)
        acc_ref[...] = jnp.zeros(acc_ref.shape, F32)

    def attend(h, k, v, k0, width):
        slope = _alibi_slopes()[h]
        q = q_ref[:, h * hw:(h + 1) * hw]
        qpos = past + lax.broadcasted_iota(jnp.int32, (tq, width), 0)
        kpos = k0 + lax.broadcasted_iota(jnp.int32, (tq, width), 1)
        bias = jnp.abs(qpos - kpos).astype(F32) * (-slope)
        vis = (kpos // CHUNK) <= (qpos // CHUNK)
        for j in range(2):
            s = _dot_nt(q[:, j * dh:(j + 1) * dh], k[:, j * dh:(j + 1) * dh]) * scale + bias
            s = jnp.where(vis, s, -jnp.inf)
            _softmax_step(j, s, v, m_ref.at[h], l_ref.at[h], acc_ref.at[h])

    @pl.when(ik < nkb)
    def _():
        for h in range(DIFF_HEADS):
            attend(h, kc_ref[:, h, :], vc_ref[:, h, :], ik * tk, tk)

    @pl.when(ik == nkb)
    def _():
        for h in range(DIFF_HEADS):
            hs = slice(h * hw, (h + 1) * hw)
            attend(h, kn_ref[:, hs], vn_ref[:, hs], past, tq)
            o_ref[:, hs] = _attn_finish(lam_ref, g_ref.at[h], l_ref.at[h], acc_ref.at[h], lambda_init)


def _attn_sample(q, k, v, cache_k, cache_v, li, lam, g_subln, lambda_init, tk):
    n_rows = q.shape[0]
    _, nb, past, nh, hw = cache_k.shape
    aw = nh * hw
    tq = n_rows // nb
    tk = min(tk, past)
    assert past % tk == 0
    nkb = past // tk
    g3 = g_subln.reshape(nh, 1, hw)
    cache_spec = pl.BlockSpec((None, None, tk, nh, hw),
                              lambda b, ik: (li, b, jnp.minimum(ik, nkb - 1), 0, 0))
    row_spec = pl.BlockSpec((tq, aw), lambda b, ik: (b, 0))
    return pl.pallas_call(
        functools.partial(_attn_sample_body, tk=tk, past=past, lambda_init=lambda_init),
        grid=(nb, nkb + 1),
        in_specs=[row_spec,
                  cache_spec, cache_spec,
                  row_spec,
                  row_spec,
                  pl.BlockSpec((4, DIFF_DH), lambda b, ik: (0, 0)),
                  pl.BlockSpec((nh, 1, hw), lambda b, ik: (0, 0, 0))],
        out_specs=pl.BlockSpec((tq, aw), lambda b, ik: (b, 0)),
        scratch_shapes=[pltpu.VMEM((nh, 2, tq, LANES), F32),
                        pltpu.VMEM((nh, 2, tq, LANES), F32),
                        pltpu.VMEM((nh, 2, tq, hw), F32)],
        out_shape=jax.ShapeDtypeStruct((n_rows, aw), F32),
        compiler_params=_cparams(("arbitrary", "arbitrary")),
        name="diff_attn_sample",
    )(q, cache_k, cache_v, k, v, lam, g3)


def _pad_cols(a, width):
    return jnp.pad(a, ((0, 0), (0, width - a.shape[1])))


def _mixer_constants(dk, dv, w_gate_up, b_gate, g_gla, conv_w, conv_b, dt_bias, a_log, d_skip, g_ssd):
    C = REC_ROWS
    nh = GLA_HEADS
    inner = g_ssd.shape[0]
    n_ssd = inner // SSD_HEADDIM
    wup = jnp.pad(w_gate_up, ((0, LANES - w_gate_up.shape[0]), (0, 0)))
    tri = jnp.asarray(np.tril(np.ones((C, C), np.float32)), BF16)
    e64 = np.zeros((LANES, inner), np.float32)
    ec = np.zeros((LANES, n_ssd * C), np.float32)
    for hh in range(n_ssd):
        e64[hh, hh * SSD_HEADDIM:(hh + 1) * SSD_HEADDIM] = 1.0
        ec[hh, hh * C:(hh + 1) * C] = 1.0
    eye = np.tile(np.eye(C, dtype=np.float32), (1, n_ssd))
    caus = np.tile(np.tril(np.ones((C, C), np.float32)), (1, n_ssd))
    return [wup, b_gate.reshape(1, -1), g_gla.reshape(nh, dv), conv_w, conv_b.reshape(1, -1),
            _pad_cols(dt_bias.reshape(1, -1), LANES),
            _pad_cols(-jnp.exp(a_log.astype(F32)).reshape(1, -1), LANES),
            jnp.repeat(d_skip, SSD_HEADDIM).reshape(1, -1), g_ssd.reshape(1, -1),
            tri, jnp.asarray(e64, BF16), jnp.asarray(ec, BF16), jnp.asarray(eye), jnp.asarray(caus)]


def kernel(x_prompt, x_sample, state_gla, state_ssm, state_conv, cache_k, cache_v, norm_mix, norm_ffn, norm_final, w_in, w_gate_up, b_gate, g_gla, conv_w, conv_b, dt_bias, a_log, d_skip, g_ssd, w_out_mix, w_qkv, lam_q1, lam_k1, lam_q2, lam_k2, g_subln, w_o, router_group_w, router_group_b, router_expert_w, router_expert_b, w1, w3, w2):
    bp, tp, d = x_prompt.shape
    bs, ts, _ = x_sample.shape
    n_p, n_s = bp * tp, bs * ts
    n = n_p + n_s
    depth = norm_mix.shape[0]
    nh = GLA_HEADS
    dk, dv = state_gla.shape[-2], state_gla.shape[-1]
    qk_w, vw = nh * dk, nh * dv
    n_ssd = state_ssm.shape[2]
    inner = n_ssd * SSD_HEADDIM
    hpg = n_ssd // SSD_GROUPS
    gs = SSD_GROUPS * SSD_STATE
    cw = inner + 2 * gs
    rank = w_gate_up.shape[1]
    assert w_gate_up.shape[2] == qk_w and rank <= LANES and n_ssd <= LANES

    x = jnp.concatenate([x_prompt.reshape(n_p, d), x_sample.reshape(n_s, d)], axis=0)
    tm_row = next(tm for tm in (1408, 1056, 768, 256) if n % tm == 0)

    i_even = i_odd = 0
    gla_p, ssm_p, conv_p, gla_s, ssm_s, conv_s = [], [], [], [], [], []
    k_p, v_p, k_s, v_s = [], [], [], []
    y_p = y_s = None
    for layer in range(depth):
        final = layer == depth - 1
        if layer % 2 == 0:
            i = i_even
            i_even += 1
            offs = np.cumsum([0, qk_w, qk_w, vw, rank, vw, inner, cw, n_ssd])
            w_t = w_in[i].T
            seg = lambda j: w_t[offs[j]:offs[j + 1]]
            pad_rows = lambda a: jnp.pad(a, ((0, LANES - a.shape[0]), (0, 0)))
            w_cat = jnp.concatenate([seg(0), seg(1), seg(2), seg(4), seg(5), seg(6),
                                     pad_rows(seg(3)), pad_rows(seg(7))], axis=0).astype(BF16)
            proj = _norm_matmul(x, norm_mix[layer], w_cat, tm_row, 1280, w_is_transposed=True)
            consts = _mixer_constants(dk, dv, w_gate_up[i], b_gate[i], g_gla[i], conv_w[i], conv_b[i],
                                      dt_bias[i], a_log[i], d_skip[i], g_ssd[i])

            def to_group_state(sm):
                b_ = sm.shape[0]
                return sm.reshape(b_, SSD_GROUPS, hpg, SSD_STATE, SSD_HEADDIM).transpose(0, 1, 3, 2, 4) \
                         .reshape(b_, SSD_GROUPS, SSD_STATE, hpg * SSD_HEADDIM)

            def from_group_state(sg):
                b_ = sg.shape[0]
                return sg.reshape(b_, SSD_GROUPS, SSD_STATE, hpg, SSD_HEADDIM).transpose(0, 1, 3, 2, 4) \
                         .reshape(b_, n_ssd, SSD_STATE, SSD_HEADDIM)

            def pad_conv(cv):
                return jnp.pad(cv, ((0, 0), (CONV_PAD - cv.shape[1], 0), (0, 0)))

            mix_a, g_fin, s_fin, c_fin = _mixer_call(
                proj, 0, bp, tp, jnp.zeros((bp, nh, dv, dk), F32),
                jnp.zeros((bp, SSD_GROUPS, SSD_STATE, hpg * SSD_HEADDIM), F32),
                jnp.zeros((bp, CONV_PAD, cw), F32), consts, dk, dv)
            gla_p.append(g_fin.transpose(0, 1, 3, 2))
            ssm_p.append(from_group_state(s_fin))
            conv_p.append(c_fin[:, CONV_PAD - (SSD_CONV - 1):])
            mix_b, g_fin, s_fin, c_fin = _mixer_call(
                proj, n_p, bs, ts, state_gla[i].transpose(0, 1, 3, 2),
                to_group_state(state_ssm[i]), pad_conv(state_conv[i]), consts, dk, dv)
            gla_s.append(g_fin.transpose(0, 1, 3, 2))
            ssm_s.append(from_group_state(s_fin))
            conv_s.append(c_fin[:, CONV_PAD - (SSD_CONV - 1):])
            w_proj = w_out_mix[i]
        else:
            i = i_odd
            i_odd += 1
            lambda_init = 0.8 - 0.6 * math.exp(-0.3 * layer)
            tm_p = next(tm for tm in (1024, 512, 256) if n_p % tm == 0)
            w_qkv_b = w_qkv[i].astype(BF16)
            q_a, k_a, v_a = _norm_matmul(x, norm_mix[layer], w_qkv_b, tm_p, 1024, nrows=n_p, parts=3)
            q_b, k_b, v_b = _norm_matmul(x, norm_mix[layer], w_qkv_b, n_s, 1024, row0=n_p, parts=3)
            lam = jnp.stack([lam_q1[i], lam_k1[i], lam_q2[i], lam_k2[i]])
            mix_a = _attn_prompt(q_a, k_a, v_a, bp, tp, lam, g_subln[i], lambda_init, min(512, tp))
            mix_b = _attn_sample(q_b, k_b, v_b, cache_k, cache_v, i, lam, g_subln[i], lambda_init, 1024)
            k_p.append(k_a.reshape(bp, tp, DIFF_HEADS, 2 * DIFF_DH))
            v_p.append(v_a.reshape(bp, tp, DIFF_HEADS, 2 * DIFF_DH))
            k_s.append(k_b.reshape(bs, ts, DIFF_HEADS, 2 * DIFF_DH))
            v_s.append(v_b.reshape(bs, ts, DIFF_HEADS, 2 * DIFF_DH))
            w_proj = w_o[i]

        wr = _pad_cols(jnp.concatenate([router_group_w[layer], router_expert_w[layer]], axis=1), LANES)
        rbias = _pad_cols(jnp.concatenate([router_group_b[layer], router_expert_b[layer]]).reshape(1, -1), LANES)
        x1, h2, logits = _proj_resid(mix_a, mix_b, w_proj, x, norm_ffn[layer], wr, 256)
        res = _moe(x1, h2, logits, rbias, w1, w3, w2, layer, norm_final, n_p, final)
        if final:
            y_p, y_s = res
        else:
            x = res[0]

    return (y_p.reshape(bp, tp, d), y_s.reshape(bs, ts, d),
            jnp.stack(gla_p), jnp.stack(ssm_p), jnp.stack(conv_p), jnp.stack(k_p), jnp.stack(v_p),
            jnp.stack(gla_s), jnp.stack(ssm_s), jnp.stack(conv_s), jnp.stack(k_s), jnp.stack(v_s))
```

```python
import functools
import math

import numpy as np
import jax
import jax.numpy as jnp
from jax import lax
from jax.experimental import pallas as pl
from jax.experimental.pallas import tpu as pltpu

F32 = jnp.float32
BF16 = jnp.bfloat16

EPS = 1e-6
CHUNK = 64
GLA_HEADS = 4
GLA_TAU = 16.0
SSD_HEADDIM = 64
SSD_STATE = 128
SSD_GROUPS = 4
SSD_CONV = 4
DIFF_HEADS = 8
DIFF_DH = 128
MOE_GROUPS = 4
MOE_PER_GROUP = 8
N_EXPERTS = MOE_GROUPS * MOE_PER_GROUP

LANES = 128
SUBLANES = 8
REC_ROWS = 128
STRIP = 16
CONV_PAD = 8
ATTN_ROW_SPLIT = 2
VMEM_LIMIT = 56 * 1024 * 1024


def _cparams(sem, vmem=VMEM_LIMIT):
    return pltpu.CompilerParams(dimension_semantics=sem, vmem_limit_bytes=vmem)


def _dot(a, b):
    return jnp.dot(a, b, preferred_element_type=F32)


def _dot_nt(a, b):
    return lax.dot_general(a, b, (((1,), (1,)), ((), ())), preferred_element_type=F32)


def _dot_tn(a, b):
    return lax.dot_general(a, b, (((0,), (0,)), ((), ())), preferred_element_type=F32)


def _split_hi_lo(a):
    hi = a.astype(BF16)
    lo = (a - hi.astype(F32)).astype(BF16)
    return hi, lo


def _exact_left(m, a):
    hi, lo = _split_hi_lo(a)
    return _dot(m, hi) + _dot(m, lo)


def _exact_right(a, m):
    hi, lo = _split_hi_lo(a)
    return _dot(hi, m) + _dot(lo, m)


def _silu(x):
    return x / (1.0 + jnp.exp(-x))


def _log1p_exp_neg_abs(x):
    e = jnp.exp(-jnp.abs(x))
    u = 1.0 + e
    return jnp.where(u == 1.0, e, jnp.log(u) * (e / (u - 1.0)))


def _rms(x, g):
    return x * lax.rsqrt(jnp.mean(x * x, axis=-1, keepdims=True) + EPS) * g


def _pack_halves(x):
    w = x.shape[1] // 2
    lo = lax.bitcast_convert_type(x[:, :w].astype(BF16).astype(F32), jnp.uint32)
    hi = lax.bitcast_convert_type(x[:, w:].astype(BF16).astype(F32), jnp.uint32)
    return (lo >> 16) | (hi & jnp.uint32(0xFFFF0000))


def _unpack_halves(p):
    lo = lax.bitcast_convert_type(p << 16, F32)
    hi = lax.bitcast_convert_type(p & jnp.uint32(0xFFFF0000), F32)
    return jnp.concatenate([lo, hi], axis=1)


def _norm_matmul_body(x_ref, g_ref, w_ref, *refs, w_is_transposed, parts):
    o_refs, h_ref = refs[:parts], refs[parts]
    j = pl.program_id(1)

    @pl.when(j == 0)
    def _():
        h_ref[...] = _rms(x_ref[...], g_ref[...]).astype(h_ref.dtype)

    res = (_dot_nt if w_is_transposed else _dot)(h_ref[...], w_ref[...])
    if parts == 1:
        o_refs[0][...] = res
    else:
        per = pl.num_programs(1) // parts
        for p in range(parts):
            @pl.when(j // per == p)
            def _(p=p):
                o_refs[p][...] = res


def _norm_matmul(x, g, w, tm, tn, *, row0=0, nrows=None, parts=1, w_is_transposed=False):
    n, d = x.shape
    nrows = n - row0 if nrows is None else nrows
    nout = w.shape[0] if w_is_transposed else w.shape[1]
    assert nrows % tm == 0 and row0 % tm == 0 and nout % (tn * parts) == 0
    blk0 = row0 // tm
    per = nout // tn // parts
    w_spec = (pl.BlockSpec((tn, d), lambda i, j: (j, 0)) if w_is_transposed
              else pl.BlockSpec((d, tn), lambda i, j: (0, j)))
    out_specs = [pl.BlockSpec((tm, tn), lambda i, j, p=p: (i, jnp.clip(j - p * per, 0, per - 1)))
                 for p in range(parts)]
    outs = pl.pallas_call(
        functools.partial(_norm_matmul_body, w_is_transposed=w_is_transposed, parts=parts),
        grid=(nrows // tm, nout // tn),
        in_specs=[pl.BlockSpec((tm, d), lambda i, j: (blk0 + i, 0), pipeline_mode=pl.Buffered(1)),
                  pl.BlockSpec((1, d), lambda i, j: (0, 0)),
                  w_spec],
        out_specs=out_specs,
        out_shape=[jax.ShapeDtypeStruct((nrows, nout // parts), F32)] * parts,
        scratch_shapes=[pltpu.VMEM((tm, d), w.dtype)],
        compiler_params=_cparams(("arbitrary", "arbitrary")),
        name="norm_matmul",
    )(x, g.reshape(1, d), w)
    return outs[0] if parts == 1 else outs


def _mixer_body(p_ref, gla0_ref, ssm0_ref, conv0_ref, wup_ref, bgate_ref, ggla_ref, convw_ref,
                convb_ref, dtb_ref, aneg_ref, dskip_ref, gssd_ref, tri_ref, e64_ref, ec_ref,
                eye_ref, caus_ref,
                o_ref, gla_ref, ssm_ref, ctail_ref,
                sg_ref, ss_ref, ext_ref, b_ref, *, rb, dk, dv):
    C = REC_ROWS
    c = pl.program_id(1)
    nc = pl.num_programs(1)
    nh = GLA_HEADS
    qk_w = nh * dk
    vw = nh * dv
    inner = vw
    gs = SSD_GROUPS * SSD_STATE
    o_q, o_k, o_v = 0, qk_w, 2 * qk_w
    o_r = o_v + vw
    o_z = o_r + vw
    o_x = o_z + inner
    o_g = o_x + inner + 2 * gs
    o_dt = o_g + LANES

    @pl.when(c == 0)
    def _init():
        sg_ref[...] = gla0_ref[...]
        ss_ref[...] = ssm0_ref[...]
        ext_ref[0:CONV_PAD, :] = conv0_ref[...]

    p = p_ref[...]
    if rb < C:
        p = jnp.concatenate([p, jnp.zeros((C - rb, p.shape[1]), F32)], axis=0)

    def rowmask(width):
        return lax.broadcasted_iota(jnp.int32, (C, width), 0) < rb

    q = p[:, o_q:o_q + qk_w] * (dk ** -0.5)
    k = p[:, o_k:o_k + qk_w]
    v = p[:, o_v:o_v + vw]
    r = p[:, o_r:o_r + vw]
    z = p[:, o_z:o_z + inner]
    xbc = p[:, o_x:o_x + inner + 2 * gs]
    glr = p[:, o_g:o_g + LANES]
    dtp = p[:, o_dt:o_dt + LANES]

    tri = tri_ref[...]

    zg = _dot(glr, wup_ref[...]) + bgate_ref[...]
    la = (jnp.minimum(zg, 0.0) - _log1p_exp_neg_abs(zg)) * (1.0 / GLA_TAU)
    if rb < C:
        la = jnp.where(rowmask(qk_w), la, 0.0)
    bcum = _exact_left(tri, la)
    b_ref[...] = bcum

    row_i = lax.broadcasted_iota(jnp.int32, (C, dk), 0)
    prow = lax.broadcasted_iota(jnp.int32, (STRIP, C), 0)
    pcol = lax.broadcasted_iota(jnp.int32, (STRIP, C), 1)
    o_heads = []
    for h in range(nh):
        hs = slice(h * dk, (h + 1) * dk)
        vs = slice(h * dv, (h + 1) * dv)
        bh = bcum[:, hs]
        qh = q[:, hs]
        kh = k[:, hs]
        vh = v[:, vs]
        strips = []
        for i in range(C // STRIP):
            r0 = i * STRIP
            if i == 0:
                ref_row = jnp.zeros((1, dk), F32)
            else:
                ref_row = b_ref[pl.ds(r0 - 1, 1), hs]
            q_i = qh[r0:r0 + STRIP] * jnp.exp(bh[r0:r0 + STRIP] - ref_row)
            e = jnp.where(row_i < r0 + STRIP, ref_row - bh, 0.0)
            k_i = kh * jnp.exp(e)
            s_i = _dot_nt(q_i, k_i)
            strips.append(jnp.where(pcol <= prow + r0, s_i, 0.0))
        pmat = jnp.concatenate(strips, axis=0)
        b_last = b_ref[pl.ds(C - 1, 1), hs]
        st = sg_ref[h]
        o_h = _dot(pmat, vh) + _dot_nt(qh * jnp.exp(bh), st)
        k_st = kh * jnp.exp(b_last - bh)
        sg_ref[h] = st * jnp.exp(b_last) + _dot_tn(vh, k_st)
        o_h = _rms(o_h, ggla_ref[pl.ds(h, 1), :]) * _silu(r[:, vs])
        o_heads.append(o_h)
    o_a = jnp.concatenate(o_heads, axis=1)

    cw = inner + 2 * gs
    ext_ref[CONV_PAD:CONV_PAD + C, :] = xbc
    conv = convb_ref[...]
    for j in range(SSD_CONV):
        conv = conv + convw_ref[pl.ds(j, 1), :] * ext_ref[pl.ds(CONV_PAD - (SSD_CONV - 1) + j, C), :]
    xc = _silu(conv)
    xs = xc[:, :inner]
    bm = xc[:, inner:inner + gs]
    cm = xc[:, inner + gs:cw]
    dtv = dtp + dtb_ref[...]
    dt = jnp.maximum(dtv, 0.0) + _log1p_exp_neg_abs(dtv)
    ld = dt * aneg_ref[...]
    if rb < C:
        dt = jnp.where(rowmask(LANES), dt, 0.0)
        ld = jnp.where(rowmask(LANES), ld, 0.0)
    bs = _exact_left(tri, ld)
    e64 = e64_ref[...]
    dt64 = _exact_right(dt, e64)
    b64 = _exact_right(bs, e64)
    blast64 = b64[C - 1:C, :]
    xdt = xs * dt64
    bc = _exact_right(bs, ec_ref[...])
    br = jnp.sum(eye_ref[...] * bc, axis=0, keepdims=True)
    vis = caus_ref[...] > 0.0
    lf = jnp.where(vis, jnp.exp(jnp.where(vis, bc - br, 0.0)), 0.0)
    hpg = inner // SSD_HEADDIM // SSD_GROUPS
    gw = hpg * SSD_HEADDIM
    lane_head = lax.broadcasted_iota(jnp.int32, (C, gw), 1) // SSD_HEADDIM
    y_groups = []
    for g in range(SSD_GROUPS):
        cm_g = cm[:, g * SSD_STATE:(g + 1) * SSD_STATE]
        bm_g = bm[:, g * SSD_STATE:(g + 1) * SSD_STATE]
        gl = slice(g * gw, (g + 1) * gw)
        gmat = _dot_nt(cm_g, bm_g)
        a4 = jnp.concatenate([gmat] * hpg, axis=1) * lf[:, g * hpg * C:(g + 1) * hpg * C]
        xg = xdt[:, gl]
        x4 = jnp.concatenate([jnp.where(lane_head == hh, xg, 0.0) for hh in range(hpg)], axis=0)
        s_g = ss_ref[g]
        y_g = _dot(a4, x4) + _dot(cm_g, s_g) * jnp.exp(b64[:, gl])
        xw = xg * jnp.exp(blast64[:, gl] - b64[:, gl])
        ss_ref[g] = s_g * jnp.exp(blast64[:, gl]) + _dot_tn(bm_g, xw)
        y_groups.append(y_g)
    y = jnp.concatenate(y_groups, axis=1) + xs * dskip_ref[...]
    y = _rms(y * _silu(z), gssd_ref[...])

    o_full = jnp.concatenate([o_a, y], axis=1)
    o_ref[...] = o_full[:rb]

    @pl.when(c == nc - 1)
    def _fin():
        gla_ref[...] = sg_ref[...]
        ssm_ref[...] = ss_ref[...]
        ctail_ref[...] = ext_ref[pl.ds(rb, CONV_PAD), :]

    ext_ref[0:CONV_PAD, :] = ext_ref[pl.ds(rb, CONV_PAD), :]


def _mixer_call(proj, row_off, nb, t, gla0, ssm0, conv0, consts, dk, dv):
    C = REC_ROWS
    rb = min(t, C)
    assert t % rb == 0 and row_off % rb == 0
    steps = t // rb
    width = proj.shape[1]
    nh = GLA_HEADS
    vw = nh * dv
    cw = conv0.shape[-1]
    blk0 = row_off // rb

    def full(a):
        nd = a.ndim
        return pl.BlockSpec(a.shape, lambda b, c, _n=nd: (0,) * _n)

    in_specs = [pl.BlockSpec((rb, width), lambda b, c: (blk0 + b * steps + c, 0)),
                pl.BlockSpec((None, nh, dv, dk), lambda b, c: (b, 0, 0, 0)),
                pl.BlockSpec((None, SSD_GROUPS, SSD_STATE, ssm0.shape[-1]), lambda b, c: (b, 0, 0, 0)),
                pl.BlockSpec((None, CONV_PAD, cw), lambda b, c: (b, 0, 0))]
    in_specs += [full(a) for a in consts]
    operands = [proj, gla0, ssm0, conv0, *consts]
    out_specs = [pl.BlockSpec((rb, 2 * vw), lambda b, c: (b * steps + c, 0)),
                 pl.BlockSpec((None, nh, dv, dk), lambda b, c: (b, 0, 0, 0)),
                 pl.BlockSpec((None, SSD_GROUPS, SSD_STATE, ssm0.shape[-1]), lambda b, c: (b, 0, 0, 0)),
                 pl.BlockSpec((None, CONV_PAD, cw), lambda b, c: (b, 0, 0))]
    out_shape = [jax.ShapeDtypeStruct((nb * t, 2 * vw), F32),
                 jax.ShapeDtypeStruct(gla0.shape, F32),
                 jax.ShapeDtypeStruct(ssm0.shape, F32),
                 jax.ShapeDtypeStruct((nb, CONV_PAD, cw), F32)]
    return pl.pallas_call(
        functools.partial(_mixer_body, rb=rb, dk=dk, dv=dv),
        grid=(nb, steps),
        in_specs=in_specs,
        out_specs=out_specs,
        out_shape=out_shape,
        scratch_shapes=[pltpu.VMEM((nh, dv, dk), F32),
                        pltpu.VMEM((SSD_GROUPS, SSD_STATE, ssm0.shape[-1]), F32),
                        pltpu.VMEM((C + CONV_PAD, cw), F32),
                        pltpu.VMEM((C, nh * dk), F32)],
        compiler_params=_cparams(("arbitrary", "arbitrary")),
        name="gla_ssd_mixer",
    )(*operands)


def _proj_resid_body(oa_ref, ob_ref, w_ref, x_ref, g_ref, wr_ref, x1_ref, h_ref, lg_ref, *, n_first):
    o = jnp.where(pl.program_id(0) < n_first, oa_ref[...], ob_ref[...])
    x1 = x_ref[...] + _dot(o, w_ref[...])
    x1_ref[...] = x1
    h = _rms(x1, g_ref[...])
    h_ref[...] = _pack_halves(h)
    lg_ref[...] = _dot(h, wr_ref[...])


def _packed_row(d):
    return jax.eval_shape(_pack_halves, jax.ShapeDtypeStruct((SUBLANES, d), F32))


def _proj_resid(o_a, o_b, w, x, g, wr, tm):
    n, d = x.shape
    kin = o_a.shape[1]
    assert o_a.shape[0] % tm == 0 and o_b.shape[0] % tm == 0
    n_first = o_a.shape[0] // tm
    hp = _packed_row(d)
    return pl.pallas_call(
        functools.partial(_proj_resid_body, n_first=n_first),
        grid=(n // tm,),
        in_specs=[pl.BlockSpec((tm, kin), lambda i: (jnp.minimum(i, n_first - 1), 0)),
                  pl.BlockSpec((tm, kin), lambda i: (jnp.maximum(i - n_first, 0), 0)),
                  pl.BlockSpec((kin, d), lambda i: (0, 0), pipeline_mode=pl.Buffered(1)),
                  pl.BlockSpec((tm, d), lambda i: (i, 0)),
                  pl.BlockSpec((1, d), lambda i: (0, 0)),
                  pl.BlockSpec((d, LANES), lambda i: (0, 0))],
        out_specs=[pl.BlockSpec((tm, d), lambda i: (i, 0)),
                   pl.BlockSpec((tm, hp.shape[1]), lambda i: (i, 0)),
                   pl.BlockSpec((tm, LANES), lambda i: (i, 0))],
        out_shape=[jax.ShapeDtypeStruct((n, d), F32),
                   jax.ShapeDtypeStruct((n, hp.shape[1]), hp.dtype),
                   jax.ShapeDtypeStruct((n, LANES), F32)],
        compiler_params=_cparams(("arbitrary",)),
        name="proj_resid_norm_router",
    )(o_a, o_b, w, x, g.reshape(1, d), wr)


_META_E0, _META_E1, _META_G0, _META_G1, _META_R0, _META_R1 = range(6)
_EXP_LANE0 = MOE_GROUPS


def _route_body(lg_ref, bias_ref, ltri_ref, eye_ref, meta_ref, idx_ref, cnt_ref, base_ref):
    i = pl.program_id(0)

    @pl.when(i == 0)
    def _():
        base_ref[...] = jnp.zeros_like(base_ref)

    lg = lg_ref[...] + bias_ref[...]
    tm = lg.shape[0]
    lane_i = lax.broadcasted_iota(jnp.int32, (tm, LANES), 1)
    lane = lane_i.astype(F32)
    neg = -jnp.inf
    glog = jnp.where(lane_i < MOE_GROUPS, lg, neg)
    gmax = jnp.max(glog, axis=1, keepdims=True)
    gsel = jnp.min(jnp.where(glog == gmax, lane, float(LANES)), axis=1, keepdims=True)
    pg = 1.0 / jnp.sum(jnp.exp(glog - gmax), axis=1, keepdims=True)
    lo = _EXP_LANE0 + MOE_PER_GROUP * gsel
    el = jnp.where((lane >= lo) & (lane < lo + MOE_PER_GROUP), lg, neg)
    v1 = jnp.max(el, axis=1, keepdims=True)
    i1 = jnp.min(jnp.where(el == v1, lane, float(LANES)), axis=1, keepdims=True)
    el2 = jnp.where(lane == i1, neg, el)
    v2 = jnp.max(el2, axis=1, keepdims=True)
    i2 = jnp.min(jnp.where(el2 == v2, lane, float(LANES)), axis=1, keepdims=True)
    e = jnp.exp(v2 - v1)
    g1 = pg / (1.0 + e)
    g2 = pg * e / (1.0 + e)
    hot1 = lane == i1
    hot2 = lane == i2
    onehot = jnp.where(hot1 | hot2, 1.0, 0.0)
    before = _dot(ltri_ref[...], onehot.astype(BF16)) + base_ref[0:1, :]
    r1 = jnp.sum(jnp.where(hot1, before, 0.0), axis=1, keepdims=True)
    r2 = jnp.sum(jnp.where(hot2, before, 0.0), axis=1, keepdims=True)
    base_ref[0:1, :] = base_ref[0:1, :] + jnp.sum(onehot, axis=0, keepdims=True)
    meta = jnp.zeros((tm, LANES), F32)
    for idx, val in ((_META_E0, i1 - _EXP_LANE0), (_META_E1, i2 - _EXP_LANE0),
                     (_META_G0, g1), (_META_G1, g2), (_META_R0, r1), (_META_R1, r2)):
        meta = jnp.where(lane_i == idx, val, meta)
    meta_ref[...] = meta

    eye = eye_ref[...]
    rows = [jnp.sum(eye * col, axis=0, keepdims=True)
            for col in (i1 - _EXP_LANE0, i2 - _EXP_LANE0, r1, r2)]
    idx_ref[...] = jnp.concatenate(rows + [jnp.zeros((SUBLANES - len(rows), tm), F32)], axis=0)

    @pl.when(i == pl.num_programs(0) - 1)
    def _():
        cnt_ref[...] = base_ref[...]


def _route(logits, bias, tm):
    n = logits.shape[0]
    ltri = jnp.asarray(np.tril(np.ones((tm, tm), np.float32), -1), BF16)
    eye = jnp.asarray(np.eye(tm, dtype=np.float32))
    return pl.pallas_call(
        _route_body,
        grid=(n // tm,),
        in_specs=[pl.BlockSpec((tm, LANES), lambda i: (i, 0)),
                  pl.BlockSpec((1, LANES), lambda i: (0, 0)),
                  pl.BlockSpec((tm, tm), lambda i: (0, 0)),
                  pl.BlockSpec((tm, tm), lambda i: (0, 0))],
        out_specs=[pl.BlockSpec((tm, LANES), lambda i: (i, 0)),
                   pl.BlockSpec((None, SUBLANES, tm), lambda i: (i, 0, 0)),
                   pl.BlockSpec((8, LANES), lambda i: (0, 0))],
        out_shape=[jax.ShapeDtypeStruct((n, LANES), F32),
                   jax.ShapeDtypeStruct((n // tm, SUBLANES, tm), F32),
                   jax.ShapeDtypeStruct((8, LANES), F32)],
        scratch_shapes=[pltpu.VMEM((8, LANES), F32)],
        compiler_params=_cparams(("arbitrary",)),
        name="moe_route",
    )(logits, bias, ltri, eye)


def _slot_of(eid_ref, rank_ref, pstart_ref, idx):
    return pstart_ref[eid_ref[idx]] + rank_ref[idx]


def _dispatch_body(eid_ref, rank_ref, pstart_ref, pad0_ref, npad_ref, nb_ref, h_ref, xs_ref, zbuf, sem, zsem,
                   *, tm, tm_e, nblk):
    i = pl.program_id(0)
    n_tok = pl.num_programs(0) * tm

    def pad_fill(wait):
        def go(cp):
            if wait:
                cp.wait()
            else:
                cp.start()

        def body(e, carry):
            off = pad0_ref[e]
            npad = npad_ref[e]
            head = (SUBLANES - off % SUBLANES) % SUBLANES
            for j in range(SUBLANES - 1):
                @pl.when(j < head)
                def _(j=j):
                    go(pltpu.make_async_copy(zbuf.at[pl.ds(0, 1), :], xs_ref.at[pl.ds(off + j, 1), :], zsem))
            off = pl.multiple_of(off + head, SUBLANES)
            rem = npad - head
            bit = pl.next_power_of_2(tm_e) // 2
            while bit >= SUBLANES:
                on = (rem & bit) != 0

                @pl.when(on)
                def _(off=off, bit=bit):
                    go(pltpu.make_async_copy(zbuf.at[pl.ds(0, bit), :], xs_ref.at[pl.ds(off, bit), :], zsem))

                off = pl.multiple_of(off + jnp.where(on, bit, 0), SUBLANES)
                bit //= 2
            return carry
        lax.fori_loop(0, N_EXPERTS, body, 0)

        def tail(tb, carry):
            go(pltpu.make_async_copy(zbuf, xs_ref.at[pl.ds(pl.multiple_of(tb * tm_e, tm_e), tm_e), :], zsem))
            return carry
        lax.fori_loop(nb_ref[0], nblk, tail, 0)

    @pl.when(i == 0)
    def _():
        zbuf[...] = jnp.zeros(zbuf.shape, zbuf.dtype)
        pad_fill(False)

    for r in range(tm):
        for kk in range(2):
            d = _slot_of(eid_ref, rank_ref, pstart_ref, kk * n_tok + i * tm + r)
            pltpu.make_async_copy(h_ref.at[pl.ds(r, 1), :], xs_ref.at[pl.ds(d, 1), :], sem).start(
                priority=(r + kk) % 2)

    @pl.when(i == 0)
    def _():
        pad_fill(True)

    for _ in range(2 * tm):
        pltpu.make_async_copy(h_ref.at[pl.ds(0, 1), :], xs_ref.at[pl.ds(0, 1), :], sem).wait()


def _dispatch(h, eid, rank, p_start, pad0, npad, nb_used, nblk, tm, tm_e):
    n, d = h.shape
    return pl.pallas_call(
        functools.partial(_dispatch_body, tm=tm, tm_e=tm_e, nblk=nblk),
        grid_spec=pltpu.PrefetchScalarGridSpec(
            num_scalar_prefetch=6,
            grid=(n // tm,),
            in_specs=[pl.BlockSpec((tm, d), lambda i, *_: (i, 0))],
            out_specs=pl.BlockSpec(memory_space=pl.ANY),
            scratch_shapes=[pltpu.VMEM((tm_e, d), h.dtype),
                            pltpu.SemaphoreType.DMA(()),
                            pltpu.SemaphoreType.DMA(())]),
        out_shape=jax.ShapeDtypeStruct((nblk * tm_e, d), h.dtype),
        compiler_params=_cparams(("arbitrary",)),
        name="moe_dispatch",
    )(eid, rank, p_start, pad0, npad, nb_used, h)


def _experts_body(be_ref, nb_ref, first_ref, nxt_ref, slot_ref, x_ref, w1_hbm, w3_hbm, w2_hbm, y_ref,
                  wb1, wb3, wb2, sem, *, layer):
    b = pl.program_id(0)
    used = b < nb_ref[0]
    s = slot_ref[b]

    def weight_copies(e, slot):
        return (pltpu.make_async_copy(w1_hbm.at[layer, e], wb1.at[slot], sem.at[slot, 0]),
                pltpu.make_async_copy(w3_hbm.at[layer, e], wb3.at[slot], sem.at[slot, 1]),
                pltpu.make_async_copy(w2_hbm.at[layer, e], wb2.at[slot], sem.at[slot, 2]))

    @pl.when(b == 0)
    def _():
        for cp in weight_copies(be_ref[0], 0):
            cp.start()

    @pl.when(jnp.logical_and(used, first_ref[b] == 1))
    def _():
        for cp in weight_copies(be_ref[b], s):
            cp.wait()

        @pl.when(nxt_ref[b] >= 0)
        def _():
            for cp in weight_copies(nxt_ref[b], 1 - s):
                cp.start()

    @pl.when(used)
    def _():
        x = _unpack_halves(x_ref[...])
        a = _dot(x, wb1[s])
        g = _dot(x, wb3[s])
        y_ref[...] = _pack_halves(_dot(_silu(a) * g, wb2[s]))

    @pl.when(jnp.logical_not(used))
    def _():
        y_ref[...] = jnp.zeros(y_ref.shape, y_ref.dtype)


def _experts(xs, block_e, nb_used, first, nxt_e, slot, w1, w3, w2, layer, tm):
    s, pw = xs.shape
    d, f = w1.shape[-2], w1.shape[-1]
    nblk = s // tm
    return pl.pallas_call(
        functools.partial(_experts_body, layer=layer),
        grid_spec=pltpu.PrefetchScalarGridSpec(
            num_scalar_prefetch=5,
            grid=(nblk,),
            in_specs=[pl.BlockSpec((tm, pw), lambda b, be, nb, *_: (jnp.minimum(b, nb[0] - 1), 0)),
                      pl.BlockSpec(memory_space=pl.ANY),
                      pl.BlockSpec(memory_space=pl.ANY),
                      pl.BlockSpec(memory_space=pl.ANY)],
            out_specs=pl.BlockSpec((tm, pw), lambda b, *_: (b, 0)),
            scratch_shapes=[pltpu.VMEM((2, d, f), F32),
                            pltpu.VMEM((2, d, f), F32),
                            pltpu.VMEM((2, f, d), F32),
                            pltpu.SemaphoreType.DMA((2, 3))]),
        out_shape=jax.ShapeDtypeStruct((s, pw), xs.dtype),
        compiler_params=_cparams(("arbitrary",)),
        name="moe_experts",
    )(block_e, nb_used, first, nxt_e, slot, xs, w1, w3, w2)


def _combine_body(eid_ref, rank_ref, pstart_ref, x_ref, meta_ref, gfin_ref, ys_ref, *rest, tm, n_first, final):
    if final:
        o_a_ref, o_b_ref, buf, sem = rest
    else:
        o_a_ref, buf, sem = rest
        o_b_ref = None
    i = pl.program_id(0)
    nsteps = pl.num_programs(0)

    def issue(step, slot):
        for r in range(tm):
            for kk in range(2):
                d = _slot_of(eid_ref, rank_ref, pstart_ref, kk * (nsteps * tm) + step * tm + r)
                pltpu.make_async_copy(ys_ref.at[pl.ds(d, 1), :], buf.at[slot, kk, pl.ds(r, 1), :],
                                      sem.at[slot]).start(priority=(r + kk) % 2)

    @pl.when(i == 0)
    def _():
        issue(0, 0)

    @pl.when(i + 1 < nsteps)
    def _():
        issue(i + 1, (i + 1) % 2)

    slot = i % 2
    for _ in range(2 * tm):
        pltpu.make_async_copy(ys_ref.at[pl.ds(0, 1), :], buf.at[slot, 0, pl.ds(0, 1), :], sem.at[slot]).wait()

    meta = meta_ref[...]
    g0 = meta[:, _META_G0:_META_G0 + 1]
    g1 = meta[:, _META_G1:_META_G1 + 1]
    out = x_ref[...] + (_unpack_halves(buf[slot, 0]) * g0 + _unpack_halves(buf[slot, 1]) * g1)
    if not final:
        o_a_ref[...] = out
    else:
        out = _rms(out, gfin_ref[...])

        @pl.when(i < n_first)
        def _():
            o_a_ref[...] = out

        @pl.when(i >= n_first)
        def _():
            o_b_ref[...] = out


def _combine(x, meta, eid, rank, p_start, ys, gfin, tm, n_first_rows, final):
    n, d = x.shape
    n_first = n_first_rows // tm
    if final:
        out_specs = [pl.BlockSpec((tm, d), lambda i, *_: (jnp.minimum(i, n_first - 1), 0)),
                     pl.BlockSpec((tm, d), lambda i, *_: (jnp.maximum(i - n_first, 0), 0))]
        out_shape = [jax.ShapeDtypeStruct((n_first_rows, d), F32),
                     jax.ShapeDtypeStruct((n - n_first_rows, d), F32)]
    else:
        out_specs = [pl.BlockSpec((tm, d), lambda i, *_: (i, 0))]
        out_shape = [jax.ShapeDtypeStruct((n, d), F32)]
    return pl.pallas_call(
        functools.partial(_combine_body, tm=tm, n_first=n_first, final=final),
        grid_spec=pltpu.PrefetchScalarGridSpec(
            num_scalar_prefetch=3,
            grid=(n // tm,),
            in_specs=[pl.BlockSpec((tm, d), lambda i, *_: (i, 0)),
                      pl.BlockSpec((tm, LANES), lambda i, *_: (i, 0)),
                      pl.BlockSpec((1, d), lambda i, *_: (0, 0)),
                      pl.BlockSpec(memory_space=pl.ANY)],
            out_specs=out_specs,
            scratch_shapes=[pltpu.VMEM((2, 2, tm, ys.shape[1]), ys.dtype),
                            pltpu.SemaphoreType.DMA((2,))]),
        out_shape=out_shape,
        compiler_params=_cparams(("arbitrary",)),
        name="moe_combine_final" if final else "moe_combine",
    )(eid, rank, p_start, x, meta, gfin.reshape(1, d), ys)


def _moe(x1, h, logits, rbias, w1, w3, w2, layer, gfin, n_first_rows, final, tm_e=192, tm_t=128):
    n, d = h.shape
    meta, idx, cnt = _route(logits, rbias, 256)
    idx = idx.astype(jnp.int32)
    eid = idx[:, 0:2, :].transpose(1, 0, 2).reshape(-1)
    rank = idx[:, 2:4, :].transpose(1, 0, 2).reshape(-1)
    counts = cnt[0, _EXP_LANE0:_EXP_LANE0 + N_EXPERTS].astype(jnp.int32)
    padded = (counts + tm_e - 1) // tm_e * tm_e
    p_end = jnp.cumsum(padded)
    p_start = (p_end - padded).astype(jnp.int32)
    s = n * 2
    nblk = (s + N_EXPERTS * (tm_e - 1) + tm_e - 1) // tm_e
    bidx = jnp.arange(nblk, dtype=jnp.int32)
    block_e = jnp.minimum(jnp.sum((p_end[None, :] <= (bidx * tm_e)[:, None]).astype(jnp.int32), axis=1),
                          N_EXPERTS - 1).astype(jnp.int32)
    nb_used = (p_end[-1:] // tm_e).astype(jnp.int32)
    prev_e = jnp.concatenate([jnp.full((1,), -1, jnp.int32), block_e[:-1]])
    first = ((bidx < nb_used[0]) & (block_e != prev_e)).astype(jnp.int32)
    slot = ((jnp.cumsum(first) - 1) % 2).astype(jnp.int32)
    eidx = jnp.arange(N_EXPERTS, dtype=jnp.int32)
    live = jnp.where(counts > 0, eidx, N_EXPERTS)
    next_live = jnp.min(jnp.where(eidx[None, :] > eidx[:, None], live[None, :], N_EXPERTS), axis=1)
    nxt_e = jnp.sum(jnp.where(block_e[:, None] == eidx[None, :], next_live[None, :], 0), axis=1)
    nxt_e = jnp.where(nxt_e < N_EXPERTS, nxt_e, -1).astype(jnp.int32)
    xs = _dispatch(h, eid, rank, p_start, (p_start + counts).astype(jnp.int32),
                   (padded - counts).astype(jnp.int32), nb_used, nblk, tm_t, tm_e)
    ys = _experts(xs, block_e, nb_used, first, nxt_e, slot, w1, w3, w2, layer, tm_e)
    return _combine(x1, meta, eid, rank, p_start, ys, gfin, tm_t, n_first_rows, final)


def _lambda_value(lam_ref, lambda_init):
    lam = lam_ref[...]
    s1 = jnp.sum(lam[0:1] * lam[1:2], axis=1, keepdims=True)
    s2 = jnp.sum(lam[2:3] * lam[3:4], axis=1, keepdims=True)
    return jnp.exp(s1) - jnp.exp(s2) + lambda_init


def _lane_tile(x, width):
    if width % LANES == 0:
        return jnp.concatenate([x] * (width // LANES), axis=1)
    return x[:, :width]


def _softmax_step(j, s, v, m_ref, l_ref, acc_ref, rows=slice(None)):
    m_prev = m_ref[j, rows]
    m_new = jnp.maximum(m_prev, jnp.max(s, axis=1, keepdims=True))
    alpha = jnp.exp(m_prev - m_new)
    p = jnp.exp(s - _lane_tile(m_new, s.shape[1]))
    l_ref[j, rows] = alpha * l_ref[j, rows] + jnp.sum(p, axis=1, keepdims=True)
    acc = acc_ref[j, rows]
    acc_ref[j, rows] = acc * _lane_tile(alpha, acc.shape[1]) + _dot(p, v)
    m_ref[j, rows] = m_new


def _attn_finish(lam_ref, g_ref, l_ref, acc_ref, lambda_init):
    lam = _lambda_value(lam_ref, lambda_init)
    width = acc_ref.shape[-1]
    o = acc_ref[0] / _lane_tile(l_ref[0], width) - lam * (acc_ref[1] / _lane_tile(l_ref[1], width))
    return _rms(o, g_ref[...]) * (1.0 - lambda_init)


def _alibi_slopes():
    slopes = [2.0 ** (-8.0 * (h + 1) / DIFF_HEADS) for h in range(DIFF_HEADS)]
    assert all(math.frexp(s)[0] == 0.5 for s in slopes)
    return slopes


def _attn_prompt_body(iq_ref, ik_ref, slope_ref, q_ref, k_ref, v_ref, kt_ref, lam_ref, g_ref,
                      o_ref, qa_ref, m_ref, l_ref, acc_ref, *, tq, lambda_init):
    h = pl.program_id(1)
    t = pl.program_id(2)
    iq = iq_ref[t]
    ik = ik_ref[t]
    dh = DIFF_DH
    slope = slope_ref[h]

    @pl.when(ik == 0)
    def _():
        m_ref[...] = jnp.full(m_ref.shape, -jnp.inf, F32)
        l_ref[...] = jnp.zeros(l_ref.shape, F32)
        acc_ref[...] = jnp.zeros(acc_ref.shape, F32)
        qpos = iq * tq + lax.broadcasted_iota(jnp.int32, (tq, LANES), 0)
        lane = lax.broadcasted_iota(jnp.int32, (tq, LANES), 1)
        qa = (qpos // CHUNK).astype(F32)
        qb = (qpos % CHUNK).astype(F32)
        feat = jnp.where(lane == 0, qa * (-slope * CHUNK),
                         jnp.where(lane == 1, qb * (-slope), jnp.where(lane < 4, slope, 0.0)))
        q = q_ref[...]
        for j in range(2):
            qa_ref[j] = jnp.concatenate([q[:, j * dh:(j + 1) * dh] * (dh ** -0.5), feat], axis=1)

    k = k_ref[...]
    v = v_ref[...]
    kt = kt_ref[...]

    ts = tq // ATTN_ROW_SPLIT
    ka = [jnp.concatenate([k[:, j * dh:(j + 1) * dh], kt], axis=1) for j in range(2)]

    @pl.when(ik < iq)
    def _():
        for r in range(ATTN_ROW_SPLIT):
            rows = slice(r * ts, (r + 1) * ts)
            for j in range(2):
                _softmax_step(j, _dot_nt(qa_ref[j, rows], ka[j]), v, m_ref, l_ref, acc_ref, rows)

    @pl.when(ik == iq)
    def _():
        for r in range(ATTN_ROW_SPLIT):
            rows = slice(r * ts, (r + 1) * ts)
            nkeys = (r + 1) * ts
            row = r * ts + lax.broadcasted_iota(jnp.int32, (ts, nkeys), 0)
            col = lax.broadcasted_iota(jnp.int32, (ts, nkeys), 1)
            fix = jnp.maximum(col - row, 0).astype(F32) * (-2.0 * slope)
            vis = (col // CHUNK) <= (row // CHUNK)
            for j in range(2):
                s = _dot_nt(qa_ref[j, rows], ka[j][:nkeys])
                _softmax_step(j, jnp.where(vis, s + fix, -jnp.inf), v[:nkeys], m_ref, l_ref, acc_ref, rows)
        o_ref[...] = _attn_finish(lam_ref, g_ref, l_ref, acc_ref, lambda_init)


def _attn_prompt(q, k, v, nb, t, lam, g_subln, lambda_init, tq):
    hw = 2 * DIFF_DH
    assert t % tq == 0 and (tq // ATTN_ROW_SPLIT) % CHUNK == 0 and t // CHUNK <= 256
    nq = t // tq
    pairs = [(iq, ik) for iq in range(nq) for ik in range(iq + 1)]
    iq_tab = jnp.asarray(np.array([p[0] for p in pairs], np.int32))
    ik_tab = jnp.asarray(np.array([p[1] for p in pairs], np.int32))
    slopes = jnp.asarray(np.array(_alibi_slopes(), np.float32))
    pos = np.arange(t)
    ktab = np.zeros((t, LANES), np.float32)
    ktab[:, 0] = 1.0
    ktab[:, 1] = 1.0
    ktab[:, 2] = (pos // CHUNK) * CHUNK
    ktab[:, 3] = pos % CHUNK
    g3 = g_subln.reshape(DIFF_HEADS, 1, hw)
    return pl.pallas_call(
        functools.partial(_attn_prompt_body, tq=tq, lambda_init=lambda_init),
        grid_spec=pltpu.PrefetchScalarGridSpec(
            num_scalar_prefetch=3,
            grid=(nb, DIFF_HEADS, len(pairs)),
            in_specs=[pl.BlockSpec((tq, hw), lambda b, h, p, iqt, ikt, sl: (b * nq + iqt[p], h)),
                      pl.BlockSpec((tq, hw), lambda b, h, p, iqt, ikt, sl: (b * nq + ikt[p], h)),
                      pl.BlockSpec((tq, hw), lambda b, h, p, iqt, ikt, sl: (b * nq + ikt[p], h)),
                      pl.BlockSpec((tq, LANES), lambda b, h, p, iqt, ikt, sl: (ikt[p], 0)),
                      pl.BlockSpec((4, DIFF_DH), lambda b, h, p, iqt, ikt, sl: (0, 0)),
                      pl.BlockSpec((None, 1, hw), lambda b, h, p, iqt, ikt, sl: (h, 0, 0))],
            out_specs=pl.BlockSpec((tq, hw), lambda b, h, p, iqt, ikt, sl: (b * nq + iqt[p], h)),
            scratch_shapes=[pltpu.VMEM((2, tq, hw), F32),
                            pltpu.VMEM((2, tq, LANES), F32),
                            pltpu.VMEM((2, tq, LANES), F32),
                            pltpu.VMEM((2, tq, hw), F32)]),
        out_shape=jax.ShapeDtypeStruct((nb * t, DIFF_HEADS * hw), F32),
        compiler_params=_cparams(("arbitrary", "arbitrary", "arbitrary")),
        name="diff_attn_prompt",
    )(iq_tab, ik_tab, slopes, q, k, v, jnp.asarray(ktab), lam, g3)


def _attn_sample_body(q_ref, kc_ref, vc_ref, kn_ref, vn_ref, lam_ref, g_ref,
                      o_ref, m_ref, l_ref, acc_ref, *, tk, past, lambda_init):
    ik = pl.program_id(1)
    nkb = pl.num_programs(1) - 1
    dh = DIFF_DH
    hw = 2 * dh
    tq = q_ref.shape[0]
    scale = dh ** -0.5

    @pl.when(ik == 0)
    def _():
        m_ref[...] = jnp.full(m_ref.shape, -jnp.inf, F32)
        l_ref[...] = jnp.zeros(l_ref.shape, F32)
        acc_ref[...] = jnp.zeros(acc_ref.shape, F32)

    def attend(h, k, v, k0, width):
        slope = _alibi_slopes()[h]
        q = q_ref[:, h * hw:(h + 1) * hw]
        qpos = past + lax.broadcasted_iota(jnp.int32, (tq, width), 0)
        kpos = k0 + lax.broadcasted_iota(jnp.int32, (tq, width), 1)
        bias = jnp.abs(qpos - kpos).astype(F32) * (-slope)
        vis = (kpos // CHUNK) <= (qpos // CHUNK)
        for j in range(2):
            s = _dot_nt(q[:, j * dh:(j + 1) * dh], k[:, j * dh:(j + 1) * dh]) * scale + bias
            s = jnp.where(vis, s, -jnp.inf)
            _softmax_step(j, s, v, m_ref.at[h], l_ref.at[h], acc_ref.at[h])

    @pl.when(ik < nkb)
    def _():
        for h in range(DIFF_HEADS):
            attend(h, kc_ref[:, h, :], vc_ref[:, h, :], ik * tk, tk)

    @pl.when(ik == nkb)
    def _():
        for h in range(DIFF_HEADS):
            hs = slice(h * hw, (h + 1) * hw)
            attend(h, kn_ref[:, hs], vn_ref[:, hs], past, tq)
            o_ref[:, hs] = _attn_finish(lam_ref, g_ref.at[h], l_ref.at[h], acc_ref.at[h], lambda_init)


def _attn_sample(q, k, v, cache_k, cache_v, li, lam, g_subln, lambda_init, tk):
    n_rows = q.shape[0]
    _, nb, past, nh, hw = cache_k.shape
    aw = nh * hw
    tq = n_rows // nb
    tk = min(tk, past)
    assert past % tk == 0
    nkb = past // tk
    g3 = g_subln.reshape(nh, 1, hw)
    cache_spec = pl.BlockSpec((None, None, tk, nh, hw),
                              lambda b, ik: (li, b, jnp.minimum(ik, nkb - 1), 0, 0))
    row_spec = pl.BlockSpec((tq, aw), lambda b, ik: (b, 0))
    return pl.pallas_call(
        functools.partial(_attn_sample_body, tk=tk, past=past, lambda_init=lambda_init),
        grid=(nb, nkb + 1),
        in_specs=[row_spec,
                  cache_spec, cache_spec,
                  row_spec,
                  row_spec,
                  pl.BlockSpec((4, DIFF_DH), lambda b, ik: (0, 0)),
                  pl.BlockSpec((nh, 1, hw), lambda b, ik: (0, 0, 0))],
        out_specs=pl.BlockSpec((tq, aw), lambda b, ik: (b, 0)),
        scratch_shapes=[pltpu.VMEM((nh, 2, tq, LANES), F32),
                        pltpu.VMEM((nh, 2, tq, LANES), F32),
                        pltpu.VMEM((nh, 2, tq, hw), F32)],
        out_shape=jax.ShapeDtypeStruct((n_rows, aw), F32),
        compiler_params=_cparams(("arbitrary", "arbitrary")),
        name="diff_attn_sample",
    )(q, cache_k, cache_v, k, v, lam, g3)


def _pad_cols(a, width):
    return jnp.pad(a, ((0, 0), (0, width - a.shape[1])))


def _mixer_constants(dk, dv, w_gate_up, b_gate, g_gla, conv_w, conv_b, dt_bias, a_log, d_skip, g_ssd):
    C = REC_ROWS
    nh = GLA_HEADS
    inner = g_ssd.shape[0]
    n_ssd = inner // SSD_HEADDIM
    wup = jnp.pad(w_gate_up, ((0, LANES - w_gate_up.shape[0]), (0, 0)))
    tri = jnp.asarray(np.tril(np.ones((C, C), np.float32)), BF16)
    e64 = np.zeros((LANES, inner), np.float32)
    ec = np.zeros((LANES, n_ssd * C), np.float32)
    for hh in range(n_ssd):
        e64[hh, hh * SSD_HEADDIM:(hh + 1) * SSD_HEADDIM] = 1.0
        ec[hh, hh * C:(hh + 1) * C] = 1.0
    eye = np.tile(np.eye(C, dtype=np.float32), (1, n_ssd))
    caus = np.tile(np.tril(np.ones((C, C), np.float32)), (1, n_ssd))
    return [wup, b_gate.reshape(1, -1), g_gla.reshape(nh, dv), conv_w, conv_b.reshape(1, -1),
            _pad_cols(dt_bias.reshape(1, -1), LANES),
            _pad_cols(-jnp.exp(a_log.astype(F32)).reshape(1, -1), LANES),
            jnp.repeat(d_skip, SSD_HEADDIM).reshape(1, -1), g_ssd.reshape(1, -1),
            tri, jnp.asarray(e64, BF16), jnp.asarray(ec, BF16), jnp.asarray(eye), jnp.asarray(caus)]


def kernel(x_prompt, x_sample, state_gla, state_ssm, state_conv, cache_k, cache_v, norm_mix, norm_ffn, norm_final, w_in, w_gate_up, b_gate, g_gla, conv_w, conv_b, dt_bias, a_log, d_skip, g_ssd, w_out_mix, w_qkv, lam_q1, lam_k1, lam_q2, lam_k2, g_subln, w_o, router_group_w, router_group_b, router_expert_w, router_expert_b, w1, w3, w2):
    bp, tp, d = x_prompt.shape
    bs, ts, _ = x_sample.shape
    n_p, n_s = bp * tp, bs * ts
    n = n_p + n_s
    depth = norm_mix.shape[0]
    nh = GLA_HEADS
    dk, dv = state_gla.shape[-2], state_gla.shape[-1]
    qk_w, vw = nh * dk, nh * dv
    n_ssd = state_ssm.shape[2]
    inner = n_ssd * SSD_HEADDIM
    hpg = n_ssd // SSD_GROUPS
    gs = SSD_GROUPS * SSD_STATE
    cw = inner + 2 * gs
    rank = w_gate_up.shape[1]
    assert w_gate_up.shape[2] == qk_w and rank <= LANES and n_ssd <= LANES

    x = jnp.concatenate([x_prompt.reshape(n_p, d), x_sample.reshape(n_s, d)], axis=0)
    tm_row = next(tm for tm in (1408, 1056, 768, 256) if n % tm == 0)

    i_even = i_odd = 0
    gla_p, ssm_p, conv_p, gla_s, ssm_s, conv_s = [], [], [], [], [], []
    k_p, v_p, k_s, v_s = [], [], [], []
    y_p = y_s = None
    for layer in range(depth):
        final = layer == depth - 1
        if layer % 2 == 0:
            i = i_even
            i_even += 1
            offs = np.cumsum([0, qk_w, qk_w, vw, rank, vw, inner, cw, n_ssd])
            w_t = w_in[i].T
            seg = lambda j: w_t[offs[j]:offs[j + 1]]
            pad_rows = lambda a: jnp.pad(a, ((0, LANES - a.shape[0]), (0, 0)))
            w_cat = jnp.concatenate([seg(0), seg(1), seg(2), seg(4), seg(5), seg(6),
                                     pad_rows(seg(3)), pad_rows(seg(7))], axis=0).astype(BF16)
            proj = _norm_matmul(x, norm_mix[layer], w_cat, tm_row, 1280, w_is_transposed=True)
            consts = _mixer_constants(dk, dv, w_gate_up[i], b_gate[i], g_gla[i], conv_w[i], conv_b[i],
                                      dt_bias[i], a_log[i], d_skip[i], g_ssd[i])

            def to_group_state(sm):
                b_ = sm.shape[0]
                return sm.reshape(b_, SSD_GROUPS, hpg, SSD_STATE, SSD_HEADDIM).transpose(0, 1, 3, 2, 4) \
                         .reshape(b_, SSD_GROUPS, SSD_STATE, hpg * SSD_HEADDIM)

            def from_group_state(sg):
                b_ = sg.shape[0]
                return sg.reshape(b_, SSD_GROUPS, SSD_STATE, hpg, SSD_HEADDIM).transpose(0, 1, 3, 2, 4) \
                         .reshape(b_, n_ssd, SSD_STATE, SSD_HEADDIM)

            def pad_conv(cv):
                return jnp.pad(cv, ((0, 0), (CONV_PAD - cv.shape[1], 0), (0, 0)))

            mix_a, g_fin, s_fin, c_fin = _mixer_call(
                proj, 0, bp, tp, jnp.zeros((bp, nh, dv, dk), F32),
                jnp.zeros((bp, SSD_GROUPS, SSD_STATE, hpg * SSD_HEADDIM), F32),
                jnp.zeros((bp, CONV_PAD, cw), F32), consts, dk, dv)
            gla_p.append(g_fin.transpose(0, 1, 3, 2))
            ssm_p.append(from_group_state(s_fin))
            conv_p.append(c_fin[:, CONV_PAD - (SSD_CONV - 1):])
            mix_b, g_fin, s_fin, c_fin = _mixer_call(
                proj, n_p, bs, ts, state_gla[i].transpose(0, 1, 3, 2),
                to_group_state(state_ssm[i]), pad_conv(state_conv[i]), consts, dk, dv)
            gla_s.append(g_fin.transpose(0, 1, 3, 2))
            ssm_s.append(from_group_state(s_fin))
            conv_s.append(c_fin[:, CONV_PAD - (SSD_CONV - 1):])
            w_proj = w_out_mix[i]
        else:
            i = i_odd
            i_odd += 1
            lambda_init = 0.8 - 0.6 * math.exp(-0.3 * layer)
            tm_p = next(tm for tm in (1024, 512, 256) if n_p % tm == 0)
            w_qkv_b = w_qkv[i].astype(BF16)
            q_a, k_a, v_a = _norm_matmul(x, norm_mix[layer], w_qkv_b, tm_p, 1024, nrows=n_p, parts=3)
            q_b, k_b, v_b = _norm_matmul(x, norm_mix[layer], w_qkv_b, n_s, 1024, row0=n_p, parts=3)
            lam = jnp.stack([lam_q1[i], lam_k1[i], lam_q2[i], lam_k2[i]])
            mix_a = _attn_prompt(q_a, k_a, v_a, bp, tp, lam, g_subln[i], lambda_init, min(512, tp))
            mix_b = _attn_sample(q_b, k_b, v_b, cache_k, cache_v, i, lam, g_subln[i], lambda_init, 1024)
            k_p.append(k_a.reshape(bp, tp, DIFF_HEADS, 2 * DIFF_DH))
            v_p.append(v_a.reshape(bp, tp, DIFF_HEADS, 2 * DIFF_DH))
            k_s.append(k_b.reshape(bs, ts, DIFF_HEADS, 2 * DIFF_DH))
            v_s.append(v_b.reshape(bs, ts, DIFF_HEADS, 2 * DIFF_DH))
            w_proj = w_o[i]

        wr = _pad_cols(jnp.concatenate([router_group_w[layer], router_expert_w[layer]], axis=1), LANES)
        rbias = _pad_cols(jnp.concatenate([router_group_b[layer], router_expert_b[layer]]).reshape(1, -1), LANES)
        x1, h2, logits = _proj_resid(mix_a, mix_b, w_proj, x, norm_ffn[layer], wr, 256)
        res = _moe(x1, h2, logits, rbias, w1, w3, w2, layer, norm_final, n_p, final)
        if final:
            y_p, y_s = res
        else:
            x = res[0]

    return (y_p.reshape(bp, tp, d), y_s.reshape(bs, ts, d),
            jnp.stack(gla_p), jnp.stack(ssm_p), jnp.stack(conv_p), jnp.stack(k_p), jnp.stack(v_p),
            jnp.stack(gla_s), jnp.stack(ssm_s), jnp.stack(conv_s), jnp.stack(k_s), jnp.stack(v_s))
```

```python
import functools
import math

import numpy as np
import jax
import jax.numpy as jnp
from jax import lax
from jax.experimental import pallas as pl
from jax.experimental.pallas import tpu as pltpu

F32 = jnp.float32
BF16 = jnp.bfloat16

EPS = 1e-6
CHUNK = 64
GLA_HEADS = 4
GLA_TAU = 16.0
SSD_HEADDIM = 64
SSD_STATE = 128
SSD_GROUPS = 4
SSD_CONV = 4
DIFF_HEADS = 8
DIFF_DH = 128
MOE_GROUPS = 4
MOE_PER_GROUP = 8
N_EXPERTS = MOE_GROUPS * MOE_PER_GROUP

LANES = 128
SUBLANES = 8
REC_ROWS = 128
STRIP = 16
CONV_PAD = 8
ATTN_ROW_SPLIT = 2
ATTN_HEADS_PER_STEP = 2
VMEM_LIMIT = 56 * 1024 * 1024


def _cparams(sem, vmem=VMEM_LIMIT):
    return pltpu.CompilerParams(dimension_semantics=sem, vmem_limit_bytes=vmem)


def _dot(a, b):
    return jnp.dot(a, b, preferred_element_type=F32)


def _dot_nt(a, b):
    return lax.dot_general(a, b, (((1,), (1,)), ((), ())), preferred_element_type=F32)


def _dot_tn(a, b):
    return lax.dot_general(a, b, (((0,), (0,)), ((), ())), preferred_element_type=F32)


def _split_hi_lo(a):
    hi = a.astype(BF16)
    lo = (a - hi.astype(F32)).astype(BF16)
    return hi, lo


def _exact_left(m, a):
    hi, lo = _split_hi_lo(a)
    return _dot(m, hi) + _dot(m, lo)


def _exact_right(a, m):
    hi, lo = _split_hi_lo(a)
    return _dot(hi, m) + _dot(lo, m)


def _silu(x):
    h = 0.5 * x
    return h * (1.0 + jnp.tanh(h))


def _log1p_exp_neg_abs(x):
    e = jnp.exp(-jnp.abs(x))
    u = 1.0 + e
    return jnp.where(u == 1.0, e, jnp.log(u) * (e / (u - 1.0)))


def _rms(x, g):
    return x * lax.rsqrt(jnp.mean(x * x, axis=-1, keepdims=True) + EPS) * g


def _pack_halves(x):
    w = x.shape[1] // 2
    lo = lax.bitcast_convert_type(x[:, :w].astype(BF16).astype(F32), jnp.uint32)
    hi = lax.bitcast_convert_type(x[:, w:].astype(BF16).astype(F32), jnp.uint32)
    return (lo >> 16) | (hi & jnp.uint32(0xFFFF0000))


def _unpack_halves(p):
    lo = lax.bitcast_convert_type(p << 16, F32)
    hi = lax.bitcast_convert_type(p & jnp.uint32(0xFFFF0000), F32)
    return jnp.concatenate([lo, hi], axis=1)


def _norm_matmul_body(x_ref, g_ref, w_ref, *refs, w_is_transposed, parts):
    o_refs, h_ref = refs[:parts], refs[parts]
    j = pl.program_id(1)

    @pl.when(j == 0)
    def _():
        h_ref[...] = _rms(x_ref[...], g_ref[...]).astype(h_ref.dtype)

    res = (_dot_nt if w_is_transposed else _dot)(h_ref[...], w_ref[...])
    if parts == 1:
        o_refs[0][...] = res
    else:
        per = pl.num_programs(1) // parts
        for p in range(parts):
            @pl.when(j // per == p)
            def _(p=p):
                o_refs[p][...] = res


def _norm_matmul(x, g, w, tm, tn, *, row0=0, nrows=None, parts=1, w_is_transposed=False):
    n, d = x.shape
    nrows = n - row0 if nrows is None else nrows
    nout = w.shape[0] if w_is_transposed else w.shape[1]
    assert nrows % tm == 0 and row0 % tm == 0 and nout % (tn * parts) == 0
    blk0 = row0 // tm
    per = nout // tn // parts
    w_spec = (pl.BlockSpec((tn, d), lambda i, j: (j, 0)) if w_is_transposed
              else pl.BlockSpec((d, tn), lambda i, j: (0, j)))
    out_specs = [pl.BlockSpec((tm, tn), lambda i, j, p=p: (i, jnp.clip(j - p * per, 0, per - 1)))
                 for p in range(parts)]
    outs = pl.pallas_call(
        functools.partial(_norm_matmul_body, w_is_transposed=w_is_transposed, parts=parts),
        grid=(nrows // tm, nout // tn),
        in_specs=[pl.BlockSpec((tm, d), lambda i, j: (blk0 + i, 0), pipeline_mode=pl.Buffered(1)),
                  pl.BlockSpec((1, d), lambda i, j: (0, 0)),
                  w_spec],
        out_specs=out_specs,
        out_shape=[jax.ShapeDtypeStruct((nrows, nout // parts), F32)] * parts,
        scratch_shapes=[pltpu.VMEM((tm, d), w.dtype)],
        compiler_params=_cparams(("arbitrary", "arbitrary")),
        name="norm_matmul",
    )(x, g.reshape(1, d), w)
    return outs[0] if parts == 1 else outs


def _mixer_body(p_ref, gla0_ref, ssm0_ref, conv0_ref, wup_ref, bgate_ref, ggla_ref, convw_ref,
                convb_ref, dtb_ref, aneg_ref, dskip_ref, gssd_ref, tri_ref, e64_ref, ec_ref,
                eye_ref, caus_ref,
                o_ref, gla_ref, ssm_ref, ctail_ref,
                sg_ref, ss_ref, ext_ref, b_ref, *, rb, dk, dv):
    C = REC_ROWS
    c = pl.program_id(1)
    nc = pl.num_programs(1)
    nh = GLA_HEADS
    qk_w = nh * dk
    vw = nh * dv
    inner = vw
    gs = SSD_GROUPS * SSD_STATE
    o_q, o_k, o_v = 0, qk_w, 2 * qk_w
    o_r = o_v + vw
    o_z = o_r + vw
    o_x = o_z + inner
    o_g = o_x + inner + 2 * gs
    o_dt = o_g + LANES

    @pl.when(c == 0)
    def _init():
        sg_ref[...] = gla0_ref[...]
        ss_ref[...] = ssm0_ref[...]
        ext_ref[0:CONV_PAD, :] = conv0_ref[...]

    p = p_ref[...]
    if rb < C:
        p = jnp.concatenate([p, jnp.zeros((C - rb, p.shape[1]), F32)], axis=0)

    def rowmask(width):
        return lax.broadcasted_iota(jnp.int32, (C, width), 0) < rb

    q = p[:, o_q:o_q + qk_w] * (dk ** -0.5)
    k = p[:, o_k:o_k + qk_w]
    v = p[:, o_v:o_v + vw]
    r = p[:, o_r:o_r + vw]
    z = p[:, o_z:o_z + inner]
    xbc = p[:, o_x:o_x + inner + 2 * gs]
    glr = p[:, o_g:o_g + LANES]
    dtp = p[:, o_dt:o_dt + LANES]

    tri = tri_ref[...]

    zg = _dot(glr, wup_ref[...]) + bgate_ref[...]
    la = (jnp.minimum(zg, 0.0) - _log1p_exp_neg_abs(zg)) * (1.0 / GLA_TAU)
    if rb < C:
        la = jnp.where(rowmask(qk_w), la, 0.0)
    bcum = _exact_left(tri, la)
    b_ref[...] = bcum

    row_i = lax.broadcasted_iota(jnp.int32, (C, dk), 0)
    prow = lax.broadcasted_iota(jnp.int32, (STRIP, C), 0)
    pcol = lax.broadcasted_iota(jnp.int32, (STRIP, C), 1)
    o_heads = []
    for h in range(nh):
        hs = slice(h * dk, (h + 1) * dk)
        vs = slice(h * dv, (h + 1) * dv)
        bh = bcum[:, hs]
        qh = q[:, hs]
        kh = k[:, hs]
        vh = v[:, vs]
        strips = []
        for i in range(C // STRIP):
            r0 = i * STRIP
            if i == 0:
                ref_row = jnp.zeros((1, dk), F32)
            else:
                ref_row = b_ref[pl.ds(r0 - 1, 1), hs]
            q_i = qh[r0:r0 + STRIP] * jnp.exp(bh[r0:r0 + STRIP] - ref_row)
            e = jnp.where(row_i < r0 + STRIP, ref_row - bh, 0.0)
            k_i = kh * jnp.exp(e)
            s_i = _dot_nt(q_i, k_i)
            strips.append(jnp.where(pcol <= prow + r0, s_i, 0.0))
        pmat = jnp.concatenate(strips, axis=0)
        b_last = b_ref[pl.ds(C - 1, 1), hs]
        st = sg_ref[h]
        o_h = _dot(pmat, vh) + _dot_nt(qh * jnp.exp(bh), st)
        k_st = kh * jnp.exp(b_last - bh)
        sg_ref[h] = st * jnp.exp(b_last) + _dot_tn(vh, k_st)
        o_h = _rms(o_h, ggla_ref[pl.ds(h, 1), :]) * _silu(r[:, vs])
        o_heads.append(o_h)
    o_a = jnp.concatenate(o_heads, axis=1)

    cw = inner + 2 * gs
    ext_ref[CONV_PAD:CONV_PAD + C, :] = xbc
    conv = convb_ref[...]
    for j in range(SSD_CONV):
        conv = conv + convw_ref[pl.ds(j, 1), :] * ext_ref[pl.ds(CONV_PAD - (SSD_CONV - 1) + j, C), :]
    xc = _silu(conv)
    xs = xc[:, :inner]
    bm = xc[:, inner:inner + gs]
    cm = xc[:, inner + gs:cw]
    dtv = dtp + dtb_ref[...]
    dt = jnp.maximum(dtv, 0.0) + _log1p_exp_neg_abs(dtv)
    ld = dt * aneg_ref[...]
    if rb < C:
        dt = jnp.where(rowmask(LANES), dt, 0.0)
        ld = jnp.where(rowmask(LANES), ld, 0.0)
    bs = _exact_left(tri, ld)
    e64 = e64_ref[...]
    dt64 = _exact_right(dt, e64)
    b64 = _exact_right(bs, e64)
    blast64 = b64[C - 1:C, :]
    xdt = xs * dt64
    bc = _exact_right(bs, ec_ref[...])
    br = jnp.sum(eye_ref[...] * bc, axis=0, keepdims=True)
    vis = caus_ref[...] > 0.0
    lf = jnp.where(vis, jnp.exp(jnp.where(vis, bc - br, 0.0)), 0.0)
    hpg = inner // SSD_HEADDIM // SSD_GROUPS
    gw = hpg * SSD_HEADDIM
    lane_head = lax.broadcasted_iota(jnp.int32, (C, gw), 1) // SSD_HEADDIM
    y_groups = []
    for g in range(SSD_GROUPS):
        cm_g = cm[:, g * SSD_STATE:(g + 1) * SSD_STATE]
        bm_g = bm[:, g * SSD_STATE:(g + 1) * SSD_STATE]
        gl = slice(g * gw, (g + 1) * gw)
        gmat = _dot_nt(cm_g, bm_g)
        a4 = jnp.concatenate([gmat] * hpg, axis=1) * lf[:, g * hpg * C:(g + 1) * hpg * C]
        xg = xdt[:, gl]
        x4 = jnp.concatenate([jnp.where(lane_head == hh, xg, 0.0) for hh in range(hpg)], axis=0)
        s_g = ss_ref[g]
        y_g = _dot(a4, x4) + _dot(cm_g, s_g) * jnp.exp(b64[:, gl])
        xw = xg * jnp.exp(blast64[:, gl] - b64[:, gl])
        ss_ref[g] = s_g * jnp.exp(blast64[:, gl]) + _dot_tn(bm_g, xw)
        y_groups.append(y_g)
    y = jnp.concatenate(y_groups, axis=1) + xs * dskip_ref[...]
    y = _rms(y * _silu(z), gssd_ref[...])

    o_full = jnp.concatenate([o_a, y], axis=1)
    o_ref[...] = o_full[:rb]

    @pl.when(c == nc - 1)
    def _fin():
        gla_ref[...] = sg_ref[...]
        ssm_ref[...] = ss_ref[...]
        ctail_ref[...] = ext_ref[pl.ds(rb, CONV_PAD), :]

    ext_ref[0:CONV_PAD, :] = ext_ref[pl.ds(rb, CONV_PAD), :]


def _mixer_call(proj, row_off, nb, t, gla0, ssm0, conv0, consts, dk, dv):
    C = REC_ROWS
    rb = min(t, C)
    assert t % rb == 0 and row_off % rb == 0
    steps = t // rb
    width = proj.shape[1]
    nh = GLA_HEADS
    vw = nh * dv
    cw = conv0.shape[-1]
    blk0 = row_off // rb

    def full(a):
        nd = a.ndim
        return pl.BlockSpec(a.shape, lambda b, c, _n=nd: (0,) * _n)

    in_specs = [pl.BlockSpec((rb, width), lambda b, c: (blk0 + b * steps + c, 0)),
                pl.BlockSpec((None, nh, dv, dk), lambda b, c: (b, 0, 0, 0)),
                pl.BlockSpec((None, SSD_GROUPS, SSD_STATE, ssm0.shape[-1]), lambda b, c: (b, 0, 0, 0)),
                pl.BlockSpec((None, CONV_PAD, cw), lambda b, c: (b, 0, 0))]
    in_specs += [full(a) for a in consts]
    operands = [proj, gla0, ssm0, conv0, *consts]
    out_specs = [pl.BlockSpec((rb, 2 * vw), lambda b, c: (b * steps + c, 0)),
                 pl.BlockSpec((None, nh, dv, dk), lambda b, c: (b, 0, 0, 0)),
                 pl.BlockSpec((None, SSD_GROUPS, SSD_STATE, ssm0.shape[-1]), lambda b, c: (b, 0, 0, 0)),
                 pl.BlockSpec((None, CONV_PAD, cw), lambda b, c: (b, 0, 0))]
    out_shape = [jax.ShapeDtypeStruct((nb * t, 2 * vw), F32),
                 jax.ShapeDtypeStruct(gla0.shape, F32),
                 jax.ShapeDtypeStruct(ssm0.shape, F32),
                 jax.ShapeDtypeStruct((nb, CONV_PAD, cw), F32)]
    return pl.pallas_call(
        functools.partial(_mixer_body, rb=rb, dk=dk, dv=dv),
        grid=(nb, steps),
        in_specs=in_specs,
        out_specs=out_specs,
        out_shape=out_shape,
        scratch_shapes=[pltpu.VMEM((nh, dv, dk), F32),
                        pltpu.VMEM((SSD_GROUPS, SSD_STATE, ssm0.shape[-1]), F32),
                        pltpu.VMEM((C + CONV_PAD, cw), F32),
                        pltpu.VMEM((C, nh * dk), F32)],
        compiler_params=_cparams(("arbitrary", "arbitrary")),
        name="gla_ssd_mixer",
    )(*operands)


def _proj_resid_body(oa_ref, ob_ref, w_ref, x_ref, g_ref, wr_ref, x1_ref, h_ref, lg_ref, *, n_first):
    o = jnp.where(pl.program_id(0) < n_first, oa_ref[...], ob_ref[...])
    x1 = x_ref[...] + _dot(o, w_ref[...])
    x1_ref[...] = x1
    h = _rms(x1, g_ref[...])
    h_ref[...] = _pack_halves(h)
    lg_ref[...] = _dot(h, wr_ref[...])


def _packed_row(d):
    return jax.eval_shape(_pack_halves, jax.ShapeDtypeStruct((SUBLANES, d), F32))


def _proj_resid(o_a, o_b, w, x, g, wr, tm):
    n, d = x.shape
    kin = o_a.shape[1]
    assert o_a.shape[0] % tm == 0 and o_b.shape[0] % tm == 0
    n_first = o_a.shape[0] // tm
    hp = _packed_row(d)
    return pl.pallas_call(
        functools.partial(_proj_resid_body, n_first=n_first),
        grid=(n // tm,),
        in_specs=[pl.BlockSpec((tm, kin), lambda i: (jnp.minimum(i, n_first - 1), 0)),
                  pl.BlockSpec((tm, kin), lambda i: (jnp.maximum(i - n_first, 0), 0)),
                  pl.BlockSpec((kin, d), lambda i: (0, 0), pipeline_mode=pl.Buffered(1)),
                  pl.BlockSpec((tm, d), lambda i: (i, 0)),
                  pl.BlockSpec((1, d), lambda i: (0, 0)),
                  pl.BlockSpec((d, LANES), lambda i: (0, 0))],
        out_specs=[pl.BlockSpec((tm, d), lambda i: (i, 0)),
                   pl.BlockSpec((tm, hp.shape[1]), lambda i: (i, 0)),
                   pl.BlockSpec((tm, LANES), lambda i: (i, 0))],
        out_shape=[jax.ShapeDtypeStruct((n, d), F32),
                   jax.ShapeDtypeStruct((n, hp.shape[1]), hp.dtype),
                   jax.ShapeDtypeStruct((n, LANES), F32)],
        compiler_params=_cparams(("arbitrary",)),
        name="proj_resid_norm_router",
    )(o_a, o_b, w, x, g.reshape(1, d), wr)


_META_E0, _META_E1, _META_G0, _META_G1, _META_R0, _META_R1 = range(6)
_EXP_LANE0 = MOE_GROUPS


def _route_body(lg_ref, bias_ref, ltri_ref, eye_ref, meta_ref, idx_ref, cnt_ref, base_ref):
    i = pl.program_id(0)

    @pl.when(i == 0)
    def _():
        base_ref[...] = jnp.zeros_like(base_ref)

    lg = lg_ref[...] + bias_ref[...]
    tm = lg.shape[0]
    lane_i = lax.broadcasted_iota(jnp.int32, (tm, LANES), 1)
    lane = lane_i.astype(F32)
    neg = -jnp.inf
    glog = jnp.where(lane_i < MOE_GROUPS, lg, neg)
    gmax = jnp.max(glog, axis=1, keepdims=True)
    gsel = jnp.min(jnp.where(glog == gmax, lane, float(LANES)), axis=1, keepdims=True)
    pg = 1.0 / jnp.sum(jnp.exp(glog - gmax), axis=1, keepdims=True)
    lo = _EXP_LANE0 + MOE_PER_GROUP * gsel
    el = jnp.where((lane >= lo) & (lane < lo + MOE_PER_GROUP), lg, neg)
    v1 = jnp.max(el, axis=1, keepdims=True)
    i1 = jnp.min(jnp.where(el == v1, lane, float(LANES)), axis=1, keepdims=True)
    el2 = jnp.where(lane == i1, neg, el)
    v2 = jnp.max(el2, axis=1, keepdims=True)
    i2 = jnp.min(jnp.where(el2 == v2, lane, float(LANES)), axis=1, keepdims=True)
    e = jnp.exp(v2 - v1)
    g1 = pg / (1.0 + e)
    g2 = pg * e / (1.0 + e)
    hot1 = lane == i1
    hot2 = lane == i2
    onehot = jnp.where(hot1 | hot2, 1.0, 0.0)
    before = _dot(ltri_ref[...], onehot.astype(BF16)) + base_ref[0:1, :]
    r1 = jnp.sum(jnp.where(hot1, before, 0.0), axis=1, keepdims=True)
    r2 = jnp.sum(jnp.where(hot2, before, 0.0), axis=1, keepdims=True)
    base_ref[0:1, :] = base_ref[0:1, :] + jnp.sum(onehot, axis=0, keepdims=True)
    meta = jnp.zeros((tm, LANES), F32)
    for idx, val in ((_META_E0, i1 - _EXP_LANE0), (_META_E1, i2 - _EXP_LANE0),
                     (_META_G0, g1), (_META_G1, g2), (_META_R0, r1), (_META_R1, r2)):
        meta = jnp.where(lane_i == idx, val, meta)
    meta_ref[...] = meta

    eye = eye_ref[...]
    rows = [jnp.sum(eye * col, axis=0, keepdims=True)
            for col in (i1 - _EXP_LANE0, i2 - _EXP_LANE0, r1, r2)]
    idx_ref[...] = jnp.concatenate(rows + [jnp.zeros((SUBLANES - len(rows), tm), F32)], axis=0)

    @pl.when(i == pl.num_programs(0) - 1)
    def _():
        cnt_ref[...] = base_ref[...]


def _route(logits, bias, tm):
    n = logits.shape[0]
    ltri = jnp.asarray(np.tril(np.ones((tm, tm), np.float32), -1), BF16)
    eye = jnp.asarray(np.eye(tm, dtype=np.float32))
    return pl.pallas_call(
        _route_body,
        grid=(n // tm,),
        in_specs=[pl.BlockSpec((tm, LANES), lambda i: (i, 0)),
                  pl.BlockSpec((1, LANES), lambda i: (0, 0)),
                  pl.BlockSpec((tm, tm), lambda i: (0, 0)),
                  pl.BlockSpec((tm, tm), lambda i: (0, 0))],
        out_specs=[pl.BlockSpec((tm, LANES), lambda i: (i, 0)),
                   pl.BlockSpec((None, SUBLANES, tm), lambda i: (i, 0, 0)),
                   pl.BlockSpec((8, LANES), lambda i: (0, 0))],
        out_shape=[jax.ShapeDtypeStruct((n, LANES), F32),
                   jax.ShapeDtypeStruct((n // tm, SUBLANES, tm), F32),
                   jax.ShapeDtypeStruct((8, LANES), F32)],
        scratch_shapes=[pltpu.VMEM((8, LANES), F32)],
        compiler_params=_cparams(("arbitrary",)),
        name="moe_route",
    )(logits, bias, ltri, eye)


def _slot_of(eid_ref, rank_ref, pstart_ref, idx):
    return pstart_ref[eid_ref[idx]] + rank_ref[idx]


def _dispatch_body(eid_ref, rank_ref, pstart_ref, pad0_ref, npad_ref, nb_ref, h_ref, xs_ref, zbuf, sem, zsem,
                   *, tm, tm_e, nblk):
    i = pl.program_id(0)
    n_tok = pl.num_programs(0) * tm

    def pad_fill(wait):
        def go(cp):
            if wait:
                cp.wait()
            else:
                cp.start()

        def body(e, carry):
            off = pad0_ref[e]
            npad = npad_ref[e]
            head = (SUBLANES - off % SUBLANES) % SUBLANES
            for j in range(SUBLANES - 1):
                @pl.when(j < head)
                def _(j=j):
                    go(pltpu.make_async_copy(zbuf.at[pl.ds(0, 1), :], xs_ref.at[pl.ds(off + j, 1), :], zsem))
            off = pl.multiple_of(off + head, SUBLANES)
            rem = npad - head
            bit = pl.next_power_of_2(tm_e) // 2
            while bit >= SUBLANES:
                on = (rem & bit) != 0

                @pl.when(on)
                def _(off=off, bit=bit):
                    go(pltpu.make_async_copy(zbuf.at[pl.ds(0, bit), :], xs_ref.at[pl.ds(off, bit), :], zsem))

                off = pl.multiple_of(off + jnp.where(on, bit, 0), SUBLANES)
                bit //= 2
            return carry
        lax.fori_loop(0, N_EXPERTS, body, 0)

        def tail(tb, carry):
            go(pltpu.make_async_copy(zbuf, xs_ref.at[pl.ds(pl.multiple_of(tb * tm_e, tm_e), tm_e), :], zsem))
            return carry
        lax.fori_loop(nb_ref[0], nblk, tail, 0)

    @pl.when(i == 0)
    def _():
        zbuf[...] = jnp.zeros(zbuf.shape, zbuf.dtype)
        pad_fill(False)

    for r in range(tm):
        for kk in range(2):
            d = _slot_of(eid_ref, rank_ref, pstart_ref, kk * n_tok + i * tm + r)
            pltpu.make_async_copy(h_ref.at[pl.ds(r, 1), :], xs_ref.at[pl.ds(d, 1), :], sem).start(
                priority=(r + kk) % 2)

    @pl.when(i == 0)
    def _():
        pad_fill(True)

    for _ in range(2 * tm):
        pltpu.make_async_copy(h_ref.at[pl.ds(0, 1), :], xs_ref.at[pl.ds(0, 1), :], sem).wait()


def _dispatch(h, eid, rank, p_start, pad0, npad, nb_used, nblk, tm, tm_e):
    n, d = h.shape
    return pl.pallas_call(
        functools.partial(_dispatch_body, tm=tm, tm_e=tm_e, nblk=nblk),
        grid_spec=pltpu.PrefetchScalarGridSpec(
            num_scalar_prefetch=6,
            grid=(n // tm,),
            in_specs=[pl.BlockSpec((tm, d), lambda i, *_: (i, 0))],
            out_specs=pl.BlockSpec(memory_space=pl.ANY),
            scratch_shapes=[pltpu.VMEM((tm_e, d), h.dtype),
                            pltpu.SemaphoreType.DMA(()),
                            pltpu.SemaphoreType.DMA(())]),
        out_shape=jax.ShapeDtypeStruct((nblk * tm_e, d), h.dtype),
        compiler_params=_cparams(("arbitrary",)),
        name="moe_dispatch",
    )(eid, rank, p_start, pad0, npad, nb_used, h)


def _experts_body(be_ref, nb_ref, first_ref, nxt_ref, slot_ref, x_ref, w1_hbm, w3_hbm, w2_hbm, y_ref,
                  wb1, wb3, wb2, sem, *, layer):
    b = pl.program_id(0)
    used = b < nb_ref[0]
    s = slot_ref[b]

    def weight_copies(e, slot):
        return (pltpu.make_async_copy(w1_hbm.at[layer, e], wb1.at[slot], sem.at[slot, 0]),
                pltpu.make_async_copy(w3_hbm.at[layer, e], wb3.at[slot], sem.at[slot, 1]),
                pltpu.make_async_copy(w2_hbm.at[layer, e], wb2.at[slot], sem.at[slot, 2]))

    @pl.when(b == 0)
    def _():
        for cp in weight_copies(be_ref[0], 0):
            cp.start()

    @pl.when(jnp.logical_and(used, first_ref[b] == 1))
    def _():
        for cp in weight_copies(be_ref[b], s):
            cp.wait()

        @pl.when(nxt_ref[b] >= 0)
        def _():
            for cp in weight_copies(nxt_ref[b], 1 - s):
                cp.start()

    @pl.when(used)
    def _():
        x = _unpack_halves(x_ref[...])
        a = _dot(x, wb1[s])
        g = _dot(x, wb3[s])
        y_ref[...] = _pack_halves(_dot(_silu(a) * g, wb2[s]))

    @pl.when(jnp.logical_not(used))
    def _():
        y_ref[...] = jnp.zeros(y_ref.shape, y_ref.dtype)


def _experts(xs, block_e, nb_used, first, nxt_e, slot, w1, w3, w2, layer, tm):
    s, pw = xs.shape
    d, f = w1.shape[-2], w1.shape[-1]
    nblk = s // tm
    return pl.pallas_call(
        functools.partial(_experts_body, layer=layer),
        grid_spec=pltpu.PrefetchScalarGridSpec(
            num_scalar_prefetch=5,
            grid=(nblk,),
            in_specs=[pl.BlockSpec((tm, pw), lambda b, be, nb, *_: (jnp.minimum(b, nb[0] - 1), 0)),
                      pl.BlockSpec(memory_space=pl.ANY),
                      pl.BlockSpec(memory_space=pl.ANY),
                      pl.BlockSpec(memory_space=pl.ANY)],
            out_specs=pl.BlockSpec((tm, pw), lambda b, *_: (b, 0)),
            scratch_shapes=[pltpu.VMEM((2, d, f), F32),
                            pltpu.VMEM((2, d, f), F32),
                            pltpu.VMEM((2, f, d), F32),
                            pltpu.SemaphoreType.DMA((2, 3))]),
        out_shape=jax.ShapeDtypeStruct((s, pw), xs.dtype),
        compiler_params=_cparams(("arbitrary",)),
        name="moe_experts",
    )(block_e, nb_used, first, nxt_e, slot, xs, w1, w3, w2)


def _combine_body(eid_ref, rank_ref, pstart_ref, x_ref, meta_ref, gfin_ref, ys_ref, *rest, tm, n_first, final):
    if final:
        o_a_ref, o_b_ref, buf, sem = rest
    else:
        o_a_ref, buf, sem = rest
        o_b_ref = None
    i = pl.program_id(0)
    nsteps = pl.num_programs(0)

    def issue(step, slot):
        for r in range(tm):
            for kk in range(2):
                d = _slot_of(eid_ref, rank_ref, pstart_ref, kk * (nsteps * tm) + step * tm + r)
                pltpu.make_async_copy(ys_ref.at[pl.ds(d, 1), :], buf.at[slot, kk, pl.ds(r, 1), :],
                                      sem.at[slot]).start(priority=(r + kk) % 2)

    @pl.when(i == 0)
    def _():
        issue(0, 0)

    @pl.when(i + 1 < nsteps)
    def _():
        issue(i + 1, (i + 1) % 2)

    slot = i % 2
    for _ in range(2 * tm):
        pltpu.make_async_copy(ys_ref.at[pl.ds(0, 1), :], buf.at[slot, 0, pl.ds(0, 1), :], sem.at[slot]).wait()

    meta = meta_ref[...]
    g0 = meta[:, _META_G0:_META_G0 + 1]
    g1 = meta[:, _META_G1:_META_G1 + 1]
    out = x_ref[...] + (_unpack_halves(buf[slot, 0]) * g0 + _unpack_halves(buf[slot, 1]) * g1)
    if not final:
        o_a_ref[...] = out
    else:
        out = _rms(out, gfin_ref[...])

        @pl.when(i < n_first)
        def _():
            o_a_ref[...] = out

        @pl.when(i >= n_first)
        def _():
            o_b_ref[...] = out


def _combine(x, meta, eid, rank, p_start, ys, gfin, tm, n_first_rows, final):
    n, d = x.shape
    n_first = n_first_rows // tm
    if final:
        out_specs = [pl.BlockSpec((tm, d), lambda i, *_: (jnp.minimum(i, n_first - 1), 0)),
                     pl.BlockSpec((tm, d), lambda i, *_: (jnp.maximum(i - n_first, 0), 0))]
        out_shape = [jax.ShapeDtypeStruct((n_first_rows, d), F32),
                     jax.ShapeDtypeStruct((n - n_first_rows, d), F32)]
    else:
        out_specs = [pl.BlockSpec((tm, d), lambda i, *_: (i, 0))]
        out_shape = [jax.ShapeDtypeStruct((n, d), F32)]
    return pl.pallas_call(
        functools.partial(_combine_body, tm=tm, n_first=n_first, final=final),
        grid_spec=pltpu.PrefetchScalarGridSpec(
            num_scalar_prefetch=3,
            grid=(n // tm,),
            in_specs=[pl.BlockSpec((tm, d), lambda i, *_: (i, 0)),
                      pl.BlockSpec((tm, LANES), lambda i, *_: (i, 0)),
                      pl.BlockSpec((1, d), lambda i, *_: (0, 0)),
                      pl.BlockSpec(memory_space=pl.ANY)],
            out_specs=out_specs,
            scratch_shapes=[pltpu.VMEM((2, 2, tm, ys.shape[1]), ys.dtype),
                            pltpu.SemaphoreType.DMA((2,))]),
        out_shape=out_shape,
        compiler_params=_cparams(("arbitrary",)),
        name="moe_combine_final" if final else "moe_combine",
    )(eid, rank, p_start, x, meta, gfin.reshape(1, d), ys)


def _moe(x1, h, logits, rbias, w1, w3, w2, layer, gfin, n_first_rows, final, tm_e=192, tm_t=128):
    n, d = h.shape
    meta, idx, cnt = _route(logits, rbias, 256)
    idx = idx.astype(jnp.int32)
    eid = idx[:, 0:2, :].transpose(1, 0, 2).reshape(-1)
    rank = idx[:, 2:4, :].transpose(1, 0, 2).reshape(-1)
    counts = cnt[0, _EXP_LANE0:_EXP_LANE0 + N_EXPERTS].astype(jnp.int32)
    padded = (counts + tm_e - 1) // tm_e * tm_e
    p_end = jnp.cumsum(padded)
    p_start = (p_end - padded).astype(jnp.int32)
    s = n * 2
    nblk = (s + N_EXPERTS * (tm_e - 1) + tm_e - 1) // tm_e
    bidx = jnp.arange(nblk, dtype=jnp.int32)
    block_e = jnp.minimum(jnp.sum((p_end[None, :] <= (bidx * tm_e)[:, None]).astype(jnp.int32), axis=1),
                          N_EXPERTS - 1).astype(jnp.int32)
    nb_used = (p_end[-1:] // tm_e).astype(jnp.int32)
    prev_e = jnp.concatenate([jnp.full((1,), -1, jnp.int32), block_e[:-1]])
    first = ((bidx < nb_used[0]) & (block_e != prev_e)).astype(jnp.int32)
    slot = ((jnp.cumsum(first) - 1) % 2).astype(jnp.int32)
    eidx = jnp.arange(N_EXPERTS, dtype=jnp.int32)
    live = jnp.where(counts > 0, eidx, N_EXPERTS)
    next_live = jnp.min(jnp.where(eidx[None, :] > eidx[:, None], live[None, :], N_EXPERTS), axis=1)
    nxt_e = jnp.sum(jnp.where(block_e[:, None] == eidx[None, :], next_live[None, :], 0), axis=1)
    nxt_e = jnp.where(nxt_e < N_EXPERTS, nxt_e, -1).astype(jnp.int32)
    xs = _dispatch(h, eid, rank, p_start, (p_start + counts).astype(jnp.int32),
                   (padded - counts).astype(jnp.int32), nb_used, nblk, tm_t, tm_e)
    ys = _experts(xs, block_e, nb_used, first, nxt_e, slot, w1, w3, w2, layer, tm_e)
    return _combine(x1, meta, eid, rank, p_start, ys, gfin, tm_t, n_first_rows, final)


def _lambda_value(lam_ref, lambda_init):
    lam = lam_ref[...]
    s1 = jnp.sum(lam[0:1] * lam[1:2], axis=1, keepdims=True)
    s2 = jnp.sum(lam[2:3] * lam[3:4], axis=1, keepdims=True)
    return jnp.exp(s1) - jnp.exp(s2) + lambda_init


def _lane_tile(x, width):
    if width % LANES == 0:
        return jnp.concatenate([x] * (width // LANES), axis=1)
    return x[:, :width]


def _softmax_step(j, s, v, m_ref, l_ref, acc_ref, rows=slice(None)):
    m_prev = m_ref[j, rows]
    m_new = jnp.maximum(m_prev, jnp.max(s, axis=1, keepdims=True))
    alpha = jnp.exp(m_prev - m_new)
    p = jnp.exp(s - _lane_tile(m_new, s.shape[1]))
    l_ref[j, rows] = alpha * l_ref[j, rows] + jnp.sum(p, axis=1, keepdims=True)
    acc = acc_ref[j, rows]
    acc_ref[j, rows] = acc * _lane_tile(alpha, acc.shape[1]) + _dot(p, v)
    m_ref[j, rows] = m_new


def _attn_finish(lam_ref, g_ref, l_ref, acc_ref, lambda_init):
    lam = _lambda_value(lam_ref, lambda_init)
    width = acc_ref.shape[-1]
    o = acc_ref[0] / _lane_tile(l_ref[0], width) - lam * (acc_ref[1] / _lane_tile(l_ref[1], width))
    return _rms(o, g_ref[...]) * (1.0 - lambda_init)


def _alibi_slopes():
    slopes = [2.0 ** (-8.0 * (h + 1) / DIFF_HEADS) for h in range(DIFF_HEADS)]
    assert all(math.frexp(s)[0] == 0.5 for s in slopes)
    return slopes


def _attn_prompt_body(iq_ref, ik_ref, slope_ref, q_ref, k_ref, v_ref, kt_ref, lam_ref, g_ref,
                      o_ref, qa_ref, m_ref, l_ref, acc_ref, *, tq, lambda_init):
    hg = pl.program_id(1)
    t = pl.program_id(2)
    iq = iq_ref[t]
    ik = ik_ref[t]
    dh = DIFF_DH
    hw = 2 * dh
    slopes = [slope_ref[hg * ATTN_HEADS_PER_STEP + hh] for hh in range(ATTN_HEADS_PER_STEP)]

    @pl.when(ik == 0)
    def _():
        m_ref[...] = jnp.full(m_ref.shape, -jnp.inf, F32)
        l_ref[...] = jnp.zeros(l_ref.shape, F32)
        acc_ref[...] = jnp.zeros(acc_ref.shape, F32)
        qpos = iq * tq + lax.broadcasted_iota(jnp.int32, (tq, LANES), 0)
        lane = lax.broadcasted_iota(jnp.int32, (tq, LANES), 1)
        qa = (qpos // CHUNK).astype(F32)
        qb = (qpos % CHUNK).astype(F32)
        for hh, slope in enumerate(slopes):
            feat = jnp.where(lane == 0, qa * (-slope * CHUNK),
                             jnp.where(lane == 1, qb * (-slope), jnp.where(lane < 4, slope, 0.0)))
            for j in range(2):
                c0 = hh * hw + j * dh
                qa_ref[hh, j] = jnp.concatenate([q_ref[:, c0:c0 + dh] * (dh ** -0.5), feat], axis=1)

    kt = kt_ref[...]
    ts = tq // ATTN_ROW_SPLIT

    def keys(hh, j, nkeys):
        c0 = hh * hw + j * dh
        return jnp.concatenate([k_ref[0:nkeys, c0:c0 + dh], kt[:nkeys]], axis=1)

    @pl.when(ik < iq)
    def _():
        for hh in range(ATTN_HEADS_PER_STEP):
            v = v_ref[:, hh * hw:(hh + 1) * hw]
            for r in range(ATTN_ROW_SPLIT):
                rows = slice(r * ts, (r + 1) * ts)
                for j in range(2):
                    _softmax_step(j, _dot_nt(qa_ref[hh, j, rows], keys(hh, j, tq)), v,
                                  m_ref.at[hh], l_ref.at[hh], acc_ref.at[hh], rows)

    @pl.when(ik == iq)
    def _():
        for hh, slope in enumerate(slopes):
            for r in range(ATTN_ROW_SPLIT):
                rows = slice(r * ts, (r + 1) * ts)
                nkeys = (r + 1) * ts
                row = r * ts + lax.broadcasted_iota(jnp.int32, (ts, nkeys), 0)
                col = lax.broadcasted_iota(jnp.int32, (ts, nkeys), 1)
                fix = jnp.maximum(col - row, 0).astype(F32) * (-2.0 * slope)
                vis = (col // CHUNK) <= (row // CHUNK)
                v = v_ref[0:nkeys, hh * hw:(hh + 1) * hw]
                for j in range(2):
                    s = _dot_nt(qa_ref[hh, j, rows], keys(hh, j, nkeys))
                    _softmax_step(j, jnp.where(vis, s + fix, -jnp.inf), v,
                                  m_ref.at[hh], l_ref.at[hh], acc_ref.at[hh], rows)
            o_ref[:, hh * hw:(hh + 1) * hw] = _attn_finish(lam_ref, g_ref.at[hh], l_ref.at[hh], acc_ref.at[hh],
                                                            lambda_init)


def _attn_prompt(q, k, v, nb, t, lam, g_subln, lambda_init, tq):
    hw = 2 * DIFF_DH
    assert t % tq == 0 and (tq // ATTN_ROW_SPLIT) % CHUNK == 0 and t // CHUNK <= 256
    nq = t // tq
    pairs = [(iq, ik) for iq in range(nq) for ik in range(iq + 1)]
    iq_tab = jnp.asarray(np.array([p[0] for p in pairs], np.int32))
    ik_tab = jnp.asarray(np.array([p[1] for p in pairs], np.int32))
    slopes = jnp.asarray(np.array(_alibi_slopes(), np.float32))
    pos = np.arange(t)
    ktab = np.zeros((t, LANES), np.float32)
    ktab[:, 0] = 1.0
    ktab[:, 1] = 1.0
    ktab[:, 2] = (pos // CHUNK) * CHUNK
    ktab[:, 3] = pos % CHUNK
    g3 = g_subln.reshape(DIFF_HEADS, 1, hw)
    hps = ATTN_HEADS_PER_STEP
    assert DIFF_HEADS % hps == 0
    bw = hps * hw
    return pl.pallas_call(
        functools.partial(_attn_prompt_body, tq=tq, lambda_init=lambda_init),
        grid_spec=pltpu.PrefetchScalarGridSpec(
            num_scalar_prefetch=3,
            grid=(nb, DIFF_HEADS // hps, len(pairs)),
            in_specs=[pl.BlockSpec((tq, bw), lambda b, h, p, iqt, ikt, sl: (b * nq + iqt[p], h)),
                      pl.BlockSpec((tq, bw), lambda b, h, p, iqt, ikt, sl: (b * nq + ikt[p], h)),
                      pl.BlockSpec((tq, bw), lambda b, h, p, iqt, ikt, sl: (b * nq + ikt[p], h)),
                      pl.BlockSpec((tq, LANES), lambda b, h, p, iqt, ikt, sl: (ikt[p], 0)),
                      pl.BlockSpec((4, DIFF_DH), lambda b, h, p, iqt, ikt, sl: (0, 0)),
                      pl.BlockSpec((hps, 1, hw), lambda b, h, p, iqt, ikt, sl: (h, 0, 0))],
            out_specs=pl.BlockSpec((tq, bw), lambda b, h, p, iqt, ikt, sl: (b * nq + iqt[p], h)),
            scratch_shapes=[pltpu.VMEM((hps, 2, tq, hw), F32),
                            pltpu.VMEM((hps, 2, tq, LANES), F32),
                            pltpu.VMEM((hps, 2, tq, LANES), F32),
                            pltpu.VMEM((hps, 2, tq, hw), F32)]),
        out_shape=jax.ShapeDtypeStruct((nb * t, DIFF_HEADS * hw), F32),
        compiler_params=_cparams(("arbitrary", "arbitrary", "arbitrary")),
        name="diff_attn_prompt",
    )(iq_tab, ik_tab, slopes, q, k, v, jnp.asarray(ktab), lam, g3)


def _attn_sample_body(q_ref, kc_hbm, vc_hbm, kn_ref, vn_ref, lam_ref, g_ref,
                      o_ref, kbuf, vbuf, sem, m_ref, l_ref, acc_ref, *, li, tk, past, lambda_init):
    b = pl.program_id(0)
    ik = pl.program_id(1)
    nb = pl.num_programs(0)
    nkb = pl.num_programs(1) - 1
    dh = DIFF_DH
    hw = 2 * dh
    tq = q_ref.shape[0]
    scale = dh ** -0.5

    def fetch_copies(f):
        fb, fk = f // nkb, f % nkb
        slot = f % 2
        cps = []
        for h in range(DIFF_HEADS):
            for hbm, buf, c in ((kc_hbm, kbuf, 0), (vc_hbm, vbuf, 1)):
                cps.append(pltpu.make_async_copy(hbm.at[li, fb, pl.ds(fk * tk, tk), h, :], buf.at[slot, h],
                                                 sem.at[slot, c]))
        return cps

    fcur = b * nkb + ik

    @pl.when(jnp.logical_and(b == 0, ik == 0))
    def _():
        for cp in fetch_copies(0):
            cp.start()

    @pl.when(ik < nkb)
    def _():
        @pl.when(fcur + 1 < nb * nkb)
        def _():
            for cp in fetch_copies(fcur + 1):
                cp.start()

        for cp in fetch_copies(fcur):
            cp.wait()

    @pl.when(ik == 0)
    def _():
        m_ref[...] = jnp.full(m_ref.shape, -jnp.inf, F32)
        l_ref[...] = jnp.zeros(l_ref.shape, F32)
        acc_ref[...] = jnp.zeros(acc_ref.shape, F32)

    def attend(h, k, v, k0, width):
        slope = _alibi_slopes()[h]
        q = q_ref[:, h * hw:(h + 1) * hw]
        qpos = past + lax.broadcasted_iota(jnp.int32, (tq, width), 0)
        kpos = k0 + lax.broadcasted_iota(jnp.int32, (tq, width), 1)
        bias = jnp.abs(qpos - kpos).astype(F32) * (-slope)
        vis = (kpos // CHUNK) <= (qpos // CHUNK)
        for j in range(2):
            s = _dot_nt(q[:, j * dh:(j + 1) * dh], k[:, j * dh:(j + 1) * dh]) * scale + bias
            s = jnp.where(vis, s, -jnp.inf)
            _softmax_step(j, s, v, m_ref.at[h], l_ref.at[h], acc_ref.at[h])

    @pl.when(ik < nkb)
    def _():
        slot = fcur % 2
        for h in range(DIFF_HEADS):
            attend(h, kbuf[slot, h], vbuf[slot, h], ik * tk, tk)

    @pl.when(ik == nkb)
    def _():
        for h in range(DIFF_HEADS):
            hs = slice(h * hw, (h + 1) * hw)
            attend(h, kn_ref[:, hs], vn_ref[:, hs], past, tq)
            o_ref[:, hs] = _attn_finish(lam_ref, g_ref.at[h], l_ref.at[h], acc_ref.at[h], lambda_init)


def _attn_sample(q, k, v, cache_k, cache_v, li, lam, g_subln, lambda_init, tk):
    n_rows = q.shape[0]
    _, nb, past, nh, hw = cache_k.shape
    aw = nh * hw
    tq = n_rows // nb
    tk = min(tk, past)
    assert past % tk == 0
    nkb = past // tk
    g3 = g_subln.reshape(nh, 1, hw)
    cache_spec = pl.BlockSpec(memory_space=pl.ANY)
    row_spec = pl.BlockSpec((tq, aw), lambda b, ik: (b, 0))
    return pl.pallas_call(
        functools.partial(_attn_sample_body, li=li, tk=tk, past=past, lambda_init=lambda_init),
        grid=(nb, nkb + 1),
        in_specs=[row_spec,
                  cache_spec, cache_spec,
                  row_spec,
                  row_spec,
                  pl.BlockSpec((4, DIFF_DH), lambda b, ik: (0, 0)),
                  pl.BlockSpec((nh, 1, hw), lambda b, ik: (0, 0, 0))],
        out_specs=pl.BlockSpec((tq, aw), lambda b, ik: (b, 0)),
        scratch_shapes=[pltpu.VMEM((2, nh, tk, hw), F32),
                        pltpu.VMEM((2, nh, tk, hw), F32),
                        pltpu.SemaphoreType.DMA((2, 2)),
                        pltpu.VMEM((nh, 2, tq, LANES), F32),
                        pltpu.VMEM((nh, 2, tq, LANES), F32),
                        pltpu.VMEM((nh, 2, tq, hw), F32)],
        out_shape=jax.ShapeDtypeStruct((n_rows, aw), F32),
        compiler_params=_cparams(("arbitrary", "arbitrary")),
        name="diff_attn_sample",
    )(q, cache_k, cache_v, k, v, lam, g3)


def _pad_cols(a, width):
    return jnp.pad(a, ((0, 0), (0, width - a.shape[1])))


def _mixer_constants(dk, dv, w_gate_up, b_gate, g_gla, conv_w, conv_b, dt_bias, a_log, d_skip, g_ssd):
    C = REC_ROWS
    nh = GLA_HEADS
    inner = g_ssd.shape[0]
    n_ssd = inner // SSD_HEADDIM
    wup = jnp.pad(w_gate_up, ((0, LANES - w_gate_up.shape[0]), (0, 0)))
    tri = jnp.asarray(np.tril(np.ones((C, C), np.float32)), BF16)
    e64 = np.zeros((LANES, inner), np.float32)
    ec = np.zeros((LANES, n_ssd * C), np.float32)
    for hh in range(n_ssd):
        e64[hh, hh * SSD_HEADDIM:(hh + 1) * SSD_HEADDIM] = 1.0
        ec[hh, hh * C:(hh + 1) * C] = 1.0
    eye = np.tile(np.eye(C, dtype=np.float32), (1, n_ssd))
    caus = np.tile(np.tril(np.ones((C, C), np.float32)), (1, n_ssd))
    return [wup, b_gate.reshape(1, -1), g_gla.reshape(nh, dv), conv_w, conv_b.reshape(1, -1),
            _pad_cols(dt_bias.reshape(1, -1), LANES),
            _pad_cols(-jnp.exp(a_log.astype(F32)).reshape(1, -1), LANES),
            jnp.repeat(d_skip, SSD_HEADDIM).reshape(1, -1), g_ssd.reshape(1, -1),
            tri, jnp.asarray(e64, BF16), jnp.asarray(ec, BF16), jnp.asarray(eye), jnp.asarray(caus)]


def kernel(x_prompt, x_sample, state_gla, state_ssm, state_conv, cache_k, cache_v, norm_mix, norm_ffn, norm_final, w_in, w_gate_up, b_gate, g_gla, conv_w, conv_b, dt_bias, a_log, d_skip, g_ssd, w_out_mix, w_qkv, lam_q1, lam_k1, lam_q2, lam_k2, g_subln, w_o, router_group_w, router_group_b, router_expert_w, router_expert_b, w1, w3, w2):
    bp, tp, d = x_prompt.shape
    bs, ts, _ = x_sample.shape
    n_p, n_s = bp * tp, bs * ts
    n = n_p + n_s
    depth = norm_mix.shape[0]
    nh = GLA_HEADS
    dk, dv = state_gla.shape[-2], state_gla.shape[-1]
    qk_w, vw = nh * dk, nh * dv
    n_ssd = state_ssm.shape[2]
    inner = n_ssd * SSD_HEADDIM
    hpg = n_ssd // SSD_GROUPS
    gs = SSD_GROUPS * SSD_STATE
    cw = inner + 2 * gs
    rank = w_gate_up.shape[1]
    assert w_gate_up.shape[2] == qk_w and rank <= LANES and n_ssd <= LANES

    x = jnp.concatenate([x_prompt.reshape(n_p, d), x_sample.reshape(n_s, d)], axis=0)
    tm_row = next(tm for tm in (1408, 1056, 768, 256) if n % tm == 0)

    i_even = i_odd = 0
    gla_p, ssm_p, conv_p, gla_s, ssm_s, conv_s = [], [], [], [], [], []
    k_p, v_p, k_s, v_s = [], [], [], []
    y_p = y_s = None
    for layer in range(depth):
        final = layer == depth - 1
        if layer % 2 == 0:
            i = i_even
            i_even += 1
            offs = np.cumsum([0, qk_w, qk_w, vw, rank, vw, inner, cw, n_ssd])
            w_t = w_in[i].T
            seg = lambda j: w_t[offs[j]:offs[j + 1]]
            pad_rows = lambda a: jnp.pad(a, ((0, LANES - a.shape[0]), (0, 0)))
            w_cat = jnp.concatenate([seg(0), seg(1), seg(2), seg(4), seg(5), seg(6),
                                     pad_rows(seg(3)), pad_rows(seg(7))], axis=0).astype(BF16)
            proj = _norm_matmul(x, norm_mix[layer], w_cat, tm_row, 1280, w_is_transposed=True)
            consts = _mixer_constants(dk, dv, w_gate_up[i], b_gate[i], g_gla[i], conv_w[i], conv_b[i],
                                      dt_bias[i], a_log[i], d_skip[i], g_ssd[i])

            def to_group_state(sm):
                b_ = sm.shape[0]
                return sm.reshape(b_, SSD_GROUPS, hpg, SSD_STATE, SSD_HEADDIM).transpose(0, 1, 3, 2, 4) \
                         .reshape(b_, SSD_GROUPS, SSD_STATE, hpg * SSD_HEADDIM)

            def from_group_state(sg):
                b_ = sg.shape[0]
                return sg.reshape(b_, SSD_GROUPS, SSD_STATE, hpg, SSD_HEADDIM).transpose(0, 1, 3, 2, 4) \
                         .reshape(b_, n_ssd, SSD_STATE, SSD_HEADDIM)

            def pad_conv(cv):
                return jnp.pad(cv, ((0, 0), (CONV_PAD - cv.shape[1], 0), (0, 0)))

            mix_a, g_fin, s_fin, c_fin = _mixer_call(
                proj, 0, bp, tp, jnp.zeros((bp, nh, dv, dk), F32),
                jnp.zeros((bp, SSD_GROUPS, SSD_STATE, hpg * SSD_HEADDIM), F32),
                jnp.zeros((bp, CONV_PAD, cw), F32), consts, dk, dv)
            gla_p.append(g_fin.transpose(0, 1, 3, 2))
            ssm_p.append(from_group_state(s_fin))
            conv_p.append(c_fin[:, CONV_PAD - (SSD_CONV - 1):])
            mix_b, g_fin, s_fin, c_fin = _mixer_call(
                proj, n_p, bs, ts, state_gla[i].transpose(0, 1, 3, 2),
                to_group_state(state_ssm[i]), pad_conv(state_conv[i]), consts, dk, dv)
            gla_s.append(g_fin.transpose(0, 1, 3, 2))
            ssm_s.append(from_group_state(s_fin))
            conv_s.append(c_fin[:, CONV_PAD - (SSD_CONV - 1):])
            w_proj = w_out_mix[i]
        else:
            i = i_odd
            i_odd += 1
            lambda_init = 0.8 - 0.6 * math.exp(-0.3 * layer)
            tm_p = next(tm for tm in (1024, 512, 256) if n_p % tm == 0)
            w_qkv_b = w_qkv[i].astype(BF16)
            q_a, k_a, v_a = _norm_matmul(x, norm_mix[layer], w_qkv_b, tm_p, 1024, nrows=n_p, parts=3)
            q_b, k_b, v_b = _norm_matmul(x, norm_mix[layer], w_qkv_b, n_s, 1024, row0=n_p, parts=3)
            lam = jnp.stack([lam_q1[i], lam_k1[i], lam_q2[i], lam_k2[i]])
            mix_a = _attn_prompt(q_a, k_a, v_a, bp, tp, lam, g_subln[i], lambda_init, min(512, tp))
            mix_b = _attn_sample(q_b, k_b, v_b, cache_k, cache_v, i, lam, g_subln[i], lambda_init, 1024)
            k_p.append(k_a.reshape(bp, tp, DIFF_HEADS, 2 * DIFF_DH))
            v_p.append(v_a.reshape(bp, tp, DIFF_HEADS, 2 * DIFF_DH))
            k_s.append(k_b.reshape(bs, ts, DIFF_HEADS, 2 * DIFF_DH))
            v_s.append(v_b.reshape(bs, ts, DIFF_HEADS, 2 * DIFF_DH))
            w_proj = w_o[i]

        wr = _pad_cols(jnp.concatenate([router_group_w[layer], router_expert_w[layer]], axis=1), LANES)
        rbias = _pad_cols(jnp.concatenate([router_group_b[layer], router_expert_b[layer]]).reshape(1, -1), LANES)
        x1, h2, logits = _proj_resid(mix_a, mix_b, w_proj, x, norm_ffn[layer], wr, 256)
        res = _moe(x1, h2, logits, rbias, w1, w3, w2, layer, norm_final, n_p, final)
        if final:
            y_p, y_s = res
        else:
            x = res[0]

    return (y_p.reshape(bp, tp, d), y_s.reshape(bs, ts, d),
            jnp.stack(gla_p), jnp.stack(ssm_p), jnp.stack(conv_p), jnp.stack(k_p), jnp.stack(v_p),
            jnp.stack(gla_s), jnp.stack(ssm_s), jnp.stack(conv_s), jnp.stack(k_s), jnp.stack(v_s))
```

```python
import functools
import math

import numpy as np
import jax
import jax.numpy as jnp
from jax import lax
from jax.experimental import pallas as pl
from jax.experimental.pallas import tpu as pltpu

F32 = jnp.float32
BF16 = jnp.bfloat16

EPS = 1e-6
CHUNK = 64
GLA_HEADS = 4
GLA_TAU = 16.0
SSD_HEADDIM = 64
SSD_STATE = 128
SSD_GROUPS = 4
SSD_CONV = 4
DIFF_HEADS = 8
DIFF_DH = 128
MOE_GROUPS = 4
MOE_PER_GROUP = 8
N_EXPERTS = MOE_GROUPS * MOE_PER_GROUP

LANES = 128
SUBLANES = 8
REC_ROWS = 128
STRIP = 16
CONV_PAD = 8
ATTN_ROW_SPLIT = 2
ATTN_HEADS_PER_STEP = 4
VMEM_LIMIT = 56 * 1024 * 1024


def _cparams(sem, vmem=VMEM_LIMIT):
    return pltpu.CompilerParams(dimension_semantics=sem, vmem_limit_bytes=vmem)


def _dot(a, b):
    return jnp.dot(a, b, preferred_element_type=F32)


def _dot_nt(a, b):
    return lax.dot_general(a, b, (((1,), (1,)), ((), ())), preferred_element_type=F32)


def _dot_tn(a, b):
    return lax.dot_general(a, b, (((0,), (0,)), ((), ())), preferred_element_type=F32)


def _split_hi_lo(a):
    hi = a.astype(BF16)
    lo = (a - hi.astype(F32)).astype(BF16)
    return hi, lo


def _exact_left(m, a):
    hi, lo = _split_hi_lo(a)
    return _dot(m, hi) + _dot(m, lo)


def _exact_right(a, m):
    hi, lo = _split_hi_lo(a)
    return _dot(hi, m) + _dot(lo, m)


def _silu(x):
    h = 0.5 * x
    return h * (1.0 + jnp.tanh(h))


def _log1p_exp_neg_abs(x):
    e = jnp.exp(-jnp.abs(x))
    u = 1.0 + e
    return jnp.where(u == 1.0, e, jnp.log(u) * (e / (u - 1.0)))


def _rms(x, g):
    return x * lax.rsqrt(jnp.mean(x * x, axis=-1, keepdims=True) + EPS) * g


def _pack_halves(x):
    w = x.shape[1] // 2
    lo = lax.bitcast_convert_type(x[:, :w].astype(BF16).astype(F32), jnp.uint32)
    hi = lax.bitcast_convert_type(x[:, w:].astype(BF16).astype(F32), jnp.uint32)
    return (lo >> 16) | (hi & jnp.uint32(0xFFFF0000))


def _unpack_halves(p):
    lo = lax.bitcast_convert_type(p << 16, F32)
    hi = lax.bitcast_convert_type(p & jnp.uint32(0xFFFF0000), F32)
    return jnp.concatenate([lo, hi], axis=1)


PACK_GROUP = 512


def _pack_groups(x):
    return jnp.concatenate([_pack_halves(x[:, c:c + PACK_GROUP]) for c in range(0, x.shape[1], PACK_GROUP)],
                           axis=1)


def _unpack_groups(p):
    step = _packed_row(PACK_GROUP).shape[1]
    return jnp.concatenate([_unpack_halves(p[:, c:c + step]) for c in range(0, p.shape[1], step)], axis=1)


def _packed_row(d, fn=None):
    return jax.eval_shape(fn or _pack_halves, jax.ShapeDtypeStruct((SUBLANES, d), F32))


def _norm_matmul_body(x_ref, g_ref, w_ref, *refs, w_is_transposed, parts):
    o_refs, h_ref = refs[:parts], refs[parts]
    j = pl.program_id(1)

    @pl.when(j == 0)
    def _():
        h_ref[...] = _rms(x_ref[...], g_ref[...]).astype(h_ref.dtype)

    res = (_dot_nt if w_is_transposed else _dot)(h_ref[...], w_ref[...])
    if parts == 1:
        o_refs[0][...] = res
    else:
        per = pl.num_programs(1) // parts
        for p in range(parts):
            @pl.when(j // per == p)
            def _(p=p):
                o_refs[p][...] = res


def _norm_matmul(x, g, w, tm, tn, *, row0=0, nrows=None, parts=1, w_is_transposed=False):
    n, d = x.shape
    nrows = n - row0 if nrows is None else nrows
    nout = w.shape[0] if w_is_transposed else w.shape[1]
    assert nrows % tm == 0 and row0 % tm == 0 and nout % (tn * parts) == 0
    blk0 = row0 // tm
    per = nout // tn // parts
    w_spec = (pl.BlockSpec((tn, d), lambda i, j: (j, 0)) if w_is_transposed
              else pl.BlockSpec((d, tn), lambda i, j: (0, j)))
    out_specs = [pl.BlockSpec((tm, tn), lambda i, j, p=p: (i, jnp.clip(j - p * per, 0, per - 1)))
                 for p in range(parts)]
    outs = pl.pallas_call(
        functools.partial(_norm_matmul_body, w_is_transposed=w_is_transposed, parts=parts),
        grid=(nrows // tm, nout // tn),
        in_specs=[pl.BlockSpec((tm, d), lambda i, j: (blk0 + i, 0), pipeline_mode=pl.Buffered(1)),
                  pl.BlockSpec((1, d), lambda i, j: (0, 0)),
                  w_spec],
        out_specs=out_specs,
        out_shape=[jax.ShapeDtypeStruct((nrows, nout // parts), F32)] * parts,
        scratch_shapes=[pltpu.VMEM((tm, d), w.dtype)],
        compiler_params=_cparams(("arbitrary", "arbitrary")),
        name="norm_matmul",
    )(x, g.reshape(1, d), w)
    return outs[0] if parts == 1 else outs


def _mixer_body(p_ref, gla0_ref, ssm0_ref, conv0_ref, wup_ref, bgate_ref, ggla_ref, convw_ref,
                convb_ref, dtb_ref, aneg_ref, dskip_ref, gssd_ref, tri_ref, e64_ref, ec_ref,
                eye_ref, caus_ref,
                o_ref, gla_ref, ssm_ref, ctail_ref,
                sg_ref, ss_ref, ext_ref, b_ref, *, rb, dk, dv):
    C = REC_ROWS
    c = pl.program_id(1)
    nc = pl.num_programs(1)
    nh = GLA_HEADS
    qk_w = nh * dk
    vw = nh * dv
    inner = vw
    gs = SSD_GROUPS * SSD_STATE
    o_q, o_k, o_v = 0, qk_w, 2 * qk_w
    o_r = o_v + vw
    o_z = o_r + vw
    o_x = o_z + inner
    o_g = o_x + inner + 2 * gs
    o_dt = o_g + LANES

    @pl.when(c == 0)
    def _init():
        sg_ref[...] = gla0_ref[...]
        ss_ref[...] = ssm0_ref[...]
        ext_ref[0:CONV_PAD, :] = conv0_ref[...]

    p = p_ref[...]
    if rb < C:
        p = jnp.concatenate([p, jnp.zeros((C - rb, p.shape[1]), F32)], axis=0)

    def rowmask(width):
        return lax.broadcasted_iota(jnp.int32, (C, width), 0) < rb

    q = p[:, o_q:o_q + qk_w] * (dk ** -0.5)
    k = p[:, o_k:o_k + qk_w]
    v = p[:, o_v:o_v + vw]
    r = p[:, o_r:o_r + vw]
    z = p[:, o_z:o_z + inner]
    xbc = p[:, o_x:o_x + inner + 2 * gs]
    glr = p[:, o_g:o_g + LANES]
    dtp = p[:, o_dt:o_dt + LANES]

    tri = tri_ref[...]

    zg = _dot(glr, wup_ref[...]) + bgate_ref[...]
    la = (jnp.minimum(zg, 0.0) - _log1p_exp_neg_abs(zg)) * (1.0 / GLA_TAU)
    if rb < C:
        la = jnp.where(rowmask(qk_w), la, 0.0)
    bcum = _exact_left(tri, la)
    b_ref[...] = bcum

    row_i = lax.broadcasted_iota(jnp.int32, (C, dk), 0)
    prow = lax.broadcasted_iota(jnp.int32, (STRIP, C), 0)
    pcol = lax.broadcasted_iota(jnp.int32, (STRIP, C), 1)
    o_heads = []
    for h in range(nh):
        hs = slice(h * dk, (h + 1) * dk)
        vs = slice(h * dv, (h + 1) * dv)
        bh = bcum[:, hs]
        qh = q[:, hs]
        kh = k[:, hs]
        vh = v[:, vs]
        strips = []
        for i in range(C // STRIP):
            r0 = i * STRIP
            if i == 0:
                ref_row = jnp.zeros((1, dk), F32)
            else:
                ref_row = b_ref[pl.ds(r0 - 1, 1), hs]
            q_i = qh[r0:r0 + STRIP] * jnp.exp(bh[r0:r0 + STRIP] - ref_row)
            e = jnp.where(row_i < r0 + STRIP, ref_row - bh, 0.0)
            k_i = kh * jnp.exp(e)
            s_i = _dot_nt(q_i, k_i)
            strips.append(jnp.where(pcol <= prow + r0, s_i, 0.0))
        pmat = jnp.concatenate(strips, axis=0)
        b_last = b_ref[pl.ds(C - 1, 1), hs]
        st = sg_ref[h]
        o_h = _dot(pmat, vh) + _dot_nt(qh * jnp.exp(bh), st)
        k_st = kh * jnp.exp(b_last - bh)
        sg_ref[h] = st * jnp.exp(b_last) + _dot_tn(vh, k_st)
        o_h = _rms(o_h, ggla_ref[pl.ds(h, 1), :]) * _silu(r[:, vs])
        o_heads.append(o_h)
    o_a = jnp.concatenate(o_heads, axis=1)

    cw = inner + 2 * gs
    ext_ref[CONV_PAD:CONV_PAD + C, :] = xbc
    conv = convb_ref[...]
    for j in range(SSD_CONV):
        conv = conv + convw_ref[pl.ds(j, 1), :] * ext_ref[pl.ds(CONV_PAD - (SSD_CONV - 1) + j, C), :]
    xc = _silu(conv)
    xs = xc[:, :inner]
    bm = xc[:, inner:inner + gs]
    cm = xc[:, inner + gs:cw]
    dtv = dtp + dtb_ref[...]
    dt = jnp.maximum(dtv, 0.0) + _log1p_exp_neg_abs(dtv)
    ld = dt * aneg_ref[...]
    if rb < C:
        dt = jnp.where(rowmask(LANES), dt, 0.0)
        ld = jnp.where(rowmask(LANES), ld, 0.0)
    bs = _exact_left(tri, ld)
    e64 = e64_ref[...]
    dt64 = _exact_right(dt, e64)
    b64 = _exact_right(bs, e64)
    blast64 = b64[C - 1:C, :]
    xdt = xs * dt64
    bc = _exact_right(bs, ec_ref[...])
    br = jnp.sum(eye_ref[...] * bc, axis=0, keepdims=True)
    vis = caus_ref[...] > 0.0
    lf = jnp.where(vis, jnp.exp(jnp.where(vis, bc - br, 0.0)), 0.0)
    hpg = inner // SSD_HEADDIM // SSD_GROUPS
    gw = hpg * SSD_HEADDIM
    lane_head = lax.broadcasted_iota(jnp.int32, (C, gw), 1) // SSD_HEADDIM
    y_groups = []
    for g in range(SSD_GROUPS):
        cm_g = cm[:, g * SSD_STATE:(g + 1) * SSD_STATE]
        bm_g = bm[:, g * SSD_STATE:(g + 1) * SSD_STATE]
        gl = slice(g * gw, (g + 1) * gw)
        gmat = _dot_nt(cm_g, bm_g)
        a4 = jnp.concatenate([gmat] * hpg, axis=1) * lf[:, g * hpg * C:(g + 1) * hpg * C]
        xg = xdt[:, gl]
        x4 = jnp.concatenate([jnp.where(lane_head == hh, xg, 0.0) for hh in range(hpg)], axis=0)
        s_g = ss_ref[g]
        y_g = _dot(a4, x4) + _dot(cm_g, s_g) * jnp.exp(b64[:, gl])
        xw = xg * jnp.exp(blast64[:, gl] - b64[:, gl])
        ss_ref[g] = s_g * jnp.exp(blast64[:, gl]) + _dot_tn(bm_g, xw)
        y_groups.append(y_g)
    y = jnp.concatenate(y_groups, axis=1) + xs * dskip_ref[...]
    y = _rms(y * _silu(z), gssd_ref[...])

    o_full = jnp.concatenate([o_a, y], axis=1)
    o_ref[...] = _pack_groups(o_full[:rb])

    @pl.when(c == nc - 1)
    def _fin():
        gla_ref[...] = sg_ref[...]
        ssm_ref[...] = ss_ref[...]
        ctail_ref[...] = ext_ref[pl.ds(rb, CONV_PAD), :]

    ext_ref[0:CONV_PAD, :] = ext_ref[pl.ds(rb, CONV_PAD), :]


def _mixer_call(proj, row_off, nb, t, gla0, ssm0, conv0, consts, dk, dv):
    C = REC_ROWS
    rb = min(t, C)
    assert t % rb == 0 and row_off % rb == 0
    steps = t // rb
    width = proj.shape[1]
    nh = GLA_HEADS
    vw = nh * dv
    cw = conv0.shape[-1]
    blk0 = row_off // rb

    def full(a):
        nd = a.ndim
        return pl.BlockSpec(a.shape, lambda b, c, _n=nd: (0,) * _n)

    in_specs = [pl.BlockSpec((rb, width), lambda b, c: (blk0 + b * steps + c, 0)),
                pl.BlockSpec((None, nh, dv, dk), lambda b, c: (b, 0, 0, 0)),
                pl.BlockSpec((None, SSD_GROUPS, SSD_STATE, ssm0.shape[-1]), lambda b, c: (b, 0, 0, 0)),
                pl.BlockSpec((None, CONV_PAD, cw), lambda b, c: (b, 0, 0))]
    in_specs += [full(a) for a in consts]
    operands = [proj, gla0, ssm0, conv0, *consts]
    op = _packed_row(2 * vw, _pack_groups)
    out_specs = [pl.BlockSpec((rb, op.shape[1]), lambda b, c: (b * steps + c, 0)),
                 pl.BlockSpec((None, nh, dv, dk), lambda b, c: (b, 0, 0, 0)),
                 pl.BlockSpec((None, SSD_GROUPS, SSD_STATE, ssm0.shape[-1]), lambda b, c: (b, 0, 0, 0)),
                 pl.BlockSpec((None, CONV_PAD, cw), lambda b, c: (b, 0, 0))]
    out_shape = [jax.ShapeDtypeStruct((nb * t, op.shape[1]), op.dtype),
                 jax.ShapeDtypeStruct(gla0.shape, F32),
                 jax.ShapeDtypeStruct(ssm0.shape, F32),
                 jax.ShapeDtypeStruct((nb, CONV_PAD, cw), F32)]
    return pl.pallas_call(
        functools.partial(_mixer_body, rb=rb, dk=dk, dv=dv),
        grid=(nb, steps),
        in_specs=in_specs,
        out_specs=out_specs,
        out_shape=out_shape,
        scratch_shapes=[pltpu.VMEM((nh, dv, dk), F32),
                        pltpu.VMEM((SSD_GROUPS, SSD_STATE, ssm0.shape[-1]), F32),
                        pltpu.VMEM((C + CONV_PAD, cw), F32),
                        pltpu.VMEM((C, nh * dk), F32)],
        compiler_params=_cparams(("arbitrary", "arbitrary")),
        name="gla_ssd_mixer",
    )(*operands)


def _proj_resid_body(oa_ref, ob_ref, w_ref, x_ref, g_ref, wr_ref, x1_ref, h_ref, lg_ref, *, n_first):
    o = _unpack_groups(jnp.where(pl.program_id(0) < n_first, oa_ref[...], ob_ref[...]))
    x1 = x_ref[...] + _dot(o, w_ref[...])
    x1_ref[...] = x1
    h = _rms(x1, g_ref[...])
    h_ref[...] = _pack_halves(h)
    lg_ref[...] = _dot(h, wr_ref[...])


def _proj_resid(o_a, o_b, w, x, g, wr, tm):
    n, d = x.shape
    kin = o_a.shape[1]
    d_in = w.shape[0]
    assert o_a.shape[0] % tm == 0 and o_b.shape[0] % tm == 0
    n_first = o_a.shape[0] // tm
    hp = _packed_row(d)
    return pl.pallas_call(
        functools.partial(_proj_resid_body, n_first=n_first),
        grid=(n // tm,),
        in_specs=[pl.BlockSpec((tm, kin), lambda i: (jnp.minimum(i, n_first - 1), 0)),
                  pl.BlockSpec((tm, kin), lambda i: (jnp.maximum(i - n_first, 0), 0)),
                  pl.BlockSpec((d_in, d), lambda i: (0, 0), pipeline_mode=pl.Buffered(1)),
                  pl.BlockSpec((tm, d), lambda i: (i, 0)),
                  pl.BlockSpec((1, d), lambda i: (0, 0)),
                  pl.BlockSpec((d, LANES), lambda i: (0, 0))],
        out_specs=[pl.BlockSpec((tm, d), lambda i: (i, 0)),
                   pl.BlockSpec((tm, hp.shape[1]), lambda i: (i, 0)),
                   pl.BlockSpec((tm, LANES), lambda i: (i, 0))],
        out_shape=[jax.ShapeDtypeStruct((n, d), F32),
                   jax.ShapeDtypeStruct((n, hp.shape[1]), hp.dtype),
                   jax.ShapeDtypeStruct((n, LANES), F32)],
        compiler_params=_cparams(("arbitrary",)),
        name="proj_resid_norm_router",
    )(o_a, o_b, w, x, g.reshape(1, d), wr)


_META_E0, _META_E1, _META_G0, _META_G1, _META_R0, _META_R1 = range(6)
_EXP_LANE0 = MOE_GROUPS


def _route_body(lg_ref, bias_ref, ltri_ref, eye_ref, meta_ref, idx_ref, cnt_ref, base_ref):
    i = pl.program_id(0)

    @pl.when(i == 0)
    def _():
        base_ref[...] = jnp.zeros_like(base_ref)

    lg = lg_ref[...] + bias_ref[...]
    tm = lg.shape[0]
    lane_i = lax.broadcasted_iota(jnp.int32, (tm, LANES), 1)
    lane = lane_i.astype(F32)
    neg = -jnp.inf
    glog = jnp.where(lane_i < MOE_GROUPS, lg, neg)
    gmax = jnp.max(glog, axis=1, keepdims=True)
    gsel = jnp.min(jnp.where(glog == gmax, lane, float(LANES)), axis=1, keepdims=True)
    pg = 1.0 / jnp.sum(jnp.exp(glog - gmax), axis=1, keepdims=True)
    lo = _EXP_LANE0 + MOE_PER_GROUP * gsel
    el = jnp.where((lane >= lo) & (lane < lo + MOE_PER_GROUP), lg, neg)
    v1 = jnp.max(el, axis=1, keepdims=True)
    i1 = jnp.min(jnp.where(el == v1, lane, float(LANES)), axis=1, keepdims=True)
    el2 = jnp.where(lane == i1, neg, el)
    v2 = jnp.max(el2, axis=1, keepdims=True)
    i2 = jnp.min(jnp.where(el2 == v2, lane, float(LANES)), axis=1, keepdims=True)
    e = jnp.exp(v2 - v1)
    g1 = pg / (1.0 + e)
    g2 = pg * e / (1.0 + e)
    hot1 = lane == i1
    hot2 = lane == i2
    onehot = jnp.where(hot1 | hot2, 1.0, 0.0)
    before = _dot(ltri_ref[...], onehot.astype(BF16)) + base_ref[0:1, :]
    r1 = jnp.sum(jnp.where(hot1, before, 0.0), axis=1, keepdims=True)
    r2 = jnp.sum(jnp.where(hot2, before, 0.0), axis=1, keepdims=True)
    base_ref[0:1, :] = base_ref[0:1, :] + jnp.sum(onehot, axis=0, keepdims=True)
    meta = jnp.zeros((tm, LANES), F32)
    for idx, val in ((_META_E0, i1 - _EXP_LANE0), (_META_E1, i2 - _EXP_LANE0),
                     (_META_G0, g1), (_META_G1, g2), (_META_R0, r1), (_META_R1, r2)):
        meta = jnp.where(lane_i == idx, val, meta)
    meta_ref[...] = meta

    eye = eye_ref[...]
    rows = [jnp.sum(eye * col, axis=0, keepdims=True)
            for col in (i1 - _EXP_LANE0, i2 - _EXP_LANE0, r1, r2)]
    idx_ref[...] = jnp.concatenate(rows + [jnp.zeros((SUBLANES - len(rows), tm), F32)], axis=0)

    @pl.when(i == pl.num_programs(0) - 1)
    def _():
        cnt_ref[...] = base_ref[...]


def _route(logits, bias, tm):
    n = logits.shape[0]
    ltri = jnp.asarray(np.tril(np.ones((tm, tm), np.float32), -1), BF16)
    eye = jnp.asarray(np.eye(tm, dtype=np.float32))
    return pl.pallas_call(
        _route_body,
        grid=(n // tm,),
        in_specs=[pl.BlockSpec((tm, LANES), lambda i: (i, 0)),
                  pl.BlockSpec((1, LANES), lambda i: (0, 0)),
                  pl.BlockSpec((tm, tm), lambda i: (0, 0)),
                  pl.BlockSpec((tm, tm), lambda i: (0, 0))],
        out_specs=[pl.BlockSpec((tm, LANES), lambda i: (i, 0)),
                   pl.BlockSpec((None, SUBLANES, tm), lambda i: (i, 0, 0)),
                   pl.BlockSpec((8, LANES), lambda i: (0, 0))],
        out_shape=[jax.ShapeDtypeStruct((n, LANES), F32),
                   jax.ShapeDtypeStruct((n // tm, SUBLANES, tm), F32),
                   jax.ShapeDtypeStruct((8, LANES), F32)],
        scratch_shapes=[pltpu.VMEM((8, LANES), F32)],
        compiler_params=_cparams(("arbitrary",)),
        name="moe_route",
    )(logits, bias, ltri, eye)


def _slot_of(eid_ref, rank_ref, pstart_ref, idx):
    return pstart_ref[eid_ref[idx]] + rank_ref[idx]


def _dispatch_body(eid_ref, rank_ref, pstart_ref, pad0_ref, npad_ref, nb_ref, h_ref, xs_ref, zbuf, sem, zsem,
                   *, tm, tm_e, nblk):
    i = pl.program_id(0)
    n_tok = pl.num_programs(0) * tm

    def pad_fill(wait):
        def go(cp):
            if wait:
                cp.wait()
            else:
                cp.start()

        def body(e, carry):
            off = pad0_ref[e]
            npad = npad_ref[e]
            head = (SUBLANES - off % SUBLANES) % SUBLANES
            for j in range(SUBLANES - 1):
                @pl.when(j < head)
                def _(j=j):
                    go(pltpu.make_async_copy(zbuf.at[pl.ds(0, 1), :], xs_ref.at[pl.ds(off + j, 1), :], zsem))
            off = pl.multiple_of(off + head, SUBLANES)
            rem = npad - head
            bit = pl.next_power_of_2(tm_e) // 2
            while bit >= SUBLANES:
                on = (rem & bit) != 0

                @pl.when(on)
                def _(off=off, bit=bit):
                    go(pltpu.make_async_copy(zbuf.at[pl.ds(0, bit), :], xs_ref.at[pl.ds(off, bit), :], zsem))

                off = pl.multiple_of(off + jnp.where(on, bit, 0), SUBLANES)
                bit //= 2
            return carry
        lax.fori_loop(0, N_EXPERTS, body, 0)

        def tail(tb, carry):
            go(pltpu.make_async_copy(zbuf, xs_ref.at[pl.ds(pl.multiple_of(tb * tm_e, tm_e), tm_e), :], zsem))
            return carry
        lax.fori_loop(nb_ref[0], nblk, tail, 0)

    @pl.when(i == 0)
    def _():
        zbuf[...] = jnp.zeros(zbuf.shape, zbuf.dtype)
        pad_fill(False)

    for r in range(tm):
        for kk in range(2):
            d = _slot_of(eid_ref, rank_ref, pstart_ref, kk * n_tok + i * tm + r)
            pltpu.make_async_copy(h_ref.at[pl.ds(r, 1), :], xs_ref.at[pl.ds(d, 1), :], sem).start(
                priority=(r + kk) % 2)

    @pl.when(i == 0)
    def _():
        pad_fill(True)

    for _ in range(2 * tm):
        pltpu.make_async_copy(h_ref.at[pl.ds(0, 1), :], xs_ref.at[pl.ds(0, 1), :], sem).wait()


def _dispatch(h, eid, rank, p_start, pad0, npad, nb_used, nblk, tm, tm_e):
    n, d = h.shape
    return pl.pallas_call(
        functools.partial(_dispatch_body, tm=tm, tm_e=tm_e, nblk=nblk),
        grid_spec=pltpu.PrefetchScalarGridSpec(
            num_scalar_prefetch=6,
            grid=(n // tm,),
            in_specs=[pl.BlockSpec((tm, d), lambda i, *_: (i, 0))],
            out_specs=pl.BlockSpec(memory_space=pl.ANY),
            scratch_shapes=[pltpu.VMEM((tm_e, d), h.dtype),
                            pltpu.SemaphoreType.DMA(()),
                            pltpu.SemaphoreType.DMA(())]),
        out_shape=jax.ShapeDtypeStruct((nblk * tm_e, d), h.dtype),
        compiler_params=_cparams(("arbitrary",)),
        name="moe_dispatch",
    )(eid, rank, p_start, pad0, npad, nb_used, h)


def _experts_body(be_ref, nb_ref, first_ref, nxt_ref, slot_ref, x_ref, w1_hbm, w3_hbm, w2_hbm, y_ref,
                  wb1, wb3, wb2, sem, *, layer):
    b = pl.program_id(0)
    used = b < nb_ref[0]
    s = slot_ref[b]

    def weight_copies(e, slot):
        return (pltpu.make_async_copy(w1_hbm.at[layer, e], wb1.at[slot], sem.at[slot, 0]),
                pltpu.make_async_copy(w3_hbm.at[layer, e], wb3.at[slot], sem.at[slot, 1]),
                pltpu.make_async_copy(w2_hbm.at[layer, e], wb2.at[slot], sem.at[slot, 2]))

    @pl.when(b == 0)
    def _():
        for cp in weight_copies(be_ref[0], 0):
            cp.start()

    @pl.when(jnp.logical_and(used, first_ref[b] == 1))
    def _():
        for cp in weight_copies(be_ref[b], s):
            cp.wait()

        @pl.when(nxt_ref[b] >= 0)
        def _():
            for cp in weight_copies(nxt_ref[b], 1 - s):
                cp.start()

    @pl.when(used)
    def _():
        x = _unpack_halves(x_ref[...])
        a = _dot(x, wb1[s])
        g = _dot(x, wb3[s])
        y_ref[...] = _pack_halves(_dot(_silu(a) * g, wb2[s]))

    @pl.when(jnp.logical_not(used))
    def _():
        y_ref[...] = jnp.zeros(y_ref.shape, y_ref.dtype)


def _experts(xs, block_e, nb_used, first, nxt_e, slot, w1, w3, w2, layer, tm):
    s, pw = xs.shape
    d, f = w1.shape[-2], w1.shape[-1]
    nblk = s // tm
    return pl.pallas_call(
        functools.partial(_experts_body, layer=layer),
        grid_spec=pltpu.PrefetchScalarGridSpec(
            num_scalar_prefetch=5,
            grid=(nblk,),
            in_specs=[pl.BlockSpec((tm, pw), lambda b, be, nb, *_: (jnp.minimum(b, nb[0] - 1), 0)),
                      pl.BlockSpec(memory_space=pl.ANY),
                      pl.BlockSpec(memory_space=pl.ANY),
                      pl.BlockSpec(memory_space=pl.ANY)],
            out_specs=pl.BlockSpec((tm, pw), lambda b, *_: (b, 0)),
            scratch_shapes=[pltpu.VMEM((2, d, f), F32),
                            pltpu.VMEM((2, d, f), F32),
                            pltpu.VMEM((2, f, d), F32),
                            pltpu.SemaphoreType.DMA((2, 3))]),
        out_shape=jax.ShapeDtypeStruct((s, pw), xs.dtype),
        compiler_params=_cparams(("arbitrary",)),
        name="moe_experts",
    )(block_e, nb_used, first, nxt_e, slot, xs, w1, w3, w2)


def _combine_body(eid_ref, rank_ref, pstart_ref, x_ref, meta_ref, gfin_ref, ys_ref, *rest, tm, n_first, final):
    if final:
        o_a_ref, o_b_ref, buf, sem = rest
    else:
        o_a_ref, buf, sem = rest
        o_b_ref = None
    i = pl.program_id(0)
    nsteps = pl.num_programs(0)

    def issue(step, slot):
        for r in range(tm):
            for kk in range(2):
                d = _slot_of(eid_ref, rank_ref, pstart_ref, kk * (nsteps * tm) + step * tm + r)
                pltpu.make_async_copy(ys_ref.at[pl.ds(d, 1), :], buf.at[slot, kk, pl.ds(r, 1), :],
                                      sem.at[slot]).start(priority=(r + kk) % 2)

    @pl.when(i == 0)
    def _():
        issue(0, 0)

    @pl.when(i + 1 < nsteps)
    def _():
        issue(i + 1, (i + 1) % 2)

    slot = i % 2
    for _ in range(2 * tm):
        pltpu.make_async_copy(ys_ref.at[pl.ds(0, 1), :], buf.at[slot, 0, pl.ds(0, 1), :], sem.at[slot]).wait()

    meta = meta_ref[...]
    g0 = meta[:, _META_G0:_META_G0 + 1]
    g1 = meta[:, _META_G1:_META_G1 + 1]
    out = x_ref[...] + (_unpack_halves(buf[slot, 0]) * g0 + _unpack_halves(buf[slot, 1]) * g1)
    if not final:
        o_a_ref[...] = out
    else:
        out = _rms(out, gfin_ref[...])

        @pl.when(i < n_first)
        def _():
            o_a_ref[...] = out

        @pl.when(i >= n_first)
        def _():
            o_b_ref[...] = out


def _combine(x, meta, eid, rank, p_start, ys, gfin, tm, n_first_rows, final):
    n, d = x.shape
    n_first = n_first_rows // tm
    if final:
        out_specs = [pl.BlockSpec((tm, d), lambda i, *_: (jnp.minimum(i, n_first - 1), 0)),
                     pl.BlockSpec((tm, d), lambda i, *_: (jnp.maximum(i - n_first, 0), 0))]
        out_shape = [jax.ShapeDtypeStruct((n_first_rows, d), F32),
                     jax.ShapeDtypeStruct((n - n_first_rows, d), F32)]
    else:
        out_specs = [pl.BlockSpec((tm, d), lambda i, *_: (i, 0))]
        out_shape = [jax.ShapeDtypeStruct((n, d), F32)]
    return pl.pallas_call(
        functools.partial(_combine_body, tm=tm, n_first=n_first, final=final),
        grid_spec=pltpu.PrefetchScalarGridSpec(
            num_scalar_prefetch=3,
            grid=(n // tm,),
            in_specs=[pl.BlockSpec((tm, d), lambda i, *_: (i, 0)),
                      pl.BlockSpec((tm, LANES), lambda i, *_: (i, 0)),
                      pl.BlockSpec((1, d), lambda i, *_: (0, 0)),
                      pl.BlockSpec(memory_space=pl.ANY)],
            out_specs=out_specs,
            scratch_shapes=[pltpu.VMEM((2, 2, tm, ys.shape[1]), ys.dtype),
                            pltpu.SemaphoreType.DMA((2,))]),
        out_shape=out_shape,
        compiler_params=_cparams(("arbitrary",)),
        name="moe_combine_final" if final else "moe_combine",
    )(eid, rank, p_start, x, meta, gfin.reshape(1, d), ys)


def _moe(x1, h, logits, rbias, w1, w3, w2, layer, gfin, n_first_rows, final, tm_e=192, tm_t=128):
    n, d = h.shape
    meta, idx, cnt = _route(logits, rbias, 256)
    idx = idx.astype(jnp.int32)
    eid = idx[:, 0:2, :].transpose(1, 0, 2).reshape(-1)
    rank = idx[:, 2:4, :].transpose(1, 0, 2).reshape(-1)
    counts = cnt[0, _EXP_LANE0:_EXP_LANE0 + N_EXPERTS].astype(jnp.int32)
    padded = (counts + tm_e - 1) // tm_e * tm_e
    p_end = jnp.cumsum(padded)
    p_start = (p_end - padded).astype(jnp.int32)
    s = n * 2
    nblk = (s + N_EXPERTS * (tm_e - 1) + tm_e - 1) // tm_e
    bidx = jnp.arange(nblk, dtype=jnp.int32)
    block_e = jnp.minimum(jnp.sum((p_end[None, :] <= (bidx * tm_e)[:, None]).astype(jnp.int32), axis=1),
                          N_EXPERTS - 1).astype(jnp.int32)
    nb_used = (p_end[-1:] // tm_e).astype(jnp.int32)
    prev_e = jnp.concatenate([jnp.full((1,), -1, jnp.int32), block_e[:-1]])
    first = ((bidx < nb_used[0]) & (block_e != prev_e)).astype(jnp.int32)
    slot = ((jnp.cumsum(first) - 1) % 2).astype(jnp.int32)
    eidx = jnp.arange(N_EXPERTS, dtype=jnp.int32)
    live = jnp.where(counts > 0, eidx, N_EXPERTS)
    next_live = jnp.min(jnp.where(eidx[None, :] > eidx[:, None], live[None, :], N_EXPERTS), axis=1)
    nxt_e = jnp.sum(jnp.where(block_e[:, None] == eidx[None, :], next_live[None, :], 0), axis=1)
    nxt_e = jnp.where(nxt_e < N_EXPERTS, nxt_e, -1).astype(jnp.int32)
    xs = _dispatch(h, eid, rank, p_start, (p_start + counts).astype(jnp.int32),
                   (padded - counts).astype(jnp.int32), nb_used, nblk, tm_t, tm_e)
    ys = _experts(xs, block_e, nb_used, first, nxt_e, slot, w1, w3, w2, layer, tm_e)
    return _combine(x1, meta, eid, rank, p_start, ys, gfin, tm_t, n_first_rows, final)


def _lambda_value(lam_ref, lambda_init):
    lam = lam_ref[...]
    s1 = jnp.sum(lam[0:1] * lam[1:2], axis=1, keepdims=True)
    s2 = jnp.sum(lam[2:3] * lam[3:4], axis=1, keepdims=True)
    return jnp.exp(s1) - jnp.exp(s2) + lambda_init


def _lane_tile(x, width):
    if width % LANES == 0:
        return jnp.concatenate([x] * (width // LANES), axis=1)
    return x[:, :width]


def _softmax_step(j, s, v, m_ref, l_ref, acc_ref, rows=slice(None)):
    m_prev = m_ref[j, rows]
    m_new = jnp.maximum(m_prev, jnp.max(s, axis=1, keepdims=True))
    alpha = jnp.exp(m_prev - m_new)
    p = jnp.exp(s - _lane_tile(m_new, s.shape[1]))
    l_ref[j, rows] = alpha * l_ref[j, rows] + jnp.sum(p, axis=1, keepdims=True)
    acc = acc_ref[j, rows]
    acc_ref[j, rows] = acc * _lane_tile(alpha, acc.shape[1]) + _dot(p, v)
    m_ref[j, rows] = m_new


def _attn_finish(lam_ref, g_ref, l_ref, acc_ref, lambda_init):
    lam = _lambda_value(lam_ref, lambda_init)
    width = acc_ref.shape[-1]
    o = acc_ref[0] / _lane_tile(l_ref[0], width) - lam * (acc_ref[1] / _lane_tile(l_ref[1], width))
    return _rms(o, g_ref[...]) * (1.0 - lambda_init)


def _alibi_slopes():
    slopes = [2.0 ** (-8.0 * (h + 1) / DIFF_HEADS) for h in range(DIFF_HEADS)]
    assert all(math.frexp(s)[0] == 0.5 for s in slopes)
    return slopes


def _attn_prompt_body(iq_ref, ik_ref, slope_ref, q_ref, k_ref, v_ref, kt_ref, lam_ref, g_ref,
                      o_ref, qa_ref, m_ref, l_ref, acc_ref, *, tq, lambda_init):
    hg = pl.program_id(1)
    t = pl.program_id(2)
    iq = iq_ref[t]
    ik = ik_ref[t]
    dh = DIFF_DH
    hw = 2 * dh
    slopes = [slope_ref[hg * ATTN_HEADS_PER_STEP + hh] for hh in range(ATTN_HEADS_PER_STEP)]

    @pl.when(ik == 0)
    def _():
        m_ref[...] = jnp.full(m_ref.shape, -jnp.inf, F32)
        l_ref[...] = jnp.zeros(l_ref.shape, F32)
        acc_ref[...] = jnp.zeros(acc_ref.shape, F32)
        qpos = iq * tq + lax.broadcasted_iota(jnp.int32, (tq, LANES), 0)
        lane = lax.broadcasted_iota(jnp.int32, (tq, LANES), 1)
        qa = (qpos // CHUNK).astype(F32)
        qb = (qpos % CHUNK).astype(F32)
        for hh, slope in enumerate(slopes):
            feat = jnp.where(lane == 0, qa * (-slope * CHUNK),
                             jnp.where(lane == 1, qb * (-slope), jnp.where(lane < 4, slope, 0.0)))
            for j in range(2):
                c0 = hh * hw + j * dh
                qa_ref[hh, j] = jnp.concatenate([q_ref[:, c0:c0 + dh] * (dh ** -0.5), feat], axis=1)

    kt = kt_ref[...]
    ts = tq // ATTN_ROW_SPLIT

    def keys(hh, j, nkeys):
        c0 = hh * hw + j * dh
        return jnp.concatenate([k_ref[0:nkeys, c0:c0 + dh], kt[:nkeys]], axis=1)

    @pl.when(ik < iq)
    def _():
        for hh in range(ATTN_HEADS_PER_STEP):
            v = v_ref[:, hh * hw:(hh + 1) * hw]
            for r in range(ATTN_ROW_SPLIT):
                rows = slice(r * ts, (r + 1) * ts)
                for j in range(2):
                    _softmax_step(j, _dot_nt(qa_ref[hh, j, rows], keys(hh, j, tq)), v,
                                  m_ref.at[hh], l_ref.at[hh], acc_ref.at[hh], rows)

    @pl.when(ik == iq)
    def _():
        outs = []
        for hh, slope in enumerate(slopes):
            for r in range(ATTN_ROW_SPLIT):
                rows = slice(r * ts, (r + 1) * ts)
                nkeys = (r + 1) * ts
                row = r * ts + lax.broadcasted_iota(jnp.int32, (ts, nkeys), 0)
                col = lax.broadcasted_iota(jnp.int32, (ts, nkeys), 1)
                fix = jnp.maximum(col - row, 0).astype(F32) * (-2.0 * slope)
                vis = (col // CHUNK) <= (row // CHUNK)
                v = v_ref[0:nkeys, hh * hw:(hh + 1) * hw]
                for j in range(2):
                    s = _dot_nt(qa_ref[hh, j, rows], keys(hh, j, nkeys))
                    _softmax_step(j, jnp.where(vis, s + fix, -jnp.inf), v,
                                  m_ref.at[hh], l_ref.at[hh], acc_ref.at[hh], rows)
            outs.append(_attn_finish(lam_ref, g_ref.at[hh], l_ref.at[hh], acc_ref.at[hh], lambda_init))
        o_ref[...] = _pack_groups(jnp.concatenate(outs, axis=1))


def _attn_prompt(q, k, v, nb, t, lam, g_subln, lambda_init, tq):
    hw = 2 * DIFF_DH
    assert t % tq == 0 and (tq // ATTN_ROW_SPLIT) % CHUNK == 0 and t // CHUNK <= 256
    nq = t // tq
    pairs = [(iq, ik) for iq in range(nq) for ik in range(iq + 1)]
    iq_tab = jnp.asarray(np.array([p[0] for p in pairs], np.int32))
    ik_tab = jnp.asarray(np.array([p[1] for p in pairs], np.int32))
    slopes = jnp.asarray(np.array(_alibi_slopes(), np.float32))
    pos = np.arange(t)
    ktab = np.zeros((t, LANES), np.float32)
    ktab[:, 0] = 1.0
    ktab[:, 1] = 1.0
    ktab[:, 2] = (pos // CHUNK) * CHUNK
    ktab[:, 3] = pos % CHUNK
    g3 = g_subln.reshape(DIFF_HEADS, 1, hw)
    hps = ATTN_HEADS_PER_STEP
    assert DIFF_HEADS % hps == 0
    bw = hps * hw
    assert bw % PACK_GROUP == 0
    op = _packed_row(bw, _pack_groups)
    return pl.pallas_call(
        functools.partial(_attn_prompt_body, tq=tq, lambda_init=lambda_init),
        grid_spec=pltpu.PrefetchScalarGridSpec(
            num_scalar_prefetch=3,
            grid=(nb, DIFF_HEADS // hps, len(pairs)),
            in_specs=[pl.BlockSpec((tq, bw), lambda b, h, p, iqt, ikt, sl: (b * nq + iqt[p], h)),
                      pl.BlockSpec((tq, bw), lambda b, h, p, iqt, ikt, sl: (b * nq + ikt[p], h)),
                      pl.BlockSpec((tq, bw), lambda b, h, p, iqt, ikt, sl: (b * nq + ikt[p], h)),
                      pl.BlockSpec((tq, LANES), lambda b, h, p, iqt, ikt, sl: (ikt[p], 0)),
                      pl.BlockSpec((4, DIFF_DH), lambda b, h, p, iqt, ikt, sl: (0, 0)),
                      pl.BlockSpec((hps, 1, hw), lambda b, h, p, iqt, ikt, sl: (h, 0, 0))],
            out_specs=pl.BlockSpec((tq, op.shape[1]), lambda b, h, p, iqt, ikt, sl: (b * nq + iqt[p], h)),
            scratch_shapes=[pltpu.VMEM((hps, 2, tq, hw), F32),
                            pltpu.VMEM((hps, 2, tq, LANES), F32),
                            pltpu.VMEM((hps, 2, tq, LANES), F32),
                            pltpu.VMEM((hps, 2, tq, hw), F32)]),
        out_shape=jax.ShapeDtypeStruct((nb * t, op.shape[1] * (DIFF_HEADS // hps)), op.dtype),
        compiler_params=_cparams(("arbitrary", "arbitrary", "arbitrary")),
        name="diff_attn_prompt",
    )(iq_tab, ik_tab, slopes, q, k, v, jnp.asarray(ktab), lam, g3)


def _attn_sample_body(q_ref, kc_hbm, vc_hbm, kn_ref, vn_ref, lam_ref, g_ref,
                      o_ref, kbuf, vbuf, sem, m_ref, l_ref, acc_ref, *, li, tk, past, lambda_init):
    b = pl.program_id(0)
    ik = pl.program_id(1)
    nb = pl.num_programs(0)
    nkb = pl.num_programs(1) - 1
    dh = DIFF_DH
    hw = 2 * dh
    tq = q_ref.shape[0]
    scale = dh ** -0.5

    def fetch_copies(f):
        fb, fk = f // nkb, f % nkb
        slot = f % 2
        cps = []
        for h in range(DIFF_HEADS):
            for hbm, buf, c in ((kc_hbm, kbuf, 0), (vc_hbm, vbuf, 1)):
                cps.append(pltpu.make_async_copy(hbm.at[li, fb, pl.ds(fk * tk, tk), h, :], buf.at[slot, h],
                                                 sem.at[slot, c]))
        return cps

    fcur = b * nkb + ik

    @pl.when(jnp.logical_and(b == 0, ik == 0))
    def _():
        for cp in fetch_copies(0):
            cp.start()

    @pl.when(ik < nkb)
    def _():
        @pl.when(fcur + 1 < nb * nkb)
        def _():
            for cp in fetch_copies(fcur + 1):
                cp.start()

        for cp in fetch_copies(fcur):
            cp.wait()

    @pl.when(ik == 0)
    def _():
        m_ref[...] = jnp.full(m_ref.shape, -jnp.inf, F32)
        l_ref[...] = jnp.zeros(l_ref.shape, F32)
        acc_ref[...] = jnp.zeros(acc_ref.shape, F32)

    def attend(h, k, v, k0, width):
        slope = _alibi_slopes()[h]
        q = q_ref[:, h * hw:(h + 1) * hw]
        qpos = past + lax.broadcasted_iota(jnp.int32, (tq, width), 0)
        kpos = k0 + lax.broadcasted_iota(jnp.int32, (tq, width), 1)
        bias = jnp.abs(qpos - kpos).astype(F32) * (-slope)
        vis = (kpos // CHUNK) <= (qpos // CHUNK)
        for j in range(2):
            s = _dot_nt(q[:, j * dh:(j + 1) * dh], k[:, j * dh:(j + 1) * dh]) * scale + bias
            s = jnp.where(vis, s, -jnp.inf)
            _softmax_step(j, s, v, m_ref.at[h], l_ref.at[h], acc_ref.at[h])

    @pl.when(ik < nkb)
    def _():
        slot = fcur % 2
        for h in range(DIFF_HEADS):
            attend(h, kbuf[slot, h], vbuf[slot, h], ik * tk, tk)

    @pl.when(ik == nkb)
    def _():
        outs = []
        for h in range(DIFF_HEADS):
            hs = slice(h * hw, (h + 1) * hw)
            attend(h, kn_ref[:, hs], vn_ref[:, hs], past, tq)
            outs.append(_attn_finish(lam_ref, g_ref.at[h], l_ref.at[h], acc_ref.at[h], lambda_init))
        o_ref[...] = _pack_groups(jnp.concatenate(outs, axis=1))


def _attn_sample(q, k, v, cache_k, cache_v, li, lam, g_subln, lambda_init, tk):
    n_rows = q.shape[0]
    _, nb, past, nh, hw = cache_k.shape
    aw = nh * hw
    tq = n_rows // nb
    tk = min(tk, past)
    assert past % tk == 0
    nkb = past // tk
    g3 = g_subln.reshape(nh, 1, hw)
    cache_spec = pl.BlockSpec(memory_space=pl.ANY)
    row_spec = pl.BlockSpec((tq, aw), lambda b, ik: (b, 0))
    op = _packed_row(aw, _pack_groups)
    return pl.pallas_call(
        functools.partial(_attn_sample_body, li=li, tk=tk, past=past, lambda_init=lambda_init),
        grid=(nb, nkb + 1),
        in_specs=[row_spec,
                  cache_spec, cache_spec,
                  row_spec,
                  row_spec,
                  pl.BlockSpec((4, DIFF_DH), lambda b, ik: (0, 0)),
                  pl.BlockSpec((nh, 1, hw), lambda b, ik: (0, 0, 0))],
        out_specs=pl.BlockSpec((tq, op.shape[1]), lambda b, ik: (b, 0)),
        scratch_shapes=[pltpu.VMEM((2, nh, tk, hw), F32),
                        pltpu.VMEM((2, nh, tk, hw), F32),
                        pltpu.SemaphoreType.DMA((2, 2)),
                        pltpu.VMEM((nh, 2, tq, LANES), F32),
                        pltpu.VMEM((nh, 2, tq, LANES), F32),
                        pltpu.VMEM((nh, 2, tq, hw), F32)],
        out_shape=jax.ShapeDtypeStruct((n_rows, op.shape[1]), op.dtype),
        compiler_params=_cparams(("arbitrary", "arbitrary")),
        name="diff_attn_sample",
    )(q, cache_k, cache_v, k, v, lam, g3)


def _pad_cols(a, width):
    return jnp.pad(a, ((0, 0), (0, width - a.shape[1])))


def _mixer_constants(dk, dv, w_gate_up, b_gate, g_gla, conv_w, conv_b, dt_bias, a_log, d_skip, g_ssd):
    C = REC_ROWS
    nh = GLA_HEADS
    inner = g_ssd.shape[0]
    n_ssd = inner // SSD_HEADDIM
    wup = jnp.pad(w_gate_up, ((0, LANES - w_gate_up.shape[0]), (0, 0)))
    tri = jnp.asarray(np.tril(np.ones((C, C), np.float32)), BF16)
    e64 = np.zeros((LANES, inner), np.float32)
    ec = np.zeros((LANES, n_ssd * C), np.float32)
    for hh in range(n_ssd):
        e64[hh, hh * SSD_HEADDIM:(hh + 1) * SSD_HEADDIM] = 1.0
        ec[hh, hh * C:(hh + 1) * C] = 1.0
    eye = np.tile(np.eye(C, dtype=np.float32), (1, n_ssd))
    caus = np.tile(np.tril(np.ones((C, C), np.float32)), (1, n_ssd))
    return [wup, b_gate.reshape(1, -1), g_gla.reshape(nh, dv), conv_w, conv_b.reshape(1, -1),
            _pad_cols(dt_bias.reshape(1, -1), LANES),
            _pad_cols(-jnp.exp(a_log.astype(F32)).reshape(1, -1), LANES),
            jnp.repeat(d_skip, SSD_HEADDIM).reshape(1, -1), g_ssd.reshape(1, -1),
            tri, jnp.asarray(e64, BF16), jnp.asarray(ec, BF16), jnp.asarray(eye), jnp.asarray(caus)]


def kernel(x_prompt, x_sample, state_gla, state_ssm, state_conv, cache_k, cache_v, norm_mix, norm_ffn, norm_final, w_in, w_gate_up, b_gate, g_gla, conv_w, conv_b, dt_bias, a_log, d_skip, g_ssd, w_out_mix, w_qkv, lam_q1, lam_k1, lam_q2, lam_k2, g_subln, w_o, router_group_w, router_group_b, router_expert_w, router_expert_b, w1, w3, w2):
    bp, tp, d = x_prompt.shape
    bs, ts, _ = x_sample.shape
    n_p, n_s = bp * tp, bs * ts
    n = n_p + n_s
    depth = norm_mix.shape[0]
    nh = GLA_HEADS
    dk, dv = state_gla.shape[-2], state_gla.shape[-1]
    qk_w, vw = nh * dk, nh * dv
    n_ssd = state_ssm.shape[2]
    inner = n_ssd * SSD_HEADDIM
    hpg = n_ssd // SSD_GROUPS
    gs = SSD_GROUPS * SSD_STATE
    cw = inner + 2 * gs
    rank = w_gate_up.shape[1]
    assert w_gate_up.shape[2] == qk_w and rank <= LANES and n_ssd <= LANES

    x = jnp.concatenate([x_prompt.reshape(n_p, d), x_sample.reshape(n_s, d)], axis=0)
    tm_row = next(tm for tm in (1408, 1056, 768, 256) if n % tm == 0)

    i_even = i_odd = 0
    gla_p, ssm_p, conv_p, gla_s, ssm_s, conv_s = [], [], [], [], [], []
    k_p, v_p, k_s, v_s = [], [], [], []
    y_p = y_s = None
    for layer in range(depth):
        final = layer == depth - 1
        if layer % 2 == 0:
            i = i_even
            i_even += 1
            offs = np.cumsum([0, qk_w, qk_w, vw, rank, vw, inner, cw, n_ssd])
            w_t = w_in[i].T
            seg = lambda j: w_t[offs[j]:offs[j + 1]]
            pad_rows = lambda a: jnp.pad(a, ((0, LANES - a.shape[0]), (0, 0)))
            w_cat = jnp.concatenate([seg(0), seg(1), seg(2), seg(4), seg(5), seg(6),
                                     pad_rows(seg(3)), pad_rows(seg(7))], axis=0).astype(BF16)
            proj = _norm_matmul(x, norm_mix[layer], w_cat, tm_row, 1280, w_is_transposed=True)
            consts = _mixer_constants(dk, dv, w_gate_up[i], b_gate[i], g_gla[i], conv_w[i], conv_b[i],
                                      dt_bias[i], a_log[i], d_skip[i], g_ssd[i])

            def to_group_state(sm):
                b_ = sm.shape[0]
                return sm.reshape(b_, SSD_GROUPS, hpg, SSD_STATE, SSD_HEADDIM).transpose(0, 1, 3, 2, 4) \
                         .reshape(b_, SSD_GROUPS, SSD_STATE, hpg * SSD_HEADDIM)

            def from_group_state(sg):
                b_ = sg.shape[0]
                return sg.reshape(b_, SSD_GROUPS, SSD_STATE, hpg, SSD_HEADDIM).transpose(0, 1, 3, 2, 4) \
                         .reshape(b_, n_ssd, SSD_STATE, SSD_HEADDIM)

            def pad_conv(cv):
                return jnp.pad(cv, ((0, 0), (CONV_PAD - cv.shape[1], 0), (0, 0)))

            mix_a, g_fin, s_fin, c_fin = _mixer_call(
                proj, 0, bp, tp, jnp.zeros((bp, nh, dv, dk), F32),
                jnp.zeros((bp, SSD_GROUPS, SSD_STATE, hpg * SSD_HEADDIM), F32),
                jnp.zeros((bp, CONV_PAD, cw), F32), consts, dk, dv)
            gla_p.append(g_fin.transpose(0, 1, 3, 2))
            ssm_p.append(from_group_state(s_fin))
            conv_p.append(c_fin[:, CONV_PAD - (SSD_CONV - 1):])
            mix_b, g_fin, s_fin, c_fin = _mixer_call(
                proj, n_p, bs, ts, state_gla[i].transpose(0, 1, 3, 2),
                to_group_state(state_ssm[i]), pad_conv(state_conv[i]), consts, dk, dv)
            gla_s.append(g_fin.transpose(0, 1, 3, 2))
            ssm_s.append(from_group_state(s_fin))
            conv_s.append(c_fin[:, CONV_PAD - (SSD_CONV - 1):])
            w_proj = w_out_mix[i]
        else:
            i = i_odd
            i_odd += 1
            lambda_init = 0.8 - 0.6 * math.exp(-0.3 * layer)
            tm_p = next(tm for tm in (1024, 512, 256) if n_p % tm == 0)
            w_qkv_b = w_qkv[i].astype(BF16)
            q_a, k_a, v_a = _norm_matmul(x, norm_mix[layer], w_qkv_b, tm_p, 1024, nrows=n_p, parts=3)
            q_b, k_b, v_b = _norm_matmul(x, norm_mix[layer], w_qkv_b, n_s, 1024, row0=n_p, parts=3)
            lam = jnp.stack([lam_q1[i], lam_k1[i], lam_q2[i], lam_k2[i]])
            mix_a = _attn_prompt(q_a, k_a, v_a, bp, tp, lam, g_subln[i], lambda_init, min(512, tp))
            mix_b = _attn_sample(q_b, k_b, v_b, cache_k, cache_v, i, lam, g_subln[i], lambda_init, 1024)
            k_p.append(k_a.reshape(bp, tp, DIFF_HEADS, 2 * DIFF_DH))
            v_p.append(v_a.reshape(bp, tp, DIFF_HEADS, 2 * DIFF_DH))
            k_s.append(k_b.reshape(bs, ts, DIFF_HEADS, 2 * DIFF_DH))
            v_s.append(v_b.reshape(bs, ts, DIFF_HEADS, 2 * DIFF_DH))
            w_proj = w_o[i]

        wr = _pad_cols(jnp.concatenate([router_group_w[layer], router_expert_w[layer]], axis=1), LANES)
        rbias = _pad_cols(jnp.concatenate([router_group_b[layer], router_expert_b[layer]]).reshape(1, -1), LANES)
        x1, h2, logits = _proj_resid(mix_a, mix_b, w_proj, x, norm_ffn[layer], wr, 256)
        res = _moe(x1, h2, logits, rbias, w1, w3, w2, layer, norm_final, n_p, final)
        if final:
            y_p, y_s = res
        else:
            x = res[0]

    return (y_p.reshape(bp, tp, d), y_s.reshape(bs, ts, d),
            jnp.stack(gla_p), jnp.stack(ssm_p), jnp.stack(conv_p), jnp.stack(k_p), jnp.stack(v_p),
            jnp.stack(gla_s), jnp.stack(ssm_s), jnp.stack(conv_s), jnp.stack(k_s), jnp.stack(v_s))
```

```python
import functools
import math

import numpy as np
import jax
import jax.numpy as jnp
from jax import lax
from jax.experimental import pallas as pl
from jax.experimental.pallas import tpu as pltpu

F32 = jnp.float32
BF16 = jnp.bfloat16

EPS = 1e-6
CHUNK = 64
GLA_HEADS = 4
GLA_TAU = 16.0
SSD_HEADDIM = 64
SSD_STATE = 128
SSD_GROUPS = 4
SSD_CONV = 4
DIFF_HEADS = 8
DIFF_DH = 128
MOE_GROUPS = 4
MOE_PER_GROUP = 8
N_EXPERTS = MOE_GROUPS * MOE_PER_GROUP

LANES = 128
SUBLANES = 8
REC_ROWS = 128
STRIP = 16
CONV_PAD = 8
ATTN_ROW_SPLIT = 2
ATTN_HEADS_PER_STEP = 4
VMEM_LIMIT = 56 * 1024 * 1024


def _cparams(sem, vmem=VMEM_LIMIT):
    return pltpu.CompilerParams(dimension_semantics=sem, vmem_limit_bytes=vmem)


def _dot(a, b):
    return jnp.dot(a, b, preferred_element_type=F32)


def _dot_nt(a, b):
    return lax.dot_general(a, b, (((1,), (1,)), ((), ())), preferred_element_type=F32)


def _dot_tn(a, b):
    return lax.dot_general(a, b, (((0,), (0,)), ((), ())), preferred_element_type=F32)


def _split_hi_lo(a):
    hi = a.astype(BF16)
    lo = (a - hi.astype(F32)).astype(BF16)
    return hi, lo


def _exact_left(m, a):
    hi, lo = _split_hi_lo(a)
    return _dot(m, hi) + _dot(m, lo)


def _exact_right(a, m):
    hi, lo = _split_hi_lo(a)
    return _dot(hi, m) + _dot(lo, m)


def _silu(x):
    h = 0.5 * x
    return h * (1.0 + jnp.tanh(h))


def _log1p_exp_neg_abs(x):
    e = jnp.exp(-jnp.abs(x))
    u = 1.0 + e
    return jnp.where(u == 1.0, e, jnp.log(u) * (e / (u - 1.0)))


def _rms(x, g):
    return x * lax.rsqrt(jnp.mean(x * x, axis=-1, keepdims=True) + EPS) * g


def _pack_halves(x):
    w = x.shape[1] // 2
    lo = lax.bitcast_convert_type(x[:, :w].astype(BF16).astype(F32), jnp.uint32)
    hi = lax.bitcast_convert_type(x[:, w:].astype(BF16).astype(F32), jnp.uint32)
    return (lo >> 16) | (hi & jnp.uint32(0xFFFF0000))


def _unpack_halves(p):
    lo = lax.bitcast_convert_type(p << 16, F32)
    hi = lax.bitcast_convert_type(p & jnp.uint32(0xFFFF0000), F32)
    return jnp.concatenate([lo, hi], axis=1)


PACK_GROUP = 512


def _pack_groups(x):
    return jnp.concatenate([_pack_halves(x[:, c:c + PACK_GROUP]) for c in range(0, x.shape[1], PACK_GROUP)],
                           axis=1)


def _unpack_groups(p):
    step = _packed_row(PACK_GROUP).shape[1]
    return jnp.concatenate([_unpack_halves(p[:, c:c + step]) for c in range(0, p.shape[1], step)], axis=1)


def _packed_row(d, fn=None):
    return jax.eval_shape(fn or _pack_halves, jax.ShapeDtypeStruct((SUBLANES, d), F32))


def _norm_matmul_body(x_ref, g_ref, w_ref, *refs, w_is_transposed, parts):
    o_refs, h_ref = refs[:parts], refs[parts]
    j = pl.program_id(1)

    @pl.when(j == 0)
    def _():
        h_ref[...] = _rms(x_ref[...], g_ref[...]).astype(h_ref.dtype)

    res = (_dot_nt if w_is_transposed else _dot)(h_ref[...], w_ref[...])
    if parts == 1:
        o_refs[0][...] = res
    else:
        per = pl.num_programs(1) // parts
        for p in range(parts):
            @pl.when(j // per == p)
            def _(p=p):
                o_refs[p][...] = res


def _norm_matmul(x, g, w, tm, tn, *, row0=0, nrows=None, parts=1, w_is_transposed=False):
    n, d = x.shape
    nrows = n - row0 if nrows is None else nrows
    nout = w.shape[0] if w_is_transposed else w.shape[1]
    assert nrows % tm == 0 and row0 % tm == 0 and nout % (tn * parts) == 0
    blk0 = row0 // tm
    per = nout // tn // parts
    w_spec = (pl.BlockSpec((tn, d), lambda i, j: (j, 0)) if w_is_transposed
              else pl.BlockSpec((d, tn), lambda i, j: (0, j)))
    out_specs = [pl.BlockSpec((tm, tn), lambda i, j, p=p: (i, jnp.clip(j - p * per, 0, per - 1)))
                 for p in range(parts)]
    outs = pl.pallas_call(
        functools.partial(_norm_matmul_body, w_is_transposed=w_is_transposed, parts=parts),
        grid=(nrows // tm, nout // tn),
        in_specs=[pl.BlockSpec((tm, d), lambda i, j: (blk0 + i, 0), pipeline_mode=pl.Buffered(1)),
                  pl.BlockSpec((1, d), lambda i, j: (0, 0)),
                  w_spec],
        out_specs=out_specs,
        out_shape=[jax.ShapeDtypeStruct((nrows, nout // parts), F32)] * parts,
        scratch_shapes=[pltpu.VMEM((tm, d), w.dtype)],
        compiler_params=_cparams(("arbitrary", "arbitrary")),
        name="norm_matmul",
    )(x, g.reshape(1, d), w)
    return outs[0] if parts == 1 else outs


def _mixer_body(p_ref, gla0_ref, ssm0_ref, conv0_ref, wup_ref, bgate_ref, ggla_ref, convw_ref,
                convb_ref, dtb_ref, aneg_ref, dskip_ref, gssd_ref, tri_ref, e64_ref, ec_ref,
                eye_ref, caus_ref,
                o_ref, gla_ref, ssm_ref, ctail_ref,
                sg_ref, ss_ref, ext_ref, b_ref, *, rb, dk, dv):
    C = REC_ROWS
    c = pl.program_id(1)
    nc = pl.num_programs(1)
    nh = GLA_HEADS
    qk_w = nh * dk
    vw = nh * dv
    inner = vw
    gs = SSD_GROUPS * SSD_STATE
    o_q, o_k, o_v = 0, qk_w, 2 * qk_w
    o_r = o_v + vw
    o_z = o_r + vw
    o_x = o_z + inner
    o_g = o_x + inner + 2 * gs
    o_dt = o_g + LANES

    @pl.when(c == 0)
    def _init():
        sg_ref[...] = gla0_ref[...]
        ss_ref[...] = ssm0_ref[...]
        ext_ref[0:CONV_PAD, :] = conv0_ref[...]

    p = p_ref[...]
    if rb < C:
        p = jnp.concatenate([p, jnp.zeros((C - rb, p.shape[1]), F32)], axis=0)

    def rowmask(width):
        return lax.broadcasted_iota(jnp.int32, (C, width), 0) < rb

    q = p[:, o_q:o_q + qk_w] * (dk ** -0.5)
    k = p[:, o_k:o_k + qk_w]
    v = p[:, o_v:o_v + vw]
    r = p[:, o_r:o_r + vw]
    z = p[:, o_z:o_z + inner]
    xbc = p[:, o_x:o_x + inner + 2 * gs]
    glr = p[:, o_g:o_g + LANES]
    dtp = p[:, o_dt:o_dt + LANES]

    tri = tri_ref[...]

    zg = _dot(glr, wup_ref[...]) + bgate_ref[...]
    la = (jnp.minimum(zg, 0.0) - _log1p_exp_neg_abs(zg)) * (1.0 / GLA_TAU)
    if rb < C:
        la = jnp.where(rowmask(qk_w), la, 0.0)
    bcum = _exact_left(tri, la)
    b_ref[...] = bcum

    row_i = lax.broadcasted_iota(jnp.int32, (C, dk), 0)
    prow = lax.broadcasted_iota(jnp.int32, (STRIP, C), 0)
    pcol = lax.broadcasted_iota(jnp.int32, (STRIP, C), 1)
    o_heads = []
    for h in range(nh):
        hs = slice(h * dk, (h + 1) * dk)
        vs = slice(h * dv, (h + 1) * dv)
        bh = bcum[:, hs]
        qh = q[:, hs]
        kh = k[:, hs]
        vh = v[:, vs]
        strips = []
        for i in range(C // STRIP):
            r0 = i * STRIP
            if i == 0:
                ref_row = jnp.zeros((1, dk), F32)
            else:
                ref_row = b_ref[pl.ds(r0 - 1, 1), hs]
            q_i = qh[r0:r0 + STRIP] * jnp.exp(bh[r0:r0 + STRIP] - ref_row)
            e = jnp.where(row_i < r0 + STRIP, ref_row - bh, 0.0)
            k_i = kh * jnp.exp(e)
            s_i = _dot_nt(q_i, k_i)
            strips.append(jnp.where(pcol <= prow + r0, s_i, 0.0))
        pmat = jnp.concatenate(strips, axis=0)
        b_last = b_ref[pl.ds(C - 1, 1), hs]
        st = sg_ref[h]
        o_h = _dot(pmat, vh) + _dot_nt(qh * jnp.exp(bh), st)
        k_st = kh * jnp.exp(b_last - bh)
        sg_ref[h] = st * jnp.exp(b_last) + _dot_tn(vh, k_st)
        o_h = _rms(o_h, ggla_ref[pl.ds(h, 1), :]) * _silu(r[:, vs])
        o_heads.append(o_h)
    o_a = jnp.concatenate(o_heads, axis=1)

    cw = inner + 2 * gs
    ext_ref[CONV_PAD:CONV_PAD + C, :] = xbc
    conv = convb_ref[...]
    for j in range(SSD_CONV):
        conv = conv + convw_ref[pl.ds(j, 1), :] * ext_ref[pl.ds(CONV_PAD - (SSD_CONV - 1) + j, C), :]
    xc = _silu(conv)
    xs = xc[:, :inner]
    bm = xc[:, inner:inner + gs]
    cm = xc[:, inner + gs:cw]
    dtv = dtp + dtb_ref[...]
    dt = jnp.maximum(dtv, 0.0) + _log1p_exp_neg_abs(dtv)
    ld = dt * aneg_ref[...]
    if rb < C:
        dt = jnp.where(rowmask(LANES), dt, 0.0)
        ld = jnp.where(rowmask(LANES), ld, 0.0)
    bs = _exact_left(tri, ld)
    e64 = e64_ref[...]
    dt64 = _exact_right(dt, e64)
    b64 = _exact_right(bs, e64)
    blast64 = b64[C - 1:C, :]
    xdt = xs * dt64
    bc = _exact_right(bs, ec_ref[...])
    br = jnp.sum(eye_ref[...] * bc, axis=0, keepdims=True)
    vis = caus_ref[...] > 0.0
    lf = jnp.where(vis, jnp.exp(jnp.where(vis, bc - br, 0.0)), 0.0)
    hpg = inner // SSD_HEADDIM // SSD_GROUPS
    gw = hpg * SSD_HEADDIM
    lane_head = lax.broadcasted_iota(jnp.int32, (C, gw), 1) // SSD_HEADDIM
    y_groups = []
    for g in range(SSD_GROUPS):
        cm_g = cm[:, g * SSD_STATE:(g + 1) * SSD_STATE]
        bm_g = bm[:, g * SSD_STATE:(g + 1) * SSD_STATE]
        gl = slice(g * gw, (g + 1) * gw)
        gmat = _dot_nt(cm_g, bm_g)
        a4 = jnp.concatenate([gmat] * hpg, axis=1) * lf[:, g * hpg * C:(g + 1) * hpg * C]
        xg = xdt[:, gl]
        x4 = jnp.concatenate([jnp.where(lane_head == hh, xg, 0.0) for hh in range(hpg)], axis=0)
        s_g = ss_ref[g]
        y_g = _dot(a4, x4) + _dot(cm_g, s_g) * jnp.exp(b64[:, gl])
        xw = xg * jnp.exp(blast64[:, gl] - b64[:, gl])
        ss_ref[g] = s_g * jnp.exp(blast64[:, gl]) + _dot_tn(bm_g, xw)
        y_groups.append(y_g)
    y = jnp.concatenate(y_groups, axis=1) + xs * dskip_ref[...]
    y = _rms(y * _silu(z), gssd_ref[...])

    o_full = jnp.concatenate([o_a, y], axis=1)
    o_ref[...] = _pack_groups(o_full[:rb])

    @pl.when(c == nc - 1)
    def _fin():
        gla_ref[...] = sg_ref[...]
        ssm_ref[...] = ss_ref[...]
        ctail_ref[...] = ext_ref[pl.ds(rb, CONV_PAD), :]

    ext_ref[0:CONV_PAD, :] = ext_ref[pl.ds(rb, CONV_PAD), :]


def _mixer_call(proj, row_off, nb, t, gla0, ssm0, conv0, consts, dk, dv):
    C = REC_ROWS
    rb = min(t, C)
    assert t % rb == 0 and row_off % rb == 0
    steps = t // rb
    width = proj.shape[1]
    nh = GLA_HEADS
    vw = nh * dv
    cw = conv0.shape[-1]
    blk0 = row_off // rb

    def full(a):
        nd = a.ndim
        return pl.BlockSpec(a.shape, lambda b, c, _n=nd: (0,) * _n)

    in_specs = [pl.BlockSpec((rb, width), lambda b, c: (blk0 + b * steps + c, 0)),
                pl.BlockSpec((None, nh, dv, dk), lambda b, c: (b, 0, 0, 0)),
                pl.BlockSpec((None, SSD_GROUPS, SSD_STATE, ssm0.shape[-1]), lambda b, c: (b, 0, 0, 0)),
                pl.BlockSpec((None, CONV_PAD, cw), lambda b, c: (b, 0, 0))]
    in_specs += [full(a) for a in consts]
    operands = [proj, gla0, ssm0, conv0, *consts]
    op = _packed_row(2 * vw, _pack_groups)
    out_specs = [pl.BlockSpec((rb, op.shape[1]), lambda b, c: (b * steps + c, 0)),
                 pl.BlockSpec((None, nh, dv, dk), lambda b, c: (b, 0, 0, 0)),
                 pl.BlockSpec((None, SSD_GROUPS, SSD_STATE, ssm0.shape[-1]), lambda b, c: (b, 0, 0, 0)),
                 pl.BlockSpec((None, CONV_PAD, cw), lambda b, c: (b, 0, 0))]
    out_shape = [jax.ShapeDtypeStruct((nb * t, op.shape[1]), op.dtype),
                 jax.ShapeDtypeStruct(gla0.shape, F32),
                 jax.ShapeDtypeStruct(ssm0.shape, F32),
                 jax.ShapeDtypeStruct((nb, CONV_PAD, cw), F32)]
    return pl.pallas_call(
        functools.partial(_mixer_body, rb=rb, dk=dk, dv=dv),
        grid=(nb, steps),
        in_specs=in_specs,
        out_specs=out_specs,
        out_shape=out_shape,
        scratch_shapes=[pltpu.VMEM((nh, dv, dk), F32),
                        pltpu.VMEM((SSD_GROUPS, SSD_STATE, ssm0.shape[-1]), F32),
                        pltpu.VMEM((C + CONV_PAD, cw), F32),
                        pltpu.VMEM((C, nh * dk), F32)],
        compiler_params=_cparams(("arbitrary", "arbitrary")),
        name="gla_ssd_mixer",
    )(*operands)


def _proj_resid_body(oa_ref, ob_ref, w_ref, x_ref, g_ref, wr_ref, x1_ref, h_ref, lg_ref, *, n_first):
    o = _unpack_groups(jnp.where(pl.program_id(0) < n_first, oa_ref[...], ob_ref[...]))
    x1 = x_ref[...] + _dot(o, w_ref[...])
    x1_ref[...] = x1
    h = _rms(x1, g_ref[...])
    h_ref[...] = _pack_halves(h)
    lg_ref[...] = _dot(h, wr_ref[...])


def _proj_resid(o_a, o_b, w, x, g, wr, tm):
    n, d = x.shape
    kin = o_a.shape[1]
    d_in = w.shape[0]
    assert o_a.shape[0] % tm == 0 and o_b.shape[0] % tm == 0
    n_first = o_a.shape[0] // tm
    hp = _packed_row(d)
    return pl.pallas_call(
        functools.partial(_proj_resid_body, n_first=n_first),
        grid=(n // tm,),
        in_specs=[pl.BlockSpec((tm, kin), lambda i: (jnp.minimum(i, n_first - 1), 0)),
                  pl.BlockSpec((tm, kin), lambda i: (jnp.maximum(i - n_first, 0), 0)),
                  pl.BlockSpec((d_in, d), lambda i: (0, 0), pipeline_mode=pl.Buffered(1)),
                  pl.BlockSpec((tm, d), lambda i: (i, 0)),
                  pl.BlockSpec((1, d), lambda i: (0, 0)),
                  pl.BlockSpec((d, LANES), lambda i: (0, 0))],
        out_specs=[pl.BlockSpec((tm, d), lambda i: (i, 0)),
                   pl.BlockSpec((tm, hp.shape[1]), lambda i: (i, 0)),
                   pl.BlockSpec((tm, LANES), lambda i: (i, 0))],
        out_shape=[jax.ShapeDtypeStruct((n, d), F32),
                   jax.ShapeDtypeStruct((n, hp.shape[1]), hp.dtype),
                   jax.ShapeDtypeStruct((n, LANES), F32)],
        compiler_params=_cparams(("arbitrary",)),
        name="proj_resid_norm_router",
    )(o_a, o_b, w, x, g.reshape(1, d), wr)


_META_E0, _META_E1, _META_G0, _META_G1, _META_R0, _META_R1 = range(6)
_EXP_LANE0 = MOE_GROUPS


def _route_body(lg_ref, bias_ref, ltri_ref, eye_ref, meta_ref, idx_ref, cnt_ref, base_ref):
    i = pl.program_id(0)

    @pl.when(i == 0)
    def _():
        base_ref[...] = jnp.zeros_like(base_ref)

    lg = lg_ref[...] + bias_ref[...]
    tm = lg.shape[0]
    lane_i = lax.broadcasted_iota(jnp.int32, (tm, LANES), 1)
    lane = lane_i.astype(F32)
    neg = -jnp.inf
    glog = jnp.where(lane_i < MOE_GROUPS, lg, neg)
    gmax = jnp.max(glog, axis=1, keepdims=True)
    gsel = jnp.min(jnp.where(glog == gmax, lane, float(LANES)), axis=1, keepdims=True)
    pg = 1.0 / jnp.sum(jnp.exp(glog - gmax), axis=1, keepdims=True)
    lo = _EXP_LANE0 + MOE_PER_GROUP * gsel
    el = jnp.where((lane >= lo) & (lane < lo + MOE_PER_GROUP), lg, neg)
    v1 = jnp.max(el, axis=1, keepdims=True)
    i1 = jnp.min(jnp.where(el == v1, lane, float(LANES)), axis=1, keepdims=True)
    el2 = jnp.where(lane == i1, neg, el)
    v2 = jnp.max(el2, axis=1, keepdims=True)
    i2 = jnp.min(jnp.where(el2 == v2, lane, float(LANES)), axis=1, keepdims=True)
    e = jnp.exp(v2 - v1)
    g1 = pg / (1.0 + e)
    g2 = pg * e / (1.0 + e)
    hot1 = lane == i1
    hot2 = lane == i2
    onehot = jnp.where(hot1 | hot2, 1.0, 0.0)
    before = _dot(ltri_ref[...], onehot.astype(BF16)) + base_ref[0:1, :]
    r1 = jnp.sum(jnp.where(hot1, before, 0.0), axis=1, keepdims=True)
    r2 = jnp.sum(jnp.where(hot2, before, 0.0), axis=1, keepdims=True)
    base_ref[0:1, :] = base_ref[0:1, :] + jnp.sum(onehot, axis=0, keepdims=True)
    meta = jnp.zeros((tm, LANES), F32)
    for idx, val in ((_META_E0, i1 - _EXP_LANE0), (_META_E1, i2 - _EXP_LANE0),
                     (_META_G0, g1), (_META_G1, g2), (_META_R0, r1), (_META_R1, r2)):
        meta = jnp.where(lane_i == idx, val, meta)
    meta_ref[...] = meta

    eye = eye_ref[...]
    rows = [jnp.sum(eye * col, axis=0, keepdims=True)
            for col in (i1 - _EXP_LANE0, i2 - _EXP_LANE0, r1, r2)]
    idx_ref[...] = jnp.concatenate(rows + [jnp.zeros((SUBLANES - len(rows), tm), F32)], axis=0)

    @pl.when(i == pl.num_programs(0) - 1)
    def _():
        cnt_ref[...] = base_ref[...]


def _route(logits, bias, tm):
    n = logits.shape[0]
    ltri = jnp.asarray(np.tril(np.ones((tm, tm), np.float32), -1), BF16)
    eye = jnp.asarray(np.eye(tm, dtype=np.float32))
    return pl.pallas_call(
        _route_body,
        grid=(n // tm,),
        in_specs=[pl.BlockSpec((tm, LANES), lambda i: (i, 0)),
                  pl.BlockSpec((1, LANES), lambda i: (0, 0)),
                  pl.BlockSpec((tm, tm), lambda i: (0, 0)),
                  pl.BlockSpec((tm, tm), lambda i: (0, 0))],
        out_specs=[pl.BlockSpec((tm, LANES), lambda i: (i, 0)),
                   pl.BlockSpec((None, SUBLANES, tm), lambda i: (i, 0, 0)),
                   pl.BlockSpec((8, LANES), lambda i: (0, 0))],
        out_shape=[jax.ShapeDtypeStruct((n, LANES), F32),
                   jax.ShapeDtypeStruct((n // tm, SUBLANES, tm), F32),
                   jax.ShapeDtypeStruct((8, LANES), F32)],
        scratch_shapes=[pltpu.VMEM((8, LANES), F32)],
        compiler_params=_cparams(("arbitrary",)),
        name="moe_route",
    )(logits, bias, ltri, eye)


def _slot_of(eid_ref, rank_ref, pstart_ref, idx):
    return pstart_ref[eid_ref[idx]] + rank_ref[idx]


def _dispatch_body(eid_ref, rank_ref, pstart_ref, pad0_ref, npad_ref, nb_ref, h_ref, xs_ref, slot_ref,
                   zbuf, sem, zsem, *, tm, tm_e, nblk):
    i = pl.program_id(0)
    n_tok = pl.num_programs(0) * tm

    def pad_fill(wait):
        def go(cp):
            if wait:
                cp.wait()
            else:
                cp.start()

        def body(e, carry):
            off = pad0_ref[e]
            npad = npad_ref[e]
            head = (SUBLANES - off % SUBLANES) % SUBLANES
            for j in range(SUBLANES - 1):
                @pl.when(j < head)
                def _(j=j):
                    go(pltpu.make_async_copy(zbuf.at[pl.ds(0, 1), :], xs_ref.at[pl.ds(off + j, 1), :], zsem))
            off = pl.multiple_of(off + head, SUBLANES)
            rem = npad - head
            bit = pl.next_power_of_2(tm_e) // 2
            while bit >= SUBLANES:
                on = (rem & bit) != 0

                @pl.when(on)
                def _(off=off, bit=bit):
                    go(pltpu.make_async_copy(zbuf.at[pl.ds(0, bit), :], xs_ref.at[pl.ds(off, bit), :], zsem))

                off = pl.multiple_of(off + jnp.where(on, bit, 0), SUBLANES)
                bit //= 2
            return carry
        lax.fori_loop(0, N_EXPERTS, body, 0)

        def tail(tb, carry):
            go(pltpu.make_async_copy(zbuf, xs_ref.at[pl.ds(pl.multiple_of(tb * tm_e, tm_e), tm_e), :], zsem))
            return carry
        lax.fori_loop(nb_ref[0], nblk, tail, 0)

    @pl.when(i == 0)
    def _():
        zbuf[...] = jnp.zeros(zbuf.shape, zbuf.dtype)
        pad_fill(False)

    for r in range(tm):
        for kk in range(2):
            idx = kk * n_tok + i * tm + r
            d = _slot_of(eid_ref, rank_ref, pstart_ref, idx)
            slot_ref[idx] = d
            pltpu.make_async_copy(h_ref.at[pl.ds(r, 1), :], xs_ref.at[pl.ds(d, 1), :], sem).start(
                priority=(r + kk) % 2)

    @pl.when(i == 0)
    def _():
        pad_fill(True)

    for _ in range(2 * tm):
        pltpu.make_async_copy(h_ref.at[pl.ds(0, 1), :], xs_ref.at[pl.ds(0, 1), :], sem).wait()


def _dispatch(h, eid, rank, p_start, pad0, npad, nb_used, nblk, tm, tm_e):
    n, d = h.shape
    return pl.pallas_call(
        functools.partial(_dispatch_body, tm=tm, tm_e=tm_e, nblk=nblk),
        grid_spec=pltpu.PrefetchScalarGridSpec(
            num_scalar_prefetch=6,
            grid=(n // tm,),
            in_specs=[pl.BlockSpec((tm, d), lambda i, *_: (i, 0))],
            out_specs=[pl.BlockSpec(memory_space=pl.ANY),
                       pl.BlockSpec(memory_space=pltpu.SMEM)],
            scratch_shapes=[pltpu.VMEM((tm_e, d), h.dtype),
                            pltpu.SemaphoreType.DMA(()),
                            pltpu.SemaphoreType.DMA(())]),
        out_shape=[jax.ShapeDtypeStruct((nblk * tm_e, d), h.dtype),
                   jax.ShapeDtypeStruct(eid.shape, jnp.int32)],
        compiler_params=_cparams(("arbitrary",)),
        name="moe_dispatch",
    )(eid, rank, p_start, pad0, npad, nb_used, h)


def _experts_body(be_ref, nb_ref, first_ref, nxt_ref, slot_ref, x_ref, w1_hbm, w3_hbm, w2_hbm, y_ref,
                  wb1, wb3, wb2, sem, *, layer):
    b = pl.program_id(0)
    used = b < nb_ref[0]
    s = slot_ref[b]

    def weight_copies(e, slot):
        return (pltpu.make_async_copy(w1_hbm.at[layer, e], wb1.at[slot], sem.at[slot, 0]),
                pltpu.make_async_copy(w3_hbm.at[layer, e], wb3.at[slot], sem.at[slot, 1]),
                pltpu.make_async_copy(w2_hbm.at[layer, e], wb2.at[slot], sem.at[slot, 2]))

    @pl.when(b == 0)
    def _():
        for cp in weight_copies(be_ref[0], 0):
            cp.start()

    @pl.when(jnp.logical_and(used, first_ref[b] == 1))
    def _():
        for cp in weight_copies(be_ref[b], s):
            cp.wait()

        @pl.when(nxt_ref[b] >= 0)
        def _():
            for cp in weight_copies(nxt_ref[b], 1 - s):
                cp.start()

    @pl.when(used)
    def _():
        x = _unpack_halves(x_ref[...])
        a = _dot(x, wb1[s])
        g = _dot(x, wb3[s])
        y_ref[...] = _pack_halves(_dot(_silu(a) * g, wb2[s]))

    @pl.when(jnp.logical_not(used))
    def _():
        y_ref[...] = jnp.zeros(y_ref.shape, y_ref.dtype)


def _experts(xs, block_e, nb_used, first, nxt_e, slot, w1, w3, w2, layer, tm):
    s, pw = xs.shape
    d, f = w1.shape[-2], w1.shape[-1]
    nblk = s // tm
    return pl.pallas_call(
        functools.partial(_experts_body, layer=layer),
        grid_spec=pltpu.PrefetchScalarGridSpec(
            num_scalar_prefetch=5,
            grid=(nblk,),
            in_specs=[pl.BlockSpec((tm, pw), lambda b, be, nb, *_: (jnp.minimum(b, nb[0] - 1), 0)),
                      pl.BlockSpec(memory_space=pl.ANY),
                      pl.BlockSpec(memory_space=pl.ANY),
                      pl.BlockSpec(memory_space=pl.ANY)],
            out_specs=pl.BlockSpec((tm, pw), lambda b, *_: (b, 0)),
            scratch_shapes=[pltpu.VMEM((2, d, f), F32),
                            pltpu.VMEM((2, d, f), F32),
                            pltpu.VMEM((2, f, d), F32),
                            pltpu.SemaphoreType.DMA((2, 3))]),
        out_shape=jax.ShapeDtypeStruct((s, pw), xs.dtype),
        compiler_params=_cparams(("arbitrary",)),
        name="moe_experts",
    )(block_e, nb_used, first, nxt_e, slot, xs, w1, w3, w2)


def _combine_body(slot_ref, x_ref, meta_ref, gfin_ref, ys_ref, *rest, tm, n_first, final):
    if final:
        o_a_ref, o_b_ref, buf, sem = rest
    else:
        o_a_ref, buf, sem = rest
        o_b_ref = None
    i = pl.program_id(0)
    nsteps = pl.num_programs(0)

    def issue(step, slot):
        for r in range(tm):
            for kk in range(2):
                d = slot_ref[kk * (nsteps * tm) + step * tm + r]
                pltpu.make_async_copy(ys_ref.at[pl.ds(d, 1), :], buf.at[slot, kk, pl.ds(r, 1), :],
                                      sem.at[slot]).start(priority=(r + kk) % 2)

    @pl.when(i == 0)
    def _():
        issue(0, 0)

    @pl.when(i + 1 < nsteps)
    def _():
        issue(i + 1, (i + 1) % 2)

    slot = i % 2
    for _ in range(2 * tm):
        pltpu.make_async_copy(ys_ref.at[pl.ds(0, 1), :], buf.at[slot, 0, pl.ds(0, 1), :], sem.at[slot]).wait()

    meta = meta_ref[...]
    g0 = meta[:, _META_G0:_META_G0 + 1]
    g1 = meta[:, _META_G1:_META_G1 + 1]
    out = x_ref[...] + (_unpack_halves(buf[slot, 0]) * g0 + _unpack_halves(buf[slot, 1]) * g1)
    if not final:
        o_a_ref[...] = out
    else:
        out = _rms(out, gfin_ref[...])

        @pl.when(i < n_first)
        def _():
            o_a_ref[...] = out

        @pl.when(i >= n_first)
        def _():
            o_b_ref[...] = out


def _combine(x, meta, slots, ys, gfin, tm, n_first_rows, final):
    n, d = x.shape
    n_first = n_first_rows // tm
    if final:
        out_specs = [pl.BlockSpec((tm, d), lambda i, *_: (jnp.minimum(i, n_first - 1), 0)),
                     pl.BlockSpec((tm, d), lambda i, *_: (jnp.maximum(i - n_first, 0), 0))]
        out_shape = [jax.ShapeDtypeStruct((n_first_rows, d), F32),
                     jax.ShapeDtypeStruct((n - n_first_rows, d), F32)]
    else:
        out_specs = [pl.BlockSpec((tm, d), lambda i, *_: (i, 0))]
        out_shape = [jax.ShapeDtypeStruct((n, d), F32)]
    return pl.pallas_call(
        functools.partial(_combine_body, tm=tm, n_first=n_first, final=final),
        grid_spec=pltpu.PrefetchScalarGridSpec(
            num_scalar_prefetch=1,
            grid=(n // tm,),
            in_specs=[pl.BlockSpec((tm, d), lambda i, *_: (i, 0)),
                      pl.BlockSpec((tm, LANES), lambda i, *_: (i, 0)),
                      pl.BlockSpec((1, d), lambda i, *_: (0, 0)),
                      pl.BlockSpec(memory_space=pl.ANY)],
            out_specs=out_specs,
            scratch_shapes=[pltpu.VMEM((2, 2, tm, ys.shape[1]), ys.dtype),
                            pltpu.SemaphoreType.DMA((2,))]),
        out_shape=out_shape,
        compiler_params=_cparams(("arbitrary",)),
        name="moe_combine_final" if final else "moe_combine",
    )(slots, x, meta, gfin.reshape(1, d), ys)


def _moe(x1, h, logits, rbias, w1, w3, w2, layer, gfin, n_first_rows, final, tm_e=192, tm_t=128):
    n, d = h.shape
    meta, idx, cnt = _route(logits, rbias, 256)
    idx = idx.astype(jnp.int32)
    eid = idx[:, 0:2, :].transpose(1, 0, 2).reshape(-1)
    rank = idx[:, 2:4, :].transpose(1, 0, 2).reshape(-1)
    counts = cnt[0, _EXP_LANE0:_EXP_LANE0 + N_EXPERTS].astype(jnp.int32)
    padded = (counts + tm_e - 1) // tm_e * tm_e
    p_end = jnp.cumsum(padded)
    p_start = (p_end - padded).astype(jnp.int32)
    s = n * 2
    nblk = (s + N_EXPERTS * (tm_e - 1) + tm_e - 1) // tm_e
    bidx = jnp.arange(nblk, dtype=jnp.int32)
    block_e = jnp.minimum(jnp.sum((p_end[None, :] <= (bidx * tm_e)[:, None]).astype(jnp.int32), axis=1),
                          N_EXPERTS - 1).astype(jnp.int32)
    nb_used = (p_end[-1:] // tm_e).astype(jnp.int32)
    prev_e = jnp.concatenate([jnp.full((1,), -1, jnp.int32), block_e[:-1]])
    first = ((bidx < nb_used[0]) & (block_e != prev_e)).astype(jnp.int32)
    slot = ((jnp.cumsum(first) - 1) % 2).astype(jnp.int32)
    eidx = jnp.arange(N_EXPERTS, dtype=jnp.int32)
    live = jnp.where(counts > 0, eidx, N_EXPERTS)
    next_live = jnp.min(jnp.where(eidx[None, :] > eidx[:, None], live[None, :], N_EXPERTS), axis=1)
    nxt_e = jnp.sum(jnp.where(block_e[:, None] == eidx[None, :], next_live[None, :], 0), axis=1)
    nxt_e = jnp.where(nxt_e < N_EXPERTS, nxt_e, -1).astype(jnp.int32)
    xs, slots = _dispatch(h, eid, rank, p_start, (p_start + counts).astype(jnp.int32),
                          (padded - counts).astype(jnp.int32), nb_used, nblk, tm_t, tm_e)
    ys = _experts(xs, block_e, nb_used, first, nxt_e, slot, w1, w3, w2, layer, tm_e)
    return _combine(x1, meta, slots, ys, gfin, tm_t, n_first_rows, final)


def _lambda_value(lam_ref, lambda_init):
    lam = lam_ref[...]
    s1 = jnp.sum(lam[0:1] * lam[1:2], axis=1, keepdims=True)
    s2 = jnp.sum(lam[2:3] * lam[3:4], axis=1, keepdims=True)
    return jnp.exp(s1) - jnp.exp(s2) + lambda_init


def _lane_tile(x, width):
    if width % LANES == 0:
        return jnp.concatenate([x] * (width // LANES), axis=1)
    return x[:, :width]


def _softmax_step(j, s, v, m_ref, l_ref, acc_ref, rows=slice(None)):
    m_prev = m_ref[j, rows]
    m_new = jnp.maximum(m_prev, jnp.max(s, axis=1, keepdims=True))
    alpha = jnp.exp(m_prev - m_new)
    p = jnp.exp(s - _lane_tile(m_new, s.shape[1]))
    l_ref[j, rows] = alpha * l_ref[j, rows] + jnp.sum(p, axis=1, keepdims=True)
    acc = acc_ref[j, rows]
    acc_ref[j, rows] = acc * _lane_tile(alpha, acc.shape[1]) + _dot(p, v)
    m_ref[j, rows] = m_new


def _attn_finish(lam_ref, g_ref, l_ref, acc_ref, lambda_init):
    lam = _lambda_value(lam_ref, lambda_init)
    width = acc_ref.shape[-1]
    o = acc_ref[0] / _lane_tile(l_ref[0], width) - lam * (acc_ref[1] / _lane_tile(l_ref[1], width))
    return _rms(o, g_ref[...]) * (1.0 - lambda_init)


def _alibi_slopes():
    slopes = [2.0 ** (-8.0 * (h + 1) / DIFF_HEADS) for h in range(DIFF_HEADS)]
    assert all(math.frexp(s)[0] == 0.5 for s in slopes)
    return slopes


def _attn_prompt_body(iq_ref, ik_ref, slope_ref, q_ref, k_ref, v_ref, kt_ref, lam_ref, g_ref,
                      o_ref, qa_ref, m_ref, l_ref, acc_ref, *, tq, lambda_init):
    hg = pl.program_id(1)
    t = pl.program_id(2)
    iq = iq_ref[t]
    ik = ik_ref[t]
    dh = DIFF_DH
    hw = 2 * dh
    slopes = [slope_ref[hg * ATTN_HEADS_PER_STEP + hh] for hh in range(ATTN_HEADS_PER_STEP)]

    @pl.when(ik == 0)
    def _():
        m_ref[...] = jnp.full(m_ref.shape, -jnp.inf, F32)
        l_ref[...] = jnp.zeros(l_ref.shape, F32)
        acc_ref[...] = jnp.zeros(acc_ref.shape, F32)
        qpos = iq * tq + lax.broadcasted_iota(jnp.int32, (tq, LANES), 0)
        lane = lax.broadcasted_iota(jnp.int32, (tq, LANES), 1)
        qa = (qpos // CHUNK).astype(F32)
        qb = (qpos % CHUNK).astype(F32)
        for hh, slope in enumerate(slopes):
            feat = jnp.where(lane == 0, qa * (-slope * CHUNK),
                             jnp.where(lane == 1, qb * (-slope), jnp.where(lane < 4, slope, 0.0)))
            for j in range(2):
                c0 = hh * hw + j * dh
                qa_ref[hh, j] = jnp.concatenate([q_ref[:, c0:c0 + dh] * (dh ** -0.5), feat], axis=1)

    kt = kt_ref[...]
    ts = tq // ATTN_ROW_SPLIT

    def keys(hh, j, nkeys):
        c0 = hh * hw + j * dh
        return jnp.concatenate([k_ref[0:nkeys, c0:c0 + dh], kt[:nkeys]], axis=1)

    @pl.when(ik < iq)
    def _():
        for hh in range(ATTN_HEADS_PER_STEP):
            v = v_ref[:, hh * hw:(hh + 1) * hw]
            for r in range(ATTN_ROW_SPLIT):
                rows = slice(r * ts, (r + 1) * ts)
                for j in range(2):
                    _softmax_step(j, _dot_nt(qa_ref[hh, j, rows], keys(hh, j, tq)), v,
                                  m_ref.at[hh], l_ref.at[hh], acc_ref.at[hh], rows)

    @pl.when(ik == iq)
    def _():
        outs = []
        for hh, slope in enumerate(slopes):
            for r in range(ATTN_ROW_SPLIT):
                rows = slice(r * ts, (r + 1) * ts)
                nkeys = (r + 1) * ts
                row = r * ts + lax.broadcasted_iota(jnp.int32, (ts, nkeys), 0)
                col = lax.broadcasted_iota(jnp.int32, (ts, nkeys), 1)
                fix = jnp.maximum(col - row, 0).astype(F32) * (-2.0 * slope)
                vis = (col // CHUNK) <= (row // CHUNK)
                v = v_ref[0:nkeys, hh * hw:(hh + 1) * hw]
                for j in range(2):
                    s = _dot_nt(qa_ref[hh, j, rows], keys(hh, j, nkeys))
                    _softmax_step(j, jnp.where(vis, s + fix, -jnp.inf), v,
                                  m_ref.at[hh], l_ref.at[hh], acc_ref.at[hh], rows)
            outs.append(_attn_finish(lam_ref, g_ref.at[hh], l_ref.at[hh], acc_ref.at[hh], lambda_init))
        o_ref[...] = _pack_groups(jnp.concatenate(outs, axis=1))


def _attn_prompt(q, k, v, nb, t, lam, g_subln, lambda_init, tq):
    hw = 2 * DIFF_DH
    assert t % tq == 0 and (tq // ATTN_ROW_SPLIT) % CHUNK == 0 and t // CHUNK <= 256
    nq = t // tq
    pairs = [(iq, ik) for iq in range(nq) for ik in range(iq + 1)]
    iq_tab = jnp.asarray(np.array([p[0] for p in pairs], np.int32))
    ik_tab = jnp.asarray(np.array([p[1] for p in pairs], np.int32))
    slopes = jnp.asarray(np.array(_alibi_slopes(), np.float32))
    pos = np.arange(t)
    ktab = np.zeros((t, LANES), np.float32)
    ktab[:, 0] = 1.0
    ktab[:, 1] = 1.0
    ktab[:, 2] = (pos // CHUNK) * CHUNK
    ktab[:, 3] = pos % CHUNK
    g3 = g_subln.reshape(DIFF_HEADS, 1, hw)
    hps = ATTN_HEADS_PER_STEP
    assert DIFF_HEADS % hps == 0
    bw = hps * hw
    assert bw % PACK_GROUP == 0
    op = _packed_row(bw, _pack_groups)
    return pl.pallas_call(
        functools.partial(_attn_prompt_body, tq=tq, lambda_init=lambda_init),
        grid_spec=pltpu.PrefetchScalarGridSpec(
            num_scalar_prefetch=3,
            grid=(nb, DIFF_HEADS // hps, len(pairs)),
            in_specs=[pl.BlockSpec((tq, bw), lambda b, h, p, iqt, ikt, sl: (b * nq + iqt[p], h)),
                      pl.BlockSpec((tq, bw), lambda b, h, p, iqt, ikt, sl: (b * nq + ikt[p], h)),
                      pl.BlockSpec((tq, bw), lambda b, h, p, iqt, ikt, sl: (b * nq + ikt[p], h)),
                      pl.BlockSpec((tq, LANES), lambda b, h, p, iqt, ikt, sl: (ikt[p], 0)),
                      pl.BlockSpec((4, DIFF_DH), lambda b, h, p, iqt, ikt, sl: (0, 0)),
                      pl.BlockSpec((hps, 1, hw), lambda b, h, p, iqt, ikt, sl: (h, 0, 0))],
            out_specs=pl.BlockSpec((tq, op.shape[1]), lambda b, h, p, iqt, ikt, sl: (b * nq + iqt[p], h)),
            scratch_shapes=[pltpu.VMEM((hps, 2, tq, hw), F32),
                            pltpu.VMEM((hps, 2, tq, LANES), F32),
                            pltpu.VMEM((hps, 2, tq, LANES), F32),
                            pltpu.VMEM((hps, 2, tq, hw), F32)]),
        out_shape=jax.ShapeDtypeStruct((nb * t, op.shape[1] * (DIFF_HEADS // hps)), op.dtype),
        compiler_params=_cparams(("arbitrary", "arbitrary", "arbitrary")),
        name="diff_attn_prompt",
    )(iq_tab, ik_tab, slopes, q, k, v, jnp.asarray(ktab), lam, g3)


def _attn_sample_body(q_ref, kc_hbm, vc_hbm, kn_ref, vn_ref, lam_ref, g_ref,
                      o_ref, kbuf, vbuf, sem, m_ref, l_ref, acc_ref, *, li, tk, past, lambda_init):
    b = pl.program_id(0)
    ik = pl.program_id(1)
    nb = pl.num_programs(0)
    nkb = pl.num_programs(1) - 1
    dh = DIFF_DH
    hw = 2 * dh
    tq = q_ref.shape[0]
    scale = dh ** -0.5

    def fetch_copies(f):
        fb, fk = f // nkb, f % nkb
        slot = f % 2
        cps = []
        for h in range(DIFF_HEADS):
            for hbm, buf, c in ((kc_hbm, kbuf, 0), (vc_hbm, vbuf, 1)):
                cps.append(pltpu.make_async_copy(hbm.at[li, fb, pl.ds(fk * tk, tk), h, :], buf.at[slot, h],
                                                 sem.at[slot, c]))
        return cps

    fcur = b * nkb + ik

    @pl.when(jnp.logical_and(b == 0, ik == 0))
    def _():
        for cp in fetch_copies(0):
            cp.start()

    @pl.when(ik < nkb)
    def _():
        @pl.when(fcur + 1 < nb * nkb)
        def _():
            for cp in fetch_copies(fcur + 1):
                cp.start()

        for cp in fetch_copies(fcur):
            cp.wait()

    @pl.when(ik == 0)
    def _():
        m_ref[...] = jnp.full(m_ref.shape, -jnp.inf, F32)
        l_ref[...] = jnp.zeros(l_ref.shape, F32)
        acc_ref[...] = jnp.zeros(acc_ref.shape, F32)

    def attend(h, k, v, k0, width):
        slope = _alibi_slopes()[h]
        q = q_ref[:, h * hw:(h + 1) * hw]
        qpos = past + lax.broadcasted_iota(jnp.int32, (tq, width), 0)
        kpos = k0 + lax.broadcasted_iota(jnp.int32, (tq, width), 1)
        bias = jnp.abs(qpos - kpos).astype(F32) * (-slope)
        vis = (kpos // CHUNK) <= (qpos // CHUNK)
        for j in range(2):
            s = _dot_nt(q[:, j * dh:(j + 1) * dh], k[:, j * dh:(j + 1) * dh]) * scale + bias
            s = jnp.where(vis, s, -jnp.inf)
            _softmax_step(j, s, v, m_ref.at[h], l_ref.at[h], acc_ref.at[h])

    @pl.when(ik < nkb)
    def _():
        slot = fcur % 2
        for h in range(DIFF_HEADS):
            attend(h, kbuf[slot, h], vbuf[slot, h], ik * tk, tk)

    @pl.when(ik == nkb)
    def _():
        outs = []
        for h in range(DIFF_HEADS):
            hs = slice(h * hw, (h + 1) * hw)
            attend(h, kn_ref[:, hs], vn_ref[:, hs], past, tq)
            outs.append(_attn_finish(lam_ref, g_ref.at[h], l_ref.at[h], acc_ref.at[h], lambda_init))
        o_ref[...] = _pack_groups(jnp.concatenate(outs, axis=1))


def _attn_sample(q, k, v, cache_k, cache_v, li, lam, g_subln, lambda_init, tk):
    n_rows = q.shape[0]
    _, nb, past, nh, hw = cache_k.shape
    aw = nh * hw
    tq = n_rows // nb
    tk = min(tk, past)
    assert past % tk == 0
    nkb = past // tk
    g3 = g_subln.reshape(nh, 1, hw)
    cache_spec = pl.BlockSpec(memory_space=pl.ANY)
    row_spec = pl.BlockSpec((tq, aw), lambda b, ik: (b, 0))
    op = _packed_row(aw, _pack_groups)
    return pl.pallas_call(
        functools.partial(_attn_sample_body, li=li, tk=tk, past=past, lambda_init=lambda_init),
        grid=(nb, nkb + 1),
        in_specs=[row_spec,
                  cache_spec, cache_spec,
                  row_spec,
                  row_spec,
                  pl.BlockSpec((4, DIFF_DH), lambda b, ik: (0, 0)),
                  pl.BlockSpec((nh, 1, hw), lambda b, ik: (0, 0, 0))],
        out_specs=pl.BlockSpec((tq, op.shape[1]), lambda b, ik: (b, 0)),
        scratch_shapes=[pltpu.VMEM((2, nh, tk, hw), F32),
                        pltpu.VMEM((2, nh, tk, hw), F32),
                        pltpu.SemaphoreType.DMA((2, 2)),
                        pltpu.VMEM((nh, 2, tq, LANES), F32),
                        pltpu.VMEM((nh, 2, tq, LANES), F32),
                        pltpu.VMEM((nh, 2, tq, hw), F32)],
        out_shape=jax.ShapeDtypeStruct((n_rows, op.shape[1]), op.dtype),
        compiler_params=_cparams(("arbitrary", "arbitrary")),
        name="diff_attn_sample",
    )(q, cache_k, cache_v, k, v, lam, g3)


def _pad_cols(a, width):
    return jnp.pad(a, ((0, 0), (0, width - a.shape[1])))


def _mixer_constants(dk, dv, w_gate_up, b_gate, g_gla, conv_w, conv_b, dt_bias, a_log, d_skip, g_ssd):
    C = REC_ROWS
    nh = GLA_HEADS
    inner = g_ssd.shape[0]
    n_ssd = inner // SSD_HEADDIM
    wup = jnp.pad(w_gate_up, ((0, LANES - w_gate_up.shape[0]), (0, 0)))
    tri = jnp.asarray(np.tril(np.ones((C, C), np.float32)), BF16)
    e64 = np.zeros((LANES, inner), np.float32)
    ec = np.zeros((LANES, n_ssd * C), np.float32)
    for hh in range(n_ssd):
        e64[hh, hh * SSD_HEADDIM:(hh + 1) * SSD_HEADDIM] = 1.0
        ec[hh, hh * C:(hh + 1) * C] = 1.0
    eye = np.tile(np.eye(C, dtype=np.float32), (1, n_ssd))
    caus = np.tile(np.tril(np.ones((C, C), np.float32)), (1, n_ssd))
    return [wup, b_gate.reshape(1, -1), g_gla.reshape(nh, dv), conv_w, conv_b.reshape(1, -1),
            _pad_cols(dt_bias.reshape(1, -1), LANES),
            _pad_cols(-jnp.exp(a_log.astype(F32)).reshape(1, -1), LANES),
            jnp.repeat(d_skip, SSD_HEADDIM).reshape(1, -1), g_ssd.reshape(1, -1),
            tri, jnp.asarray(e64, BF16), jnp.asarray(ec, BF16), jnp.asarray(eye), jnp.asarray(caus)]


def kernel(x_prompt, x_sample, state_gla, state_ssm, state_conv, cache_k, cache_v, norm_mix, norm_ffn, norm_final, w_in, w_gate_up, b_gate, g_gla, conv_w, conv_b, dt_bias, a_log, d_skip, g_ssd, w_out_mix, w_qkv, lam_q1, lam_k1, lam_q2, lam_k2, g_subln, w_o, router_group_w, router_group_b, router_expert_w, router_expert_b, w1, w3, w2):
    bp, tp, d = x_prompt.shape
    bs, ts, _ = x_sample.shape
    n_p, n_s = bp * tp, bs * ts
    n = n_p + n_s
    depth = norm_mix.shape[0]
    nh = GLA_HEADS
    dk, dv = state_gla.shape[-2], state_gla.shape[-1]
    qk_w, vw = nh * dk, nh * dv
    n_ssd = state_ssm.shape[2]
    inner = n_ssd * SSD_HEADDIM
    hpg = n_ssd // SSD_GROUPS
    gs = SSD_GROUPS * SSD_STATE
    cw = inner + 2 * gs
    rank = w_gate_up.shape[1]
    assert w_gate_up.shape[2] == qk_w and rank <= LANES and n_ssd <= LANES

    x = jnp.concatenate([x_prompt.reshape(n_p, d), x_sample.reshape(n_s, d)], axis=0)
    tm_row = next(tm for tm in (1408, 1056, 768, 256) if n % tm == 0)

    i_even = i_odd = 0
    gla_p, ssm_p, conv_p, gla_s, ssm_s, conv_s = [], [], [], [], [], []
    k_p, v_p, k_s, v_s = [], [], [], []
    y_p = y_s = None
    for layer in range(depth):
        final = layer == depth - 1
        if layer % 2 == 0:
            i = i_even
            i_even += 1
            offs = np.cumsum([0, qk_w, qk_w, vw, rank, vw, inner, cw, n_ssd])
            w_t = w_in[i].T
            seg = lambda j: w_t[offs[j]:offs[j + 1]]
            pad_rows = lambda a: jnp.pad(a, ((0, LANES - a.shape[0]), (0, 0)))
            w_cat = jnp.concatenate([seg(0), seg(1), seg(2), seg(4), seg(5), seg(6),
                                     pad_rows(seg(3)), pad_rows(seg(7))], axis=0).astype(BF16)
            proj = _norm_matmul(x, norm_mix[layer], w_cat, tm_row, 1280, w_is_transposed=True)
            consts = _mixer_constants(dk, dv, w_gate_up[i], b_gate[i], g_gla[i], conv_w[i], conv_b[i],
                                      dt_bias[i], a_log[i], d_skip[i], g_ssd[i])

            def to_group_state(sm):
                b_ = sm.shape[0]
                return sm.reshape(b_, SSD_GROUPS, hpg, SSD_STATE, SSD_HEADDIM).transpose(0, 1, 3, 2, 4) \
                         .reshape(b_, SSD_GROUPS, SSD_STATE, hpg * SSD_HEADDIM)

            def from_group_state(sg):
                b_ = sg.shape[0]
                return sg.reshape(b_, SSD_GROUPS, SSD_STATE, hpg, SSD_HEADDIM).transpose(0, 1, 3, 2, 4) \
                         .reshape(b_, n_ssd, SSD_STATE, SSD_HEADDIM)

            def pad_conv(cv):
                return jnp.pad(cv, ((0, 0), (CONV_PAD - cv.shape[1], 0), (0, 0)))

            mix_a, g_fin, s_fin, c_fin = _mixer_call(
                proj, 0, bp, tp, jnp.zeros((bp, nh, dv, dk), F32),
                jnp.zeros((bp, SSD_GROUPS, SSD_STATE, hpg * SSD_HEADDIM), F32),
                jnp.zeros((bp, CONV_PAD, cw), F32), consts, dk, dv)
            gla_p.append(g_fin.transpose(0, 1, 3, 2))
            ssm_p.append(from_group_state(s_fin))
            conv_p.append(c_fin[:, CONV_PAD - (SSD_CONV - 1):])
            mix_b, g_fin, s_fin, c_fin = _mixer_call(
                proj, n_p, bs, ts, state_gla[i].transpose(0, 1, 3, 2),
                to_group_state(state_ssm[i]), pad_conv(state_conv[i]), consts, dk, dv)
            gla_s.append(g_fin.transpose(0, 1, 3, 2))
            ssm_s.append(from_group_state(s_fin))
            conv_s.append(c_fin[:, CONV_PAD - (SSD_CONV - 1):])
            w_proj = w_out_mix[i]
        else:
            i = i_odd
            i_odd += 1
            lambda_init = 0.8 - 0.6 * math.exp(-0.3 * layer)
            tm_p = next(tm for tm in (1024, 512, 256) if n_p % tm == 0)
            w_qkv_b = w_qkv[i].astype(BF16)
            q_a, k_a, v_a = _norm_matmul(x, norm_mix[layer], w_qkv_b, tm_p, 1024, nrows=n_p, parts=3)
            q_b, k_b, v_b = _norm_matmul(x, norm_mix[layer], w_qkv_b, n_s, 1024, row0=n_p, parts=3)
            lam = jnp.stack([lam_q1[i], lam_k1[i], lam_q2[i], lam_k2[i]])
            mix_a = _attn_prompt(q_a, k_a, v_a, bp, tp, lam, g_subln[i], lambda_init, min(512, tp))
            mix_b = _attn_sample(q_b, k_b, v_b, cache_k, cache_v, i, lam, g_subln[i], lambda_init, 1024)
            k_p.append(k_a.reshape(bp, tp, DIFF_HEADS, 2 * DIFF_DH))
            v_p.append(v_a.reshape(bp, tp, DIFF_HEADS, 2 * DIFF_DH))
            k_s.append(k_b.reshape(bs, ts, DIFF_HEADS, 2 * DIFF_DH))
            v_s.append(v_b.reshape(bs, ts, DIFF_HEADS, 2 * DIFF_DH))
            w_proj = w_o[i]

        wr = _pad_cols(jnp.concatenate([router_group_w[layer], router_expert_w[layer]], axis=1), LANES)
        rbias = _pad_cols(jnp.concatenate([router_group_b[layer], router_expert_b[layer]]).reshape(1, -1), LANES)
        x1, h2, logits = _proj_resid(mix_a, mix_b, w_proj, x, norm_ffn[layer], wr, 256)
        res = _moe(x1, h2, logits, rbias, w1, w3, w2, layer, norm_final, n_p, final)
        if final:
            y_p, y_s = res
        else:
            x = res[0]

    return (y_p.reshape(bp, tp, d), y_s.reshape(bs, ts, d),
            jnp.stack(gla_p), jnp.stack(ssm_p), jnp.stack(conv_p), jnp.stack(k_p), jnp.stack(v_p),
            jnp.stack(gla_s), jnp.stack(ssm_s), jnp.stack(conv_s), jnp.stack(k_s), jnp.stack(v_s))
```

```python
import functools
import math

import numpy as np
import jax
import jax.numpy as jnp
from jax import lax
from jax.experimental import pallas as pl
from jax.experimental.pallas import tpu as pltpu

F32 = jnp.float32
BF16 = jnp.bfloat16

EPS = 1e-6
CHUNK = 64
GLA_HEADS = 4
GLA_TAU = 16.0
SSD_HEADDIM = 64
SSD_STATE = 128
SSD_GROUPS = 4
SSD_CONV = 4
DIFF_HEADS = 8
DIFF_DH = 128
MOE_GROUPS = 4
MOE_PER_GROUP = 8
N_EXPERTS = MOE_GROUPS * MOE_PER_GROUP

LANES = 128
SUBLANES = 8
REC_ROWS = 128
STRIP = 16
CONV_PAD = 8
ATTN_ROW_SPLIT = 2
ATTN_HEADS_PER_STEP = 4
VMEM_LIMIT = 56 * 1024 * 1024


def _cparams(sem, vmem=VMEM_LIMIT):
    return pltpu.CompilerParams(dimension_semantics=sem, vmem_limit_bytes=vmem)


def _dot(a, b):
    return jnp.dot(a, b, preferred_element_type=F32)


def _dot_nt(a, b):
    return lax.dot_general(a, b, (((1,), (1,)), ((), ())), preferred_element_type=F32)


def _dot_tn(a, b):
    return lax.dot_general(a, b, (((0,), (0,)), ((), ())), preferred_element_type=F32)


def _split_hi_lo(a):
    hi = a.astype(BF16)
    lo = (a - hi.astype(F32)).astype(BF16)
    return hi, lo


def _exact_left(m, a):
    hi, lo = _split_hi_lo(a)
    return _dot(m, hi) + _dot(m, lo)


def _exact_right(a, m):
    hi, lo = _split_hi_lo(a)
    return _dot(hi, m) + _dot(lo, m)


def _silu(x):
    h = 0.5 * x
    return h * (1.0 + jnp.tanh(h))


def _log1p_exp_neg_abs(x):
    e = jnp.exp(-jnp.abs(x))
    u = 1.0 + e
    return jnp.where(u == 1.0, e, jnp.log(u) * (e / (u - 1.0)))


def _rms(x, g):
    return x * lax.rsqrt(jnp.mean(x * x, axis=-1, keepdims=True) + EPS) * g


def _pack_halves(x):
    w = x.shape[1] // 2
    lo = lax.bitcast_convert_type(x[:, :w].astype(BF16).astype(F32), jnp.uint32)
    hi = lax.bitcast_convert_type(x[:, w:].astype(BF16).astype(F32), jnp.uint32)
    return (lo >> 16) | (hi & jnp.uint32(0xFFFF0000))


def _unpack_halves(p):
    lo = lax.bitcast_convert_type(p << 16, F32)
    hi = lax.bitcast_convert_type(p & jnp.uint32(0xFFFF0000), F32)
    return jnp.concatenate([lo, hi], axis=1)


PACK_GROUP = 512


def _pack_groups(x):
    return jnp.concatenate([_pack_halves(x[:, c:c + PACK_GROUP]) for c in range(0, x.shape[1], PACK_GROUP)],
                           axis=1)


def _unpack_groups(p):
    step = _packed_row(PACK_GROUP).shape[1]
    return jnp.concatenate([_unpack_halves(p[:, c:c + step]) for c in range(0, p.shape[1], step)], axis=1)


def _packed_row(d, fn=None):
    return jax.eval_shape(fn or _pack_halves, jax.ShapeDtypeStruct((SUBLANES, d), F32))


def _norm_matmul_body(x_ref, g_ref, w_ref, *refs, w_is_transposed, parts):
    o_refs, h_ref = refs[:parts], refs[parts]
    j = pl.program_id(1)

    @pl.when(j == 0)
    def _():
        h_ref[...] = _rms(x_ref[...], g_ref[...]).astype(h_ref.dtype)

    res = (_dot_nt if w_is_transposed else _dot)(h_ref[...], w_ref[...])
    if parts == 1:
        o_refs[0][...] = res
    else:
        per = pl.num_programs(1) // parts
        for p in range(parts):
            @pl.when(j // per == p)
            def _(p=p):
                o_refs[p][...] = res


def _norm_matmul(x, g, w, tm, tn, *, row0=0, nrows=None, parts=1, w_is_transposed=False):
    n, d = x.shape
    nrows = n - row0 if nrows is None else nrows
    nout = w.shape[0] if w_is_transposed else w.shape[1]
    assert nrows % tm == 0 and row0 % tm == 0 and nout % (tn * parts) == 0
    blk0 = row0 // tm
    per = nout // tn // parts
    w_spec = (pl.BlockSpec((tn, d), lambda i, j: (j, 0)) if w_is_transposed
              else pl.BlockSpec((d, tn), lambda i, j: (0, j)))
    out_specs = [pl.BlockSpec((tm, tn), lambda i, j, p=p: (i, jnp.clip(j - p * per, 0, per - 1)))
                 for p in range(parts)]
    outs = pl.pallas_call(
        functools.partial(_norm_matmul_body, w_is_transposed=w_is_transposed, parts=parts),
        grid=(nrows // tm, nout // tn),
        in_specs=[pl.BlockSpec((tm, d), lambda i, j: (blk0 + i, 0), pipeline_mode=pl.Buffered(1)),
                  pl.BlockSpec((1, d), lambda i, j: (0, 0)),
                  w_spec],
        out_specs=out_specs,
        out_shape=[jax.ShapeDtypeStruct((nrows, nout // parts), F32)] * parts,
        scratch_shapes=[pltpu.VMEM((tm, d), w.dtype)],
        compiler_params=_cparams(("arbitrary", "arbitrary")),
        name="norm_matmul",
    )(x, g.reshape(1, d), w)
    return outs[0] if parts == 1 else outs


def _mixer_body(p_ref, gla0_ref, ssm0_ref, conv0_ref, wup_ref, bgate_ref, ggla_ref, convw_ref,
                convb_ref, dtb_ref, aneg_ref, dskip_ref, gssd_ref, tri_ref, e64_ref, ec_ref,
                eye_ref, caus_ref,
                o_ref, gla_ref, ssm_ref, ctail_ref,
                sg_ref, ss_ref, ext_ref, b_ref, *, rb, dk, dv):
    C = REC_ROWS
    c = pl.program_id(1)
    nc = pl.num_programs(1)
    nh = GLA_HEADS
    qk_w = nh * dk
    vw = nh * dv
    inner = vw
    gs = SSD_GROUPS * SSD_STATE
    o_q, o_k, o_v = 0, qk_w, 2 * qk_w
    o_r = o_v + vw
    o_z = o_r + vw
    o_x = o_z + inner
    o_g = o_x + inner + 2 * gs
    o_dt = o_g + LANES

    @pl.when(c == 0)
    def _init():
        sg_ref[...] = gla0_ref[...]
        ss_ref[...] = ssm0_ref[...]
        ext_ref[0:CONV_PAD, :] = conv0_ref[...]

    p = p_ref[...]
    if rb < C:
        p = jnp.concatenate([p, jnp.zeros((C - rb, p.shape[1]), F32)], axis=0)

    def rowmask(width):
        return lax.broadcasted_iota(jnp.int32, (C, width), 0) < rb

    q = p[:, o_q:o_q + qk_w] * (dk ** -0.5)
    k = p[:, o_k:o_k + qk_w]
    v = p[:, o_v:o_v + vw]
    r = p[:, o_r:o_r + vw]
    z = p[:, o_z:o_z + inner]
    xbc = p[:, o_x:o_x + inner + 2 * gs]
    glr = p[:, o_g:o_g + LANES]
    dtp = p[:, o_dt:o_dt + LANES]

    tri = tri_ref[...]

    zg = _dot(glr, wup_ref[...]) + bgate_ref[...]
    la = (jnp.minimum(zg, 0.0) - _log1p_exp_neg_abs(zg)) * (1.0 / GLA_TAU)
    if rb < C:
        la = jnp.where(rowmask(qk_w), la, 0.0)
    bcum = _exact_left(tri, la)
    b_ref[...] = bcum

    prow = lax.broadcasted_iota(jnp.int32, (STRIP, C), 0)
    pcol = lax.broadcasted_iota(jnp.int32, (STRIP, C), 1)
    o_heads = []
    for h in range(nh):
        hs = slice(h * dk, (h + 1) * dk)
        vs = slice(h * dv, (h + 1) * dv)
        bh = bcum[:, hs]
        qh = q[:, hs]
        kh = k[:, hs]
        vh = v[:, vs]
        strips = []
        for i in range(C // STRIP):
            r0 = i * STRIP
            if r0 >= rb:
                strips.append(jnp.zeros((STRIP, C), F32))
                continue
            if i == 0:
                ref_row = jnp.zeros((1, dk), F32)
            else:
                ref_row = b_ref[pl.ds(r0 - 1, 1), hs]
            q_i = qh[r0:r0 + STRIP] * jnp.exp(bh[r0:r0 + STRIP] - ref_row)
            n_i = r0 + STRIP
            k_i = kh[:n_i] * jnp.exp(ref_row - bh[:n_i])
            if n_i < C:
                k_i = jnp.concatenate([k_i, jnp.zeros((C - n_i, dk), F32)], axis=0)
            s_i = _dot_nt(q_i, k_i)
            strips.append(jnp.where(pcol <= prow + r0, s_i, 0.0))
        pmat = jnp.concatenate(strips, axis=0)
        b_last = b_ref[pl.ds(C - 1, 1), hs]
        st = sg_ref[h]
        o_h = _dot(pmat, vh) + _dot_nt(qh * jnp.exp(bh), st)
        k_st = kh * jnp.exp(b_last - bh)
        sg_ref[h] = st * jnp.exp(b_last) + _dot_tn(vh, k_st)
        o_h = _rms(o_h, ggla_ref[pl.ds(h, 1), :]) * _silu(r[:, vs])
        o_heads.append(o_h)
    o_a = jnp.concatenate(o_heads, axis=1)

    cw = inner + 2 * gs
    ext_ref[CONV_PAD:CONV_PAD + C, :] = xbc
    conv = convb_ref[...]
    for j in range(SSD_CONV):
        conv = conv + convw_ref[pl.ds(j, 1), :] * ext_ref[pl.ds(CONV_PAD - (SSD_CONV - 1) + j, C), :]
    xc = _silu(conv)
    xs = xc[:, :inner]
    bm = xc[:, inner:inner + gs]
    cm = xc[:, inner + gs:cw]
    dtv = dtp + dtb_ref[...]
    dt = jnp.maximum(dtv, 0.0) + _log1p_exp_neg_abs(dtv)
    ld = dt * aneg_ref[...]
    if rb < C:
        dt = jnp.where(rowmask(LANES), dt, 0.0)
        ld = jnp.where(rowmask(LANES), ld, 0.0)
    bs = _exact_left(tri, ld)
    e64 = e64_ref[...]
    dt64 = _exact_right(dt, e64)
    b64 = _exact_right(bs, e64)
    blast64 = b64[C - 1:C, :]
    xdt = xs * dt64
    bc = _exact_right(bs, ec_ref[...])
    br = jnp.sum(eye_ref[...] * bc, axis=0, keepdims=True)
    vis = caus_ref[...] > 0.0
    lf = jnp.where(vis, jnp.exp(jnp.where(vis, bc - br, 0.0)), 0.0)
    hpg = inner // SSD_HEADDIM // SSD_GROUPS
    gw = hpg * SSD_HEADDIM
    lane_head = lax.broadcasted_iota(jnp.int32, (C, gw), 1) // SSD_HEADDIM
    y_groups = []
    for g in range(SSD_GROUPS):
        cm_g = cm[:, g * SSD_STATE:(g + 1) * SSD_STATE]
        bm_g = bm[:, g * SSD_STATE:(g + 1) * SSD_STATE]
        gl = slice(g * gw, (g + 1) * gw)
        gmat = _dot_nt(cm_g, bm_g)
        a4 = jnp.concatenate([gmat] * hpg, axis=1) * lf[:, g * hpg * C:(g + 1) * hpg * C]
        xg = xdt[:, gl]
        x4 = jnp.concatenate([jnp.where(lane_head == hh, xg, 0.0) for hh in range(hpg)], axis=0)
        s_g = ss_ref[g]
        y_g = _dot(a4, x4) + _dot(cm_g, s_g) * jnp.exp(b64[:, gl])
        xw = xg * jnp.exp(blast64[:, gl] - b64[:, gl])
        ss_ref[g] = s_g * jnp.exp(blast64[:, gl]) + _dot_tn(bm_g, xw)
        y_groups.append(y_g)
    y = jnp.concatenate(y_groups, axis=1) + xs * dskip_ref[...]
    y = _rms(y * _silu(z), gssd_ref[...])

    o_full = jnp.concatenate([o_a, y], axis=1)
    o_ref[...] = _pack_groups(o_full[:rb])

    @pl.when(c == nc - 1)
    def _fin():
        gla_ref[...] = sg_ref[...]
        ssm_ref[...] = ss_ref[...]
        ctail_ref[...] = ext_ref[pl.ds(rb, CONV_PAD), :]

    ext_ref[0:CONV_PAD, :] = ext_ref[pl.ds(rb, CONV_PAD), :]


def _mixer_call(proj, row_off, nb, t, gla0, ssm0, conv0, consts, dk, dv):
    C = REC_ROWS
    rb = min(t, C)
    assert t % rb == 0 and row_off % rb == 0
    steps = t // rb
    width = proj.shape[1]
    nh = GLA_HEADS
    vw = nh * dv
    cw = conv0.shape[-1]
    blk0 = row_off // rb

    def full(a):
        nd = a.ndim
        return pl.BlockSpec(a.shape, lambda b, c, _n=nd: (0,) * _n)

    in_specs = [pl.BlockSpec((rb, width), lambda b, c: (blk0 + b * steps + c, 0)),
                pl.BlockSpec((None, nh, dv, dk), lambda b, c: (b, 0, 0, 0)),
                pl.BlockSpec((None, SSD_GROUPS, SSD_STATE, ssm0.shape[-1]), lambda b, c: (b, 0, 0, 0)),
                pl.BlockSpec((None, CONV_PAD, cw), lambda b, c: (b, 0, 0))]
    in_specs += [full(a) for a in consts]
    operands = [proj, gla0, ssm0, conv0, *consts]
    op = _packed_row(2 * vw, _pack_groups)
    out_specs = [pl.BlockSpec((rb, op.shape[1]), lambda b, c: (b * steps + c, 0)),
                 pl.BlockSpec((None, nh, dv, dk), lambda b, c: (b, 0, 0, 0)),
                 pl.BlockSpec((None, SSD_GROUPS, SSD_STATE, ssm0.shape[-1]), lambda b, c: (b, 0, 0, 0)),
                 pl.BlockSpec((None, CONV_PAD, cw), lambda b, c: (b, 0, 0))]
    out_shape = [jax.ShapeDtypeStruct((nb * t, op.shape[1]), op.dtype),
                 jax.ShapeDtypeStruct(gla0.shape, F32),
                 jax.ShapeDtypeStruct(ssm0.shape, F32),
                 jax.ShapeDtypeStruct((nb, CONV_PAD, cw), F32)]
    return pl.pallas_call(
        functools.partial(_mixer_body, rb=rb, dk=dk, dv=dv),
        grid=(nb, steps),
        in_specs=in_specs,
        out_specs=out_specs,
        out_shape=out_shape,
        scratch_shapes=[pltpu.VMEM((nh, dv, dk), F32),
                        pltpu.VMEM((SSD_GROUPS, SSD_STATE, ssm0.shape[-1]), F32),
                        pltpu.VMEM((C + CONV_PAD, cw), F32),
                        pltpu.VMEM((C, nh * dk), F32)],
        compiler_params=_cparams(("arbitrary", "arbitrary")),
        name="gla_ssd_mixer",
    )(*operands)


def _proj_resid_body(oa_ref, ob_ref, w_ref, x_ref, g_ref, wr_ref, x1_ref, h_ref, lg_ref, *, n_first):
    o = _unpack_groups(jnp.where(pl.program_id(0) < n_first, oa_ref[...], ob_ref[...]))
    x1 = x_ref[...] + _dot(o, w_ref[...])
    x1_ref[...] = x1
    h = _rms(x1, g_ref[...])
    h_ref[...] = _pack_halves(h)
    lg_ref[...] = _dot(h, wr_ref[...])


def _proj_resid(o_a, o_b, w, x, g, wr, tm):
    n, d = x.shape
    kin = o_a.shape[1]
    d_in = w.shape[0]
    assert o_a.shape[0] % tm == 0 and o_b.shape[0] % tm == 0
    n_first = o_a.shape[0] // tm
    hp = _packed_row(d)
    return pl.pallas_call(
        functools.partial(_proj_resid_body, n_first=n_first),
        grid=(n // tm,),
        in_specs=[pl.BlockSpec((tm, kin), lambda i: (jnp.minimum(i, n_first - 1), 0)),
                  pl.BlockSpec((tm, kin), lambda i: (jnp.maximum(i - n_first, 0), 0)),
                  pl.BlockSpec((d_in, d), lambda i: (0, 0), pipeline_mode=pl.Buffered(1)),
                  pl.BlockSpec((tm, d), lambda i: (i, 0)),
                  pl.BlockSpec((1, d), lambda i: (0, 0)),
                  pl.BlockSpec((d, LANES), lambda i: (0, 0))],
        out_specs=[pl.BlockSpec((tm, d), lambda i: (i, 0)),
                   pl.BlockSpec((tm, hp.shape[1]), lambda i: (i, 0)),
                   pl.BlockSpec((tm, LANES), lambda i: (i, 0))],
        out_shape=[jax.ShapeDtypeStruct((n, d), F32),
                   jax.ShapeDtypeStruct((n, hp.shape[1]), hp.dtype),
                   jax.ShapeDtypeStruct((n, LANES), F32)],
        compiler_params=_cparams(("arbitrary",)),
        name="proj_resid_norm_router",
    )(o_a, o_b, w, x, g.reshape(1, d), wr)


_META_E0, _META_E1, _META_G0, _META_G1, _META_R0, _META_R1 = range(6)
_EXP_LANE0 = MOE_GROUPS


def _route_body(lg_ref, bias_ref, ltri_ref, eye_ref, meta_ref, idx_ref, cnt_ref, base_ref):
    i = pl.program_id(0)

    @pl.when(i == 0)
    def _():
        base_ref[...] = jnp.zeros_like(base_ref)

    lg = lg_ref[...] + bias_ref[...]
    tm = lg.shape[0]
    lane_i = lax.broadcasted_iota(jnp.int32, (tm, LANES), 1)
    lane = lane_i.astype(F32)
    neg = -jnp.inf
    glog = jnp.where(lane_i < MOE_GROUPS, lg, neg)
    gmax = jnp.max(glog, axis=1, keepdims=True)
    gsel = jnp.min(jnp.where(glog == gmax, lane, float(LANES)), axis=1, keepdims=True)
    pg = 1.0 / jnp.sum(jnp.exp(glog - gmax), axis=1, keepdims=True)
    lo = _EXP_LANE0 + MOE_PER_GROUP * gsel
    el = jnp.where((lane >= lo) & (lane < lo + MOE_PER_GROUP), lg, neg)
    v1 = jnp.max(el, axis=1, keepdims=True)
    i1 = jnp.min(jnp.where(el == v1, lane, float(LANES)), axis=1, keepdims=True)
    el2 = jnp.where(lane == i1, neg, el)
    v2 = jnp.max(el2, axis=1, keepdims=True)
    i2 = jnp.min(jnp.where(el2 == v2, lane, float(LANES)), axis=1, keepdims=True)
    e = jnp.exp(v2 - v1)
    g1 = pg / (1.0 + e)
    g2 = pg * e / (1.0 + e)
    hot1 = lane == i1
    hot2 = lane == i2
    onehot = jnp.where(hot1 | hot2, 1.0, 0.0)
    before = _dot(ltri_ref[...], onehot.astype(BF16)) + base_ref[0:1, :]
    r1 = jnp.sum(jnp.where(hot1, before, 0.0), axis=1, keepdims=True)
    r2 = jnp.sum(jnp.where(hot2, before, 0.0), axis=1, keepdims=True)
    base_ref[0:1, :] = base_ref[0:1, :] + jnp.sum(onehot, axis=0, keepdims=True)
    meta = jnp.zeros((tm, LANES), F32)
    for idx, val in ((_META_E0, i1 - _EXP_LANE0), (_META_E1, i2 - _EXP_LANE0),
                     (_META_G0, g1), (_META_G1, g2), (_META_R0, r1), (_META_R1, r2)):
        meta = jnp.where(lane_i == idx, val, meta)
    meta_ref[...] = meta

    eye = eye_ref[...]
    rows = [jnp.sum(eye * col, axis=0, keepdims=True)
            for col in (i1 - _EXP_LANE0, i2 - _EXP_LANE0, r1, r2)]
    idx_ref[...] = jnp.concatenate(rows + [jnp.zeros((SUBLANES - len(rows), tm), F32)], axis=0)

    @pl.when(i == pl.num_programs(0) - 1)
    def _():
        cnt_ref[...] = base_ref[...]


def _route(logits, bias, tm):
    n = logits.shape[0]
    ltri = jnp.asarray(np.tril(np.ones((tm, tm), np.float32), -1), BF16)
    eye = jnp.asarray(np.eye(tm, dtype=np.float32))
    return pl.pallas_call(
        _route_body,
        grid=(n // tm,),
        in_specs=[pl.BlockSpec((tm, LANES), lambda i: (i, 0)),
                  pl.BlockSpec((1, LANES), lambda i: (0, 0)),
                  pl.BlockSpec((tm, tm), lambda i: (0, 0)),
                  pl.BlockSpec((tm, tm), lambda i: (0, 0))],
        out_specs=[pl.BlockSpec((tm, LANES), lambda i: (i, 0)),
                   pl.BlockSpec((None, SUBLANES, tm), lambda i: (i, 0, 0)),
                   pl.BlockSpec((8, LANES), lambda i: (0, 0))],
        out_shape=[jax.ShapeDtypeStruct((n, LANES), F32),
                   jax.ShapeDtypeStruct((n // tm, SUBLANES, tm), F32),
                   jax.ShapeDtypeStruct((8, LANES), F32)],
        scratch_shapes=[pltpu.VMEM((8, LANES), F32)],
        compiler_params=_cparams(("arbitrary",)),
        name="moe_route",
    )(logits, bias, ltri, eye)


def _slot_of(eid_ref, rank_ref, pstart_ref, idx):
    return pstart_ref[eid_ref[idx]] + rank_ref[idx]


def _dispatch_body(eid_ref, rank_ref, pstart_ref, pad0_ref, npad_ref, nb_ref, h_ref, xs_ref, zbuf, sem, zsem,
                   *, tm, tm_e, nblk):
    i = pl.program_id(0)
    n_tok = pl.num_programs(0) * tm

    def pad_fill(wait):
        def go(cp):
            if wait:
                cp.wait()
            else:
                cp.start()

        def body(e, carry):
            off = pad0_ref[e]
            npad = npad_ref[e]
            head = (SUBLANES - off % SUBLANES) % SUBLANES
            for j in range(SUBLANES - 1):
                @pl.when(j < head)
                def _(j=j):
                    go(pltpu.make_async_copy(zbuf.at[pl.ds(0, 1), :], xs_ref.at[pl.ds(off + j, 1), :], zsem))
            off = pl.multiple_of(off + head, SUBLANES)
            rem = npad - head
            bit = pl.next_power_of_2(tm_e) // 2
            while bit >= SUBLANES:
                on = (rem & bit) != 0

                @pl.when(on)
                def _(off=off, bit=bit):
                    go(pltpu.make_async_copy(zbuf.at[pl.ds(0, bit), :], xs_ref.at[pl.ds(off, bit), :], zsem))

                off = pl.multiple_of(off + jnp.where(on, bit, 0), SUBLANES)
                bit //= 2
            return carry
        lax.fori_loop(0, N_EXPERTS, body, 0)

        def tail(tb, carry):
            go(pltpu.make_async_copy(zbuf, xs_ref.at[pl.ds(pl.multiple_of(tb * tm_e, tm_e), tm_e), :], zsem))
            return carry
        lax.fori_loop(nb_ref[0], nblk, tail, 0)

    @pl.when(i == 0)
    def _():
        zbuf[...] = jnp.zeros(zbuf.shape, zbuf.dtype)
        pad_fill(False)

    for r in range(tm):
        for kk in range(2):
            d = _slot_of(eid_ref, rank_ref, pstart_ref, kk * n_tok + i * tm + r)
            pltpu.make_async_copy(h_ref.at[pl.ds(r, 1), :], xs_ref.at[pl.ds(d, 1), :], sem).start(
                priority=(r + kk) % 2)

    @pl.when(i == 0)
    def _():
        pad_fill(True)

    for _ in range(2 * tm):
        pltpu.make_async_copy(h_ref.at[pl.ds(0, 1), :], xs_ref.at[pl.ds(0, 1), :], sem).wait()


def _dispatch(h, eid, rank, p_start, pad0, npad, nb_used, nblk, tm, tm_e):
    n, d = h.shape
    return pl.pallas_call(
        functools.partial(_dispatch_body, tm=tm, tm_e=tm_e, nblk=nblk),
        grid_spec=pltpu.PrefetchScalarGridSpec(
            num_scalar_prefetch=6,
            grid=(n // tm,),
            in_specs=[pl.BlockSpec((tm, d), lambda i, *_: (i, 0))],
            out_specs=pl.BlockSpec(memory_space=pl.ANY),
            scratch_shapes=[pltpu.VMEM((tm_e, d), h.dtype),
                            pltpu.SemaphoreType.DMA(()),
                            pltpu.SemaphoreType.DMA(())]),
        out_shape=jax.ShapeDtypeStruct((nblk * tm_e, d), h.dtype),
        compiler_params=_cparams(("arbitrary",)),
        name="moe_dispatch",
    )(eid, rank, p_start, pad0, npad, nb_used, h)


def _experts_body(be_ref, nb_ref, first_ref, nxt_ref, slot_ref, x_ref, w1_hbm, w3_hbm, w2_hbm, y_ref,
                  wb1, wb3, wb2, sem, *, layer):
    b = pl.program_id(0)
    used = b < nb_ref[0]
    s = slot_ref[b]

    def weight_copies(e, slot):
        return (pltpu.make_async_copy(w1_hbm.at[layer, e], wb1.at[slot], sem.at[slot, 0]),
                pltpu.make_async_copy(w3_hbm.at[layer, e], wb3.at[slot], sem.at[slot, 1]),
                pltpu.make_async_copy(w2_hbm.at[layer, e], wb2.at[slot], sem.at[slot, 2]))

    @pl.when(b == 0)
    def _():
        for cp in weight_copies(be_ref[0], 0):
            cp.start()

    @pl.when(jnp.logical_and(used, first_ref[b] == 1))
    def _():
        for cp in weight_copies(be_ref[b], s):
            cp.wait()

        @pl.when(nxt_ref[b] >= 0)
        def _():
            for cp in weight_copies(nxt_ref[b], 1 - s):
                cp.start()

    @pl.when(used)
    def _():
        x = _unpack_halves(x_ref[...])
        a = _dot(x, wb1[s])
        g = _dot(x, wb3[s])
        y_ref[...] = _pack_halves(_dot(_silu(a) * g, wb2[s]))

    @pl.when(jnp.logical_not(used))
    def _():
        y_ref[...] = jnp.zeros(y_ref.shape, y_ref.dtype)


def _experts(xs, block_e, nb_used, first, nxt_e, slot, w1, w3, w2, layer, tm):
    s, pw = xs.shape
    d, f = w1.shape[-2], w1.shape[-1]
    nblk = s // tm
    return pl.pallas_call(
        functools.partial(_experts_body, layer=layer),
        grid_spec=pltpu.PrefetchScalarGridSpec(
            num_scalar_prefetch=5,
            grid=(nblk,),
            in_specs=[pl.BlockSpec((tm, pw), lambda b, be, nb, *_: (jnp.minimum(b, nb[0] - 1), 0)),
                      pl.BlockSpec(memory_space=pl.ANY),
                      pl.BlockSpec(memory_space=pl.ANY),
                      pl.BlockSpec(memory_space=pl.ANY)],
            out_specs=pl.BlockSpec((tm, pw), lambda b, *_: (b, 0)),
            scratch_shapes=[pltpu.VMEM((2, d, f), F32),
                            pltpu.VMEM((2, d, f), F32),
                            pltpu.VMEM((2, f, d), F32),
                            pltpu.SemaphoreType.DMA((2, 3))]),
        out_shape=jax.ShapeDtypeStruct((s, pw), xs.dtype),
        compiler_params=_cparams(("arbitrary",)),
        name="moe_experts",
    )(block_e, nb_used, first, nxt_e, slot, xs, w1, w3, w2)


def _combine_body(eid_ref, rank_ref, pstart_ref, x_ref, meta_ref, gfin_ref, ys_ref, *rest, tm, n_first, final):
    if final:
        o_a_ref, o_b_ref, buf, sem = rest
    else:
        o_a_ref, buf, sem = rest
        o_b_ref = None
    i = pl.program_id(0)
    nsteps = pl.num_programs(0)

    def issue(step, slot):
        for r in range(tm):
            for kk in range(2):
                d = _slot_of(eid_ref, rank_ref, pstart_ref, kk * (nsteps * tm) + step * tm + r)
                pltpu.make_async_copy(ys_ref.at[pl.ds(d, 1), :], buf.at[slot, kk, pl.ds(r, 1), :],
                                      sem.at[slot]).start(priority=(r + kk) % 2)

    @pl.when(i == 0)
    def _():
        issue(0, 0)

    @pl.when(i + 1 < nsteps)
    def _():
        issue(i + 1, (i + 1) % 2)

    slot = i % 2
    for _ in range(2 * tm):
        pltpu.make_async_copy(ys_ref.at[pl.ds(0, 1), :], buf.at[slot, 0, pl.ds(0, 1), :], sem.at[slot]).wait()

    meta = meta_ref[...]
    g0 = meta[:, _META_G0:_META_G0 + 1]
    g1 = meta[:, _META_G1:_META_G1 + 1]
    out = x_ref[...] + (_unpack_halves(buf[slot, 0]) * g0 + _unpack_halves(buf[slot, 1]) * g1)
    if not final:
        o_a_ref[...] = out
    else:
        out = _rms(out, gfin_ref[...])

        @pl.when(i < n_first)
        def _():
            o_a_ref[...] = out

        @pl.when(i >= n_first)
        def _():
            o_b_ref[...] = out


def _combine(x, meta, eid, rank, p_start, ys, gfin, tm, n_first_rows, final):
    n, d = x.shape
    n_first = n_first_rows // tm
    if final:
        out_specs = [pl.BlockSpec((tm, d), lambda i, *_: (jnp.minimum(i, n_first - 1), 0)),
                     pl.BlockSpec((tm, d), lambda i, *_: (jnp.maximum(i - n_first, 0), 0))]
        out_shape = [jax.ShapeDtypeStruct((n_first_rows, d), F32),
                     jax.ShapeDtypeStruct((n - n_first_rows, d), F32)]
    else:
        out_specs = [pl.BlockSpec((tm, d), lambda i, *_: (i, 0))]
        out_shape = [jax.ShapeDtypeStruct((n, d), F32)]
    return pl.pallas_call(
        functools.partial(_combine_body, tm=tm, n_first=n_first, final=final),
        grid_spec=pltpu.PrefetchScalarGridSpec(
            num_scalar_prefetch=3,
            grid=(n // tm,),
            in_specs=[pl.BlockSpec((tm, d), lambda i, *_: (i, 0)),
                      pl.BlockSpec((tm, LANES), lambda i, *_: (i, 0)),
                      pl.BlockSpec((1, d), lambda i, *_: (0, 0)),
                      pl.BlockSpec(memory_space=pl.ANY)],
            out_specs=out_specs,
            scratch_shapes=[pltpu.VMEM((2, 2, tm, ys.shape[1]), ys.dtype),
                            pltpu.SemaphoreType.DMA((2,))]),
        out_shape=out_shape,
        compiler_params=_cparams(("arbitrary",)),
        name="moe_combine_final" if final else "moe_combine",
    )(eid, rank, p_start, x, meta, gfin.reshape(1, d), ys)


def _moe(x1, h, logits, rbias, w1, w3, w2, layer, gfin, n_first_rows, final, tm_e=192, tm_t=256):
    n, d = h.shape
    meta, idx, cnt = _route(logits, rbias, 256)
    idx = idx.astype(jnp.int32)
    eid = idx[:, 0:2, :].transpose(1, 0, 2).reshape(-1)
    rank = idx[:, 2:4, :].transpose(1, 0, 2).reshape(-1)
    counts = cnt[0, _EXP_LANE0:_EXP_LANE0 + N_EXPERTS].astype(jnp.int32)
    padded = (counts + tm_e - 1) // tm_e * tm_e
    p_end = jnp.cumsum(padded)
    p_start = (p_end - padded).astype(jnp.int32)
    s = n * 2
    nblk = (s + N_EXPERTS * (tm_e - 1) + tm_e - 1) // tm_e
    bidx = jnp.arange(nblk, dtype=jnp.int32)
    block_e = jnp.minimum(jnp.sum((p_end[None, :] <= (bidx * tm_e)[:, None]).astype(jnp.int32), axis=1),
                          N_EXPERTS - 1).astype(jnp.int32)
    nb_used = (p_end[-1:] // tm_e).astype(jnp.int32)
    prev_e = jnp.concatenate([jnp.full((1,), -1, jnp.int32), block_e[:-1]])
    first = ((bidx < nb_used[0]) & (block_e != prev_e)).astype(jnp.int32)
    slot = ((jnp.cumsum(first) - 1) % 2).astype(jnp.int32)
    eidx = jnp.arange(N_EXPERTS, dtype=jnp.int32)
    live = jnp.where(counts > 0, eidx, N_EXPERTS)
    next_live = jnp.min(jnp.where(eidx[None, :] > eidx[:, None], live[None, :], N_EXPERTS), axis=1)
    nxt_e = jnp.sum(jnp.where(block_e[:, None] == eidx[None, :], next_live[None, :], 0), axis=1)
    nxt_e = jnp.where(nxt_e < N_EXPERTS, nxt_e, -1).astype(jnp.int32)
    xs = _dispatch(h, eid, rank, p_start, (p_start + counts).astype(jnp.int32),
                   (padded - counts).astype(jnp.int32), nb_used, nblk, tm_t, tm_e)
    ys = _experts(xs, block_e, nb_used, first, nxt_e, slot, w1, w3, w2, layer, tm_e)
    return _combine(x1, meta, eid, rank, p_start, ys, gfin, tm_t, n_first_rows, final)


def _lambda_value(lam_ref, lambda_init):
    lam = lam_ref[...]
    s1 = jnp.sum(lam[0:1] * lam[1:2], axis=1, keepdims=True)
    s2 = jnp.sum(lam[2:3] * lam[3:4], axis=1, keepdims=True)
    return jnp.exp(s1) - jnp.exp(s2) + lambda_init


def _lane_tile(x, width):
    if width % LANES == 0:
        return jnp.concatenate([x] * (width // LANES), axis=1)
    return x[:, :width]


def _softmax_step(j, s, v, m_ref, l_ref, acc_ref, rows=slice(None)):
    m_prev = m_ref[j, rows]
    m_new = jnp.maximum(m_prev, jnp.max(s, axis=1, keepdims=True))
    alpha = jnp.exp(m_prev - m_new)
    p = jnp.exp(s - _lane_tile(m_new, s.shape[1]))
    l_ref[j, rows] = alpha * l_ref[j, rows] + jnp.sum(p, axis=1, keepdims=True)
    acc = acc_ref[j, rows]
    acc_ref[j, rows] = acc * _lane_tile(alpha, acc.shape[1]) + _dot(p, v)
    m_ref[j, rows] = m_new


def _attn_finish(lam_ref, g_ref, l_ref, acc_ref, lambda_init):
    lam = _lambda_value(lam_ref, lambda_init)
    width = acc_ref.shape[-1]
    o = acc_ref[0] / _lane_tile(l_ref[0], width) - lam * (acc_ref[1] / _lane_tile(l_ref[1], width))
    return _rms(o, g_ref[...]) * (1.0 - lambda_init)


def _alibi_slopes():
    slopes = [2.0 ** (-8.0 * (h + 1) / DIFF_HEADS) for h in range(DIFF_HEADS)]
    assert all(math.frexp(s)[0] == 0.5 for s in slopes)
    return slopes


def _attn_prompt_body(iq_ref, ik_ref, slope_ref, q_ref, k_ref, v_ref, kt_ref, lam_ref, g_ref,
                      o_ref, qa_ref, m_ref, l_ref, acc_ref, *, tq, lambda_init):
    hg = pl.program_id(1)
    t = pl.program_id(2)
    iq = iq_ref[t]
    ik = ik_ref[t]
    dh = DIFF_DH
    hw = 2 * dh
    slopes = [slope_ref[hg * ATTN_HEADS_PER_STEP + hh] for hh in range(ATTN_HEADS_PER_STEP)]

    @pl.when(ik == 0)
    def _():
        m_ref[...] = jnp.full(m_ref.shape, -jnp.inf, F32)
        l_ref[...] = jnp.zeros(l_ref.shape, F32)
        acc_ref[...] = jnp.zeros(acc_ref.shape, F32)
        qpos = iq * tq + lax.broadcasted_iota(jnp.int32, (tq, LANES), 0)
        lane = lax.broadcasted_iota(jnp.int32, (tq, LANES), 1)
        qa = (qpos // CHUNK).astype(F32)
        qb = (qpos % CHUNK).astype(F32)
        for hh, slope in enumerate(slopes):
            feat = jnp.where(lane == 0, qa * (-slope * CHUNK),
                             jnp.where(lane == 1, qb * (-slope), jnp.where(lane < 4, slope, 0.0)))
            for j in range(2):
                c0 = hh * hw + j * dh
                qa_ref[hh, j] = jnp.concatenate([q_ref[:, c0:c0 + dh] * (dh ** -0.5), feat], axis=1)

    kt = kt_ref[...]
    ts = tq // ATTN_ROW_SPLIT

    def keys(hh, j, nkeys):
        c0 = hh * hw + j * dh
        return jnp.concatenate([k_ref[0:nkeys, c0:c0 + dh], kt[:nkeys]], axis=1)

    @pl.when(ik < iq)
    def _():
        for hh in range(ATTN_HEADS_PER_STEP):
            v = v_ref[:, hh * hw:(hh + 1) * hw]
            for r in range(ATTN_ROW_SPLIT):
                rows = slice(r * ts, (r + 1) * ts)
                for j in range(2):
                    _softmax_step(j, _dot_nt(qa_ref[hh, j, rows], keys(hh, j, tq)), v,
                                  m_ref.at[hh], l_ref.at[hh], acc_ref.at[hh], rows)

    @pl.when(ik == iq)
    def _():
        outs = []
        for hh, slope in enumerate(slopes):
            for r in range(ATTN_ROW_SPLIT):
                rows = slice(r * ts, (r + 1) * ts)
                nkeys = (r + 1) * ts
                row = r * ts + lax.broadcasted_iota(jnp.int32, (ts, nkeys), 0)
                col = lax.broadcasted_iota(jnp.int32, (ts, nkeys), 1)
                fix = jnp.maximum(col - row, 0).astype(F32) * (-2.0 * slope)
                vis = (col // CHUNK) <= (row // CHUNK)
                v = v_ref[0:nkeys, hh * hw:(hh + 1) * hw]
                for j in range(2):
                    s = _dot_nt(qa_ref[hh, j, rows], keys(hh, j, nkeys))
                    _softmax_step(j, jnp.where(vis, s + fix, -jnp.inf), v,
                                  m_ref.at[hh], l_ref.at[hh], acc_ref.at[hh], rows)
            outs.append(_attn_finish(lam_ref, g_ref.at[hh], l_ref.at[hh], acc_ref.at[hh], lambda_init))
        o_ref[...] = _pack_groups(jnp.concatenate(outs, axis=1))


def _attn_prompt(q, k, v, nb, t, lam, g_subln, lambda_init, tq):
    hw = 2 * DIFF_DH
    assert t % tq == 0 and (tq // ATTN_ROW_SPLIT) % CHUNK == 0 and t // CHUNK <= 256
    nq = t // tq
    pairs = [(iq, ik) for iq in range(nq) for ik in range(iq + 1)]
    iq_tab = jnp.asarray(np.array([p[0] for p in pairs], np.int32))
    ik_tab = jnp.asarray(np.array([p[1] for p in pairs], np.int32))
    slopes = jnp.asarray(np.array(_alibi_slopes(), np.float32))
    pos = np.arange(t)
    ktab = np.zeros((t, LANES), np.float32)
    ktab[:, 0] = 1.0
    ktab[:, 1] = 1.0
    ktab[:, 2] = (pos // CHUNK) * CHUNK
    ktab[:, 3] = pos % CHUNK
    g3 = g_subln.reshape(DIFF_HEADS, 1, hw)
    hps = ATTN_HEADS_PER_STEP
    assert DIFF_HEADS % hps == 0
    bw = hps * hw
    assert bw % PACK_GROUP == 0
    op = _packed_row(bw, _pack_groups)
    return pl.pallas_call(
        functools.partial(_attn_prompt_body, tq=tq, lambda_init=lambda_init),
        grid_spec=pltpu.PrefetchScalarGridSpec(
            num_scalar_prefetch=3,
            grid=(nb, DIFF_HEADS // hps, len(pairs)),
            in_specs=[pl.BlockSpec((tq, bw), lambda b, h, p, iqt, ikt, sl: (b * nq + iqt[p], h)),
                      pl.BlockSpec((tq, bw), lambda b, h, p, iqt, ikt, sl: (b * nq + ikt[p], h)),
                      pl.BlockSpec((tq, bw), lambda b, h, p, iqt, ikt, sl: (b * nq + ikt[p], h)),
                      pl.BlockSpec((tq, LANES), lambda b, h, p, iqt, ikt, sl: (ikt[p], 0)),
                      pl.BlockSpec((4, DIFF_DH), lambda b, h, p, iqt, ikt, sl: (0, 0)),
                      pl.BlockSpec((hps, 1, hw), lambda b, h, p, iqt, ikt, sl: (h, 0, 0))],
            out_specs=pl.BlockSpec((tq, op.shape[1]), lambda b, h, p, iqt, ikt, sl: (b * nq + iqt[p], h)),
            scratch_shapes=[pltpu.VMEM((hps, 2, tq, hw), F32),
                            pltpu.VMEM((hps, 2, tq, LANES), F32),
                            pltpu.VMEM((hps, 2, tq, LANES), F32),
                            pltpu.VMEM((hps, 2, tq, hw), F32)]),
        out_shape=jax.ShapeDtypeStruct((nb * t, op.shape[1] * (DIFF_HEADS // hps)), op.dtype),
        compiler_params=_cparams(("arbitrary", "arbitrary", "arbitrary")),
        name="diff_attn_prompt",
    )(iq_tab, ik_tab, slopes, q, k, v, jnp.asarray(ktab), lam, g3)


def _attn_sample_body(q_ref, kc_hbm, vc_hbm, kn_ref, vn_ref, lam_ref, g_ref,
                      o_ref, kbuf, vbuf, sem, m_ref, l_ref, acc_ref, *, li, tk, past, lambda_init):
    b = pl.program_id(0)
    ik = pl.program_id(1)
    nb = pl.num_programs(0)
    nkb = pl.num_programs(1) - 1
    dh = DIFF_DH
    hw = 2 * dh
    tq = q_ref.shape[0]
    scale = dh ** -0.5

    def fetch_copies(f):
        fb, fk = f // nkb, f % nkb
        slot = f % 2
        cps = []
        for h in range(DIFF_HEADS):
            for hbm, buf, c in ((kc_hbm, kbuf, 0), (vc_hbm, vbuf, 1)):
                cps.append(pltpu.make_async_copy(hbm.at[li, fb, pl.ds(fk * tk, tk), h, :], buf.at[slot, h],
                                                 sem.at[slot, c]))
        return cps

    fcur = b * nkb + ik

    @pl.when(jnp.logical_and(b == 0, ik == 0))
    def _():
        for cp in fetch_copies(0):
            cp.start()

    @pl.when(ik < nkb)
    def _():
        @pl.when(fcur + 1 < nb * nkb)
        def _():
            for cp in fetch_copies(fcur + 1):
                cp.start()

        for cp in fetch_copies(fcur):
            cp.wait()

    @pl.when(ik == 0)
    def _():
        m_ref[...] = jnp.full(m_ref.shape, -jnp.inf, F32)
        l_ref[...] = jnp.zeros(l_ref.shape, F32)
        acc_ref[...] = jnp.zeros(acc_ref.shape, F32)

    def attend(h, k, v, k0, width):
        slope = _alibi_slopes()[h]
        q = q_ref[:, h * hw:(h + 1) * hw]
        qpos = past + lax.broadcasted_iota(jnp.int32, (tq, width), 0)
        kpos = k0 + lax.broadcasted_iota(jnp.int32, (tq, width), 1)
        bias = jnp.abs(qpos - kpos).astype(F32) * (-slope)
        vis = (kpos // CHUNK) <= (qpos // CHUNK)
        for j in range(2):
            s = _dot_nt(q[:, j * dh:(j + 1) * dh], k[:, j * dh:(j + 1) * dh]) * scale + bias
            s = jnp.where(vis, s, -jnp.inf)
            _softmax_step(j, s, v, m_ref.at[h], l_ref.at[h], acc_ref.at[h])

    @pl.when(ik < nkb)
    def _():
        slot = fcur % 2
        for h in range(DIFF_HEADS):
            attend(h, kbuf[slot, h], vbuf[slot, h], ik * tk, tk)

    @pl.when(ik == nkb)
    def _():
        outs = []
        for h in range(DIFF_HEADS):
            hs = slice(h * hw, (h + 1) * hw)
            attend(h, kn_ref[:, hs], vn_ref[:, hs], past, tq)
            outs.append(_attn_finish(lam_ref, g_ref.at[h], l_ref.at[h], acc_ref.at[h], lambda_init))
        o_ref[...] = _pack_groups(jnp.concatenate(outs, axis=1))


def _attn_sample(q, k, v, cache_k, cache_v, li, lam, g_subln, lambda_init, tk):
    n_rows = q.shape[0]
    _, nb, past, nh, hw = cache_k.shape
    aw = nh * hw
    tq = n_rows // nb
    tk = min(tk, past)
    assert past % tk == 0
    nkb = past // tk
    g3 = g_subln.reshape(nh, 1, hw)
    cache_spec = pl.BlockSpec(memory_space=pl.ANY)
    row_spec = pl.BlockSpec((tq, aw), lambda b, ik: (b, 0))
    op = _packed_row(aw, _pack_groups)
    return pl.pallas_call(
        functools.partial(_attn_sample_body, li=li, tk=tk, past=past, lambda_init=lambda_init),
        grid=(nb, nkb + 1),
        in_specs=[row_spec,
                  cache_spec, cache_spec,
                  row_spec,
                  row_spec,
                  pl.BlockSpec((4, DIFF_DH), lambda b, ik: (0, 0)),
                  pl.BlockSpec((nh, 1, hw), lambda b, ik: (0, 0, 0))],
        out_specs=pl.BlockSpec((tq, op.shape[1]), lambda b, ik: (b, 0)),
        scratch_shapes=[pltpu.VMEM((2, nh, tk, hw), F32),
                        pltpu.VMEM((2, nh, tk, hw), F32),
                        pltpu.SemaphoreType.DMA((2, 2)),
                        pltpu.VMEM((nh, 2, tq, LANES), F32),
                        pltpu.VMEM((nh, 2, tq, LANES), F32),
                        pltpu.VMEM((nh, 2, tq, hw), F32)],
        out_shape=jax.ShapeDtypeStruct((n_rows, op.shape[1]), op.dtype),
        compiler_params=_cparams(("arbitrary", "arbitrary")),
        name="diff_attn_sample",
    )(q, cache_k, cache_v, k, v, lam, g3)


def _pad_cols(a, width):
    return jnp.pad(a, ((0, 0), (0, width - a.shape[1])))


def _mixer_constants(dk, dv, w_gate_up, b_gate, g_gla, conv_w, conv_b, dt_bias, a_log, d_skip, g_ssd):
    C = REC_ROWS
    nh = GLA_HEADS
    inner = g_ssd.shape[0]
    n_ssd = inner // SSD_HEADDIM
    wup = jnp.pad(w_gate_up, ((0, LANES - w_gate_up.shape[0]), (0, 0)))
    tri = jnp.asarray(np.tril(np.ones((C, C), np.float32)), BF16)
    e64 = np.zeros((LANES, inner), np.float32)
    ec = np.zeros((LANES, n_ssd * C), np.float32)
    for hh in range(n_ssd):
        e64[hh, hh * SSD_HEADDIM:(hh + 1) * SSD_HEADDIM] = 1.0
        ec[hh, hh * C:(hh + 1) * C] = 1.0
    eye = np.tile(np.eye(C, dtype=np.float32), (1, n_ssd))
    caus = np.tile(np.tril(np.ones((C, C), np.float32)), (1, n_ssd))
    return [wup, b_gate.reshape(1, -1), g_gla.reshape(nh, dv), conv_w, conv_b.reshape(1, -1),
            _pad_cols(dt_bias.reshape(1, -1), LANES),
            _pad_cols(-jnp.exp(a_log.astype(F32)).reshape(1, -1), LANES),
            jnp.repeat(d_skip, SSD_HEADDIM).reshape(1, -1), g_ssd.reshape(1, -1),
            tri, jnp.asarray(e64, BF16), jnp.asarray(ec, BF16), jnp.asarray(eye), jnp.asarray(caus)]


def kernel(x_prompt, x_sample, state_gla, state_ssm, state_conv, cache_k, cache_v, norm_mix, norm_ffn, norm_final, w_in, w_gate_up, b_gate, g_gla, conv_w, conv_b, dt_bias, a_log, d_skip, g_ssd, w_out_mix, w_qkv, lam_q1, lam_k1, lam_q2, lam_k2, g_subln, w_o, router_group_w, router_group_b, router_expert_w, router_expert_b, w1, w3, w2):
    bp, tp, d = x_prompt.shape
    bs, ts, _ = x_sample.shape
    n_p, n_s = bp * tp, bs * ts
    n = n_p + n_s
    depth = norm_mix.shape[0]
    nh = GLA_HEADS
    dk, dv = state_gla.shape[-2], state_gla.shape[-1]
    qk_w, vw = nh * dk, nh * dv
    n_ssd = state_ssm.shape[2]
    inner = n_ssd * SSD_HEADDIM
    hpg = n_ssd // SSD_GROUPS
    gs = SSD_GROUPS * SSD_STATE
    cw = inner + 2 * gs
    rank = w_gate_up.shape[1]
    assert w_gate_up.shape[2] == qk_w and rank <= LANES and n_ssd <= LANES

    x = jnp.concatenate([x_prompt.reshape(n_p, d), x_sample.reshape(n_s, d)], axis=0)
    tm_row = next(tm for tm in (1408, 1056, 768, 256) if n % tm == 0)

    i_even = i_odd = 0
    gla_p, ssm_p, conv_p, gla_s, ssm_s, conv_s = [], [], [], [], [], []
    k_p, v_p, k_s, v_s = [], [], [], []
    y_p = y_s = None
    for layer in range(depth):
        final = layer == depth - 1
        if layer % 2 == 0:
            i = i_even
            i_even += 1
            offs = np.cumsum([0, qk_w, qk_w, vw, rank, vw, inner, cw, n_ssd])
            w_t = w_in[i].T
            seg = lambda j: w_t[offs[j]:offs[j + 1]]
            pad_rows = lambda a: jnp.pad(a, ((0, LANES - a.shape[0]), (0, 0)))
            w_cat = jnp.concatenate([seg(0), seg(1), seg(2), seg(4), seg(5), seg(6),
                                     pad_rows(seg(3)), pad_rows(seg(7))], axis=0).astype(BF16)
            proj = _norm_matmul(x, norm_mix[layer], w_cat, tm_row, 1280, w_is_transposed=True)
            consts = _mixer_constants(dk, dv, w_gate_up[i], b_gate[i], g_gla[i], conv_w[i], conv_b[i],
                                      dt_bias[i], a_log[i], d_skip[i], g_ssd[i])

            def to_group_state(sm):
                b_ = sm.shape[0]
                return sm.reshape(b_, SSD_GROUPS, hpg, SSD_STATE, SSD_HEADDIM).transpose(0, 1, 3, 2, 4) \
                         .reshape(b_, SSD_GROUPS, SSD_STATE, hpg * SSD_HEADDIM)

            def from_group_state(sg):
                b_ = sg.shape[0]
                return sg.reshape(b_, SSD_GROUPS, SSD_STATE, hpg, SSD_HEADDIM).transpose(0, 1, 3, 2, 4) \
                         .reshape(b_, n_ssd, SSD_STATE, SSD_HEADDIM)

            def pad_conv(cv):
                return jnp.pad(cv, ((0, 0), (CONV_PAD - cv.shape[1], 0), (0, 0)))

            mix_a, g_fin, s_fin, c_fin = _mixer_call(
                proj, 0, bp, tp, jnp.zeros((bp, nh, dv, dk), F32),
                jnp.zeros((bp, SSD_GROUPS, SSD_STATE, hpg * SSD_HEADDIM), F32),
                jnp.zeros((bp, CONV_PAD, cw), F32), consts, dk, dv)
            gla_p.append(g_fin.transpose(0, 1, 3, 2))
            ssm_p.append(from_group_state(s_fin))
            conv_p.append(c_fin[:, CONV_PAD - (SSD_CONV - 1):])
            mix_b, g_fin, s_fin, c_fin = _mixer_call(
                proj, n_p, bs, ts, state_gla[i].transpose(0, 1, 3, 2),
                to_group_state(state_ssm[i]), pad_conv(state_conv[i]), consts, dk, dv)
            gla_s.append(g_fin.transpose(0, 1, 3, 2))
            ssm_s.append(from_group_state(s_fin))
            conv_s.append(c_fin[:, CONV_PAD - (SSD_CONV - 1):])
            w_proj = w_out_mix[i]
        else:
            i = i_odd
            i_odd += 1
            lambda_init = 0.8 - 0.6 * math.exp(-0.3 * layer)
            tm_p = next(tm for tm in (1024, 512, 256) if n_p % tm == 0)
            w_qkv_b = w_qkv[i].astype(BF16)
            q_a, k_a, v_a = _norm_matmul(x, norm_mix[layer], w_qkv_b, tm_p, 1024, nrows=n_p, parts=3)
            q_b, k_b, v_b = _norm_matmul(x, norm_mix[layer], w_qkv_b, n_s, 1024, row0=n_p, parts=3)
            lam = jnp.stack([lam_q1[i], lam_k1[i], lam_q2[i], lam_k2[i]])
            mix_a = _attn_prompt(q_a, k_a, v_a, bp, tp, lam, g_subln[i], lambda_init, min(512, tp))
            mix_b = _attn_sample(q_b, k_b, v_b, cache_k, cache_v, i, lam, g_subln[i], lambda_init, 1024)
            k_p.append(k_a.reshape(bp, tp, DIFF_HEADS, 2 * DIFF_DH))
            v_p.append(v_a.reshape(bp, tp, DIFF_HEADS, 2 * DIFF_DH))
            k_s.append(k_b.reshape(bs, ts, DIFF_HEADS, 2 * DIFF_DH))
            v_s.append(v_b.reshape(bs, ts, DIFF_HEADS, 2 * DIFF_DH))
            w_proj = w_o[i]

        wr = _pad_cols(jnp.concatenate([router_group_w[layer], router_expert_w[layer]], axis=1), LANES)
        rbias = _pad_cols(jnp.concatenate([router_group_b[layer], router_expert_b[layer]]).reshape(1, -1), LANES)
        x1, h2, logits = _proj_resid(mix_a, mix_b, w_proj, x, norm_ffn[layer], wr, 256)
        res = _moe(x1, h2, logits, rbias, w1, w3, w2, layer, norm_final, n_p, final)
        if final:
            y_p, y_s = res
        else:
            x = res[0]

    return (y_p.reshape(bp, tp, d), y_s.reshape(bs, ts, d),
            jnp.stack(gla_p), jnp.stack(ssm_p), jnp.stack(conv_p), jnp.stack(k_p), jnp.stack(v_p),
            jnp.stack(gla_s), jnp.stack(ssm_s), jnp.stack(conv_s), jnp.stack(k_s), jnp.stack(v_s))
```

```python
import functools
import math

import numpy as np
import jax
import jax.numpy as jnp
from jax import lax
from jax.experimental import pallas as pl
from jax.experimental.pallas import tpu as pltpu

F32 = jnp.float32
BF16 = jnp.bfloat16

EPS = 1e-6
CHUNK = 64
GLA_HEADS = 4
GLA_TAU = 16.0
SSD_HEADDIM = 64
SSD_STATE = 128
SSD_GROUPS = 4
SSD_CONV = 4
DIFF_HEADS = 8
DIFF_DH = 128
MOE_GROUPS = 4
MOE_PER_GROUP = 8
N_EXPERTS = MOE_GROUPS * MOE_PER_GROUP

LANES = 128
SUBLANES = 8
REC_ROWS = 128
STRIP = 16
CONV_PAD = 8
ATTN_ROW_SPLIT = 2
ATTN_HEADS_PER_STEP = 4
VMEM_LIMIT = 56 * 1024 * 1024


def _cparams(sem, vmem=VMEM_LIMIT):
    return pltpu.CompilerParams(dimension_semantics=sem, vmem_limit_bytes=vmem)


def _dot(a, b):
    return jnp.dot(a, b, preferred_element_type=F32)


def _dot_nt(a, b):
    return lax.dot_general(a, b, (((1,), (1,)), ((), ())), preferred_element_type=F32)


def _dot_tn(a, b):
    return lax.dot_general(a, b, (((0,), (0,)), ((), ())), preferred_element_type=F32)


def _split_hi_lo(a):
    hi = a.astype(BF16)
    lo = (a - hi.astype(F32)).astype(BF16)
    return hi, lo


def _exact_left(m, a):
    hi, lo = _split_hi_lo(a)
    return _dot(m, hi) + _dot(m, lo)


def _exact_right(a, m):
    hi, lo = _split_hi_lo(a)
    return _dot(hi, m) + _dot(lo, m)


def _silu(x):
    h = 0.5 * x
    return h * (1.0 + jnp.tanh(h))


def _log1p_exp_neg_abs(x):
    e = jnp.exp(-jnp.abs(x))
    u = 1.0 + e
    return jnp.where(u == 1.0, e, jnp.log(u) * (e / (u - 1.0)))


def _rms(x, g):
    return x * lax.rsqrt(jnp.mean(x * x, axis=-1, keepdims=True) + EPS) * g


def _pack_halves(x):
    w = x.shape[1] // 2
    lo = lax.bitcast_convert_type(x[:, :w].astype(BF16).astype(F32), jnp.uint32)
    hi = lax.bitcast_convert_type(x[:, w:].astype(BF16).astype(F32), jnp.uint32)
    return (lo >> 16) | (hi & jnp.uint32(0xFFFF0000))


def _unpack_halves(p):
    lo = lax.bitcast_convert_type(p << 16, F32)
    hi = lax.bitcast_convert_type(p & jnp.uint32(0xFFFF0000), F32)
    return jnp.concatenate([lo, hi], axis=1)


PACK_GROUP = 512


def _pack_groups(x):
    return jnp.concatenate([_pack_halves(x[:, c:c + PACK_GROUP]) for c in range(0, x.shape[1], PACK_GROUP)],
                           axis=1)


def _unpack_groups(p):
    step = _packed_row(PACK_GROUP).shape[1]
    return jnp.concatenate([_unpack_halves(p[:, c:c + step]) for c in range(0, p.shape[1], step)], axis=1)


def _packed_row(d, fn=None):
    return jax.eval_shape(fn or _pack_halves, jax.ShapeDtypeStruct((SUBLANES, d), F32))


def _norm_matmul_body(x_ref, g_ref, w_ref, *refs, w_is_transposed, parts):
    o_refs, h_ref = refs[:parts], refs[parts]
    j = pl.program_id(1)

    @pl.when(j == 0)
    def _():
        h_ref[...] = _rms(x_ref[...], g_ref[...]).astype(h_ref.dtype)

    res = (_dot_nt if w_is_transposed else _dot)(h_ref[...], w_ref[...])
    if parts == 1:
        o_refs[0][...] = res
    else:
        per = pl.num_programs(1) // parts
        for p in range(parts):
            @pl.when(j // per == p)
            def _(p=p):
                o_refs[p][...] = res


def _norm_matmul(x, g, w, tm, tn, *, row0=0, nrows=None, parts=1, w_is_transposed=False):
    n, d = x.shape
    nrows = n - row0 if nrows is None else nrows
    nout = w.shape[0] if w_is_transposed else w.shape[1]
    assert nrows % tm == 0 and row0 % tm == 0 and nout % (tn * parts) == 0
    blk0 = row0 // tm
    per = nout // tn // parts
    w_spec = (pl.BlockSpec((tn, d), lambda i, j: (j, 0)) if w_is_transposed
              else pl.BlockSpec((d, tn), lambda i, j: (0, j)))
    out_specs = [pl.BlockSpec((tm, tn), lambda i, j, p=p: (i, jnp.clip(j - p * per, 0, per - 1)))
                 for p in range(parts)]
    outs = pl.pallas_call(
        functools.partial(_norm_matmul_body, w_is_transposed=w_is_transposed, parts=parts),
        grid=(nrows // tm, nout // tn),
        in_specs=[pl.BlockSpec((tm, d), lambda i, j: (blk0 + i, 0), pipeline_mode=pl.Buffered(1)),
                  pl.BlockSpec((1, d), lambda i, j: (0, 0)),
                  w_spec],
        out_specs=out_specs,
        out_shape=[jax.ShapeDtypeStruct((nrows, nout // parts), F32)] * parts,
        scratch_shapes=[pltpu.VMEM((tm, d), w.dtype)],
        compiler_params=_cparams(("arbitrary", "arbitrary")),
        name="norm_matmul",
    )(x, g.reshape(1, d), w)
    return outs[0] if parts == 1 else outs


def _mixer_body(p_ref, gla0_ref, ssm0_ref, conv0_ref, wup_ref, bgate_ref, ggla_ref, convw_ref,
                convb_ref, dtb_ref, aneg_ref, dskip_ref, gssd_ref, tri_ref, e64_ref, ec_ref,
                eye_ref, caus_ref,
                o_ref, gla_ref, ssm_ref, ctail_ref,
                sg_ref, ss_ref, ext_ref, b_ref, *, rb, dk, dv):
    C = REC_ROWS
    c = pl.program_id(1)
    nc = pl.num_programs(1)
    nh = GLA_HEADS
    qk_w = nh * dk
    vw = nh * dv
    inner = vw
    gs = SSD_GROUPS * SSD_STATE
    o_q, o_k, o_v = 0, qk_w, 2 * qk_w
    o_r = o_v + vw
    o_z = o_r + vw
    o_x = o_z + inner
    o_g = o_x + inner + 2 * gs
    o_dt = o_g + LANES

    @pl.when(c == 0)
    def _init():
        sg_ref[...] = gla0_ref[...]
        ss_ref[...] = ssm0_ref[...]
        ext_ref[0:CONV_PAD, :] = conv0_ref[...]

    p = p_ref[...]
    if rb < C:
        p = jnp.concatenate([p, jnp.zeros((C - rb, p.shape[1]), F32)], axis=0)

    def rowmask(width):
        return lax.broadcasted_iota(jnp.int32, (C, width), 0) < rb

    q = p[:, o_q:o_q + qk_w] * (dk ** -0.5)
    k = p[:, o_k:o_k + qk_w]
    v = p[:, o_v:o_v + vw]
    r = p[:, o_r:o_r + vw]
    z = p[:, o_z:o_z + inner]
    xbc = p[:, o_x:o_x + inner + 2 * gs]
    glr = p[:, o_g:o_g + LANES]
    dtp = p[:, o_dt:o_dt + LANES]

    tri = tri_ref[...]

    zg = _dot(glr, wup_ref[...]) + bgate_ref[...]
    la = (jnp.minimum(zg, 0.0) - _log1p_exp_neg_abs(zg)) * (1.0 / GLA_TAU)
    if rb < C:
        la = jnp.where(rowmask(qk_w), la, 0.0)
    bcum = _exact_left(tri, la)
    b_ref[...] = bcum

    prow = lax.broadcasted_iota(jnp.int32, (STRIP, C), 0)
    pcol = lax.broadcasted_iota(jnp.int32, (STRIP, C), 1)
    o_heads = []
    for h in range(nh):
        hs = slice(h * dk, (h + 1) * dk)
        vs = slice(h * dv, (h + 1) * dv)
        bh = bcum[:, hs]
        qh = q[:, hs]
        kh = k[:, hs]
        vh = v[:, vs]
        strips = []
        for i in range(C // STRIP):
            r0 = i * STRIP
            if r0 >= rb:
                strips.append(jnp.zeros((STRIP, C), F32))
                continue
            if i == 0:
                ref_row = jnp.zeros((1, dk), F32)
            else:
                ref_row = b_ref[pl.ds(r0 - 1, 1), hs]
            q_i = qh[r0:r0 + STRIP] * jnp.exp(bh[r0:r0 + STRIP] - ref_row)
            n_i = r0 + STRIP
            k_i = kh[:n_i] * jnp.exp(ref_row - bh[:n_i])
            if n_i < C:
                k_i = jnp.concatenate([k_i, jnp.zeros((C - n_i, dk), F32)], axis=0)
            s_i = _dot_nt(q_i, k_i)
            strips.append(jnp.where(pcol <= prow + r0, s_i, 0.0))
        pmat = jnp.concatenate(strips, axis=0)
        b_last = b_ref[pl.ds(C - 1, 1), hs]
        st = sg_ref[h]
        o_h = _dot(pmat, vh) + _dot_nt(qh * jnp.exp(bh), st)
        k_st = kh * jnp.exp(b_last - bh)
        sg_ref[h] = st * jnp.exp(b_last) + _dot_tn(vh, k_st)
        o_h = _rms(o_h, ggla_ref[pl.ds(h, 1), :]) * _silu(r[:, vs])
        o_heads.append(o_h)
    o_a = jnp.concatenate(o_heads, axis=1)

    cw = inner + 2 * gs
    ext_ref[CONV_PAD:CONV_PAD + C, :] = xbc
    conv = convb_ref[...]
    for j in range(SSD_CONV):
        conv = conv + convw_ref[pl.ds(j, 1), :] * ext_ref[pl.ds(CONV_PAD - (SSD_CONV - 1) + j, C), :]
    xc = _silu(conv)
    xs = xc[:, :inner]
    bm = xc[:, inner:inner + gs]
    cm = xc[:, inner + gs:cw]
    dtv = dtp + dtb_ref[...]
    dt = jnp.maximum(dtv, 0.0) + _log1p_exp_neg_abs(dtv)
    ld = dt * aneg_ref[...]
    if rb < C:
        dt = jnp.where(rowmask(LANES), dt, 0.0)
        ld = jnp.where(rowmask(LANES), ld, 0.0)
    bs = _exact_left(tri, ld)
    e64 = e64_ref[...]
    dt64 = _exact_right(dt, e64)
    b64 = _exact_right(bs, e64)
    blast64 = b64[C - 1:C, :]
    xdt = xs * dt64
    bc = _exact_right(bs, ec_ref[...])
    br = jnp.sum(eye_ref[...] * bc, axis=0, keepdims=True)
    vis = caus_ref[...] > 0.0
    lf = jnp.where(vis, jnp.exp(jnp.where(vis, bc - br, 0.0)), 0.0)
    hpg = inner // SSD_HEADDIM // SSD_GROUPS
    gw = hpg * SSD_HEADDIM
    lane_head = lax.broadcasted_iota(jnp.int32, (C, gw), 1) // SSD_HEADDIM
    y_groups = []
    for g in range(SSD_GROUPS):
        cm_g = cm[:, g * SSD_STATE:(g + 1) * SSD_STATE]
        bm_g = bm[:, g * SSD_STATE:(g + 1) * SSD_STATE]
        gl = slice(g * gw, (g + 1) * gw)
        gmat = _dot_nt(cm_g, bm_g)
        a4 = jnp.concatenate([gmat] * hpg, axis=1) * lf[:, g * hpg * C:(g + 1) * hpg * C]
        xg = xdt[:, gl]
        x4 = jnp.concatenate([jnp.where(lane_head == hh, xg, 0.0) for hh in range(hpg)], axis=0)
        s_g = ss_ref[g]
        y_g = _dot(a4, x4) + _dot(cm_g, s_g) * jnp.exp(b64[:, gl])
        xw = xg * jnp.exp(blast64[:, gl] - b64[:, gl])
        ss_ref[g] = s_g * jnp.exp(blast64[:, gl]) + _dot_tn(bm_g, xw)
        y_groups.append(y_g)
    y = jnp.concatenate(y_groups, axis=1) + xs * dskip_ref[...]
    y = _rms(y * _silu(z), gssd_ref[...])

    o_full = jnp.concatenate([o_a, y], axis=1)
    o_ref[...] = _pack_groups(o_full[:rb])

    @pl.when(c == nc - 1)
    def _fin():
        gla_ref[...] = sg_ref[...]
        ssm_ref[...] = ss_ref[...]
        ctail_ref[...] = ext_ref[pl.ds(rb, CONV_PAD), :]

    ext_ref[0:CONV_PAD, :] = ext_ref[pl.ds(rb, CONV_PAD), :]


def _mixer_call(proj, row_off, nb, t, gla0, ssm0, conv0, consts, dk, dv):
    C = REC_ROWS
    rb = min(t, C)
    assert t % rb == 0 and row_off % rb == 0
    steps = t // rb
    width = proj.shape[1]
    nh = GLA_HEADS
    vw = nh * dv
    cw = conv0.shape[-1]
    blk0 = row_off // rb

    def full(a):
        nd = a.ndim
        return pl.BlockSpec(a.shape, lambda b, c, _n=nd: (0,) * _n)

    in_specs = [pl.BlockSpec((rb, width), lambda b, c: (blk0 + b * steps + c, 0)),
                pl.BlockSpec((None, nh, dv, dk), lambda b, c: (b, 0, 0, 0)),
                pl.BlockSpec((None, SSD_GROUPS, SSD_STATE, ssm0.shape[-1]), lambda b, c: (b, 0, 0, 0)),
                pl.BlockSpec((None, CONV_PAD, cw), lambda b, c: (b, 0, 0))]
    in_specs += [full(a) for a in consts]
    operands = [proj, gla0, ssm0, conv0, *consts]
    op = _packed_row(2 * vw, _pack_groups)
    out_specs = [pl.BlockSpec((rb, op.shape[1]), lambda b, c: (b * steps + c, 0)),
                 pl.BlockSpec((None, nh, dv, dk), lambda b, c: (b, 0, 0, 0)),
                 pl.BlockSpec((None, SSD_GROUPS, SSD_STATE, ssm0.shape[-1]), lambda b, c: (b, 0, 0, 0)),
                 pl.BlockSpec((None, CONV_PAD, cw), lambda b, c: (b, 0, 0))]
    out_shape = [jax.ShapeDtypeStruct((nb * t, op.shape[1]), op.dtype),
                 jax.ShapeDtypeStruct(gla0.shape, F32),
                 jax.ShapeDtypeStruct(ssm0.shape, F32),
                 jax.ShapeDtypeStruct((nb, CONV_PAD, cw), F32)]
    return pl.pallas_call(
        functools.partial(_mixer_body, rb=rb, dk=dk, dv=dv),
        grid=(nb, steps),
        in_specs=in_specs,
        out_specs=out_specs,
        out_shape=out_shape,
        scratch_shapes=[pltpu.VMEM((nh, dv, dk), F32),
                        pltpu.VMEM((SSD_GROUPS, SSD_STATE, ssm0.shape[-1]), F32),
                        pltpu.VMEM((C + CONV_PAD, cw), F32),
                        pltpu.VMEM((C, nh * dk), F32)],
        compiler_params=_cparams(("arbitrary", "arbitrary")),
        name="gla_ssd_mixer",
    )(*operands)


def _proj_resid_body(oa_ref, ob_ref, w_ref, x_ref, g_ref, wr_ref, rb_ref, ltri_ref, eye_ref,
                     x1_ref, h_ref, meta_ref, idx_ref, cnt_ref, base_ref, *, n_first):
    o = _unpack_groups(jnp.where(pl.program_id(0) < n_first, oa_ref[...], ob_ref[...]))
    x1 = x_ref[...] + _dot(o, w_ref[...])
    x1_ref[...] = x1
    h = _rms(x1, g_ref[...])
    h_ref[...] = _pack_halves(h)
    _route_tile(_dot(h, wr_ref[...]) + rb_ref[...], ltri_ref, eye_ref, meta_ref, idx_ref, cnt_ref, base_ref)


def _proj_resid(o_a, o_b, w, x, g, wr, rbias, tm):
    n, d = x.shape
    kin = o_a.shape[1]
    d_in = w.shape[0]
    assert o_a.shape[0] % tm == 0 and o_b.shape[0] % tm == 0
    n_first = o_a.shape[0] // tm
    hp = _packed_row(d)
    ltri = jnp.asarray(np.tril(np.ones((tm, tm), np.float32), -1), BF16)
    eye = jnp.asarray(np.eye(tm, dtype=np.float32))
    return pl.pallas_call(
        functools.partial(_proj_resid_body, n_first=n_first),
        grid=(n // tm,),
        in_specs=[pl.BlockSpec((tm, kin), lambda i: (jnp.minimum(i, n_first - 1), 0)),
                  pl.BlockSpec((tm, kin), lambda i: (jnp.maximum(i - n_first, 0), 0)),
                  pl.BlockSpec((d_in, d), lambda i: (0, 0), pipeline_mode=pl.Buffered(1)),
                  pl.BlockSpec((tm, d), lambda i: (i, 0)),
                  pl.BlockSpec((1, d), lambda i: (0, 0)),
                  pl.BlockSpec((d, LANES), lambda i: (0, 0)),
                  pl.BlockSpec((1, LANES), lambda i: (0, 0)),
                  pl.BlockSpec((tm, tm), lambda i: (0, 0)),
                  pl.BlockSpec((tm, tm), lambda i: (0, 0))],
        out_specs=[pl.BlockSpec((tm, d), lambda i: (i, 0)),
                   pl.BlockSpec((tm, hp.shape[1]), lambda i: (i, 0)),
                   pl.BlockSpec((tm, LANES), lambda i: (i, 0)),
                   pl.BlockSpec((None, SUBLANES, tm), lambda i: (i, 0, 0)),
                   pl.BlockSpec((8, LANES), lambda i: (0, 0))],
        out_shape=[jax.ShapeDtypeStruct((n, d), F32),
                   jax.ShapeDtypeStruct((n, hp.shape[1]), hp.dtype),
                   jax.ShapeDtypeStruct((n, LANES), F32),
                   jax.ShapeDtypeStruct((n // tm, SUBLANES, tm), F32),
                   jax.ShapeDtypeStruct((8, LANES), F32)],
        scratch_shapes=[pltpu.VMEM((8, LANES), F32)],
        compiler_params=_cparams(("arbitrary",)),
        name="proj_resid_norm_route",
    )(o_a, o_b, w, x, g.reshape(1, d), wr, rbias, ltri, eye)


_META_E0, _META_E1, _META_G0, _META_G1, _META_R0, _META_R1 = range(6)
_EXP_LANE0 = MOE_GROUPS


def _route_tile(lg, ltri_ref, eye_ref, meta_ref, idx_ref, cnt_ref, base_ref):
    i = pl.program_id(0)

    @pl.when(i == 0)
    def _():
        base_ref[...] = jnp.zeros_like(base_ref)

    tm = lg.shape[0]
    lane_i = lax.broadcasted_iota(jnp.int32, (tm, LANES), 1)
    lane = lane_i.astype(F32)
    neg = -jnp.inf
    glog = jnp.where(lane_i < MOE_GROUPS, lg, neg)
    gmax = jnp.max(glog, axis=1, keepdims=True)
    gsel = jnp.min(jnp.where(glog == gmax, lane, float(LANES)), axis=1, keepdims=True)
    pg = 1.0 / jnp.sum(jnp.exp(glog - gmax), axis=1, keepdims=True)
    lo = _EXP_LANE0 + MOE_PER_GROUP * gsel
    el = jnp.where((lane >= lo) & (lane < lo + MOE_PER_GROUP), lg, neg)
    v1 = jnp.max(el, axis=1, keepdims=True)
    i1 = jnp.min(jnp.where(el == v1, lane, float(LANES)), axis=1, keepdims=True)
    el2 = jnp.where(lane == i1, neg, el)
    v2 = jnp.max(el2, axis=1, keepdims=True)
    i2 = jnp.min(jnp.where(el2 == v2, lane, float(LANES)), axis=1, keepdims=True)
    e = jnp.exp(v2 - v1)
    g1 = pg / (1.0 + e)
    g2 = pg * e / (1.0 + e)
    hot1 = lane == i1
    hot2 = lane == i2
    onehot = jnp.where(hot1 | hot2, 1.0, 0.0)
    before = _dot(ltri_ref[...], onehot.astype(BF16)) + base_ref[0:1, :]
    r1 = jnp.sum(jnp.where(hot1, before, 0.0), axis=1, keepdims=True)
    r2 = jnp.sum(jnp.where(hot2, before, 0.0), axis=1, keepdims=True)
    base_ref[0:1, :] = base_ref[0:1, :] + jnp.sum(onehot, axis=0, keepdims=True)
    meta = jnp.zeros((tm, LANES), F32)
    for idx, val in ((_META_E0, i1 - _EXP_LANE0), (_META_E1, i2 - _EXP_LANE0),
                     (_META_G0, g1), (_META_G1, g2), (_META_R0, r1), (_META_R1, r2)):
        meta = jnp.where(lane_i == idx, val, meta)
    meta_ref[...] = meta

    eye = eye_ref[...]
    rows = [jnp.sum(eye * col, axis=0, keepdims=True)
            for col in (i1 - _EXP_LANE0, i2 - _EXP_LANE0, r1, r2)]
    idx_ref[...] = jnp.concatenate(rows + [jnp.zeros((SUBLANES - len(rows), tm), F32)], axis=0)

    @pl.when(i == pl.num_programs(0) - 1)
    def _():
        cnt_ref[...] = base_ref[...]


def _slot_of(eid_ref, rank_ref, pstart_ref, idx):
    return pstart_ref[eid_ref[idx]] + rank_ref[idx]


def _dispatch_body(eid_ref, rank_ref, pstart_ref, pad0_ref, npad_ref, nb_ref, h_ref, xs_ref, zbuf, sem, zsem,
                   *, tm, tm_e, nblk):
    i = pl.program_id(0)
    n_tok = pl.num_programs(0) * tm

    def pad_fill(wait):
        def go(cp):
            if wait:
                cp.wait()
            else:
                cp.start()

        def body(e, carry):
            off = pad0_ref[e]
            npad = npad_ref[e]
            head = (SUBLANES - off % SUBLANES) % SUBLANES
            for j in range(SUBLANES - 1):
                @pl.when(j < head)
                def _(j=j):
                    go(pltpu.make_async_copy(zbuf.at[pl.ds(0, 1), :], xs_ref.at[pl.ds(off + j, 1), :], zsem))
            off = pl.multiple_of(off + head, SUBLANES)
            rem = npad - head
            bit = pl.next_power_of_2(tm_e) // 2
            while bit >= SUBLANES:
                on = (rem & bit) != 0

                @pl.when(on)
                def _(off=off, bit=bit):
                    go(pltpu.make_async_copy(zbuf.at[pl.ds(0, bit), :], xs_ref.at[pl.ds(off, bit), :], zsem))

                off = pl.multiple_of(off + jnp.where(on, bit, 0), SUBLANES)
                bit //= 2
            return carry
        lax.fori_loop(0, N_EXPERTS, body, 0)

        def tail(tb, carry):
            go(pltpu.make_async_copy(zbuf, xs_ref.at[pl.ds(pl.multiple_of(tb * tm_e, tm_e), tm_e), :], zsem))
            return carry
        lax.fori_loop(nb_ref[0], nblk, tail, 0)

    @pl.when(i == 0)
    def _():
        zbuf[...] = jnp.zeros(zbuf.shape, zbuf.dtype)
        pad_fill(False)

    for r in range(tm):
        for kk in range(2):
            d = _slot_of(eid_ref, rank_ref, pstart_ref, kk * n_tok + i * tm + r)
            pltpu.make_async_copy(h_ref.at[pl.ds(r, 1), :], xs_ref.at[pl.ds(d, 1), :], sem).start(
                priority=(r + kk) % 2)

    @pl.when(i == 0)
    def _():
        pad_fill(True)

    for _ in range(2 * tm):
        pltpu.make_async_copy(h_ref.at[pl.ds(0, 1), :], xs_ref.at[pl.ds(0, 1), :], sem).wait()


def _dispatch(h, eid, rank, p_start, pad0, npad, nb_used, nblk, tm, tm_e):
    n, d = h.shape
    return pl.pallas_call(
        functools.partial(_dispatch_body, tm=tm, tm_e=tm_e, nblk=nblk),
        grid_spec=pltpu.PrefetchScalarGridSpec(
            num_scalar_prefetch=6,
            grid=(n // tm,),
            in_specs=[pl.BlockSpec((tm, d), lambda i, *_: (i, 0))],
            out_specs=pl.BlockSpec(memory_space=pl.ANY),
            scratch_shapes=[pltpu.VMEM((tm_e, d), h.dtype),
                            pltpu.SemaphoreType.DMA(()),
                            pltpu.SemaphoreType.DMA(())]),
        out_shape=jax.ShapeDtypeStruct((nblk * tm_e, d), h.dtype),
        compiler_params=_cparams(("arbitrary",)),
        name="moe_dispatch",
    )(eid, rank, p_start, pad0, npad, nb_used, h)


def _experts_body(be_ref, nb_ref, first_ref, nxt_ref, slot_ref, x_ref, w1_hbm, w3_hbm, w2_hbm, y_ref,
                  wb1, wb3, wb2, sem, *, layer):
    b = pl.program_id(0)
    used = b < nb_ref[0]
    s = slot_ref[b]

    def weight_copies(e, slot):
        return (pltpu.make_async_copy(w1_hbm.at[layer, e], wb1.at[slot], sem.at[slot, 0]),
                pltpu.make_async_copy(w3_hbm.at[layer, e], wb3.at[slot], sem.at[slot, 1]),
                pltpu.make_async_copy(w2_hbm.at[layer, e], wb2.at[slot], sem.at[slot, 2]))

    @pl.when(b == 0)
    def _():
        for cp in weight_copies(be_ref[0], 0):
            cp.start()

    @pl.when(jnp.logical_and(used, first_ref[b] == 1))
    def _():
        for cp in weight_copies(be_ref[b], s):
            cp.wait()

        @pl.when(nxt_ref[b] >= 0)
        def _():
            for cp in weight_copies(nxt_ref[b], 1 - s):
                cp.start()

    @pl.when(used)
    def _():
        x = _unpack_halves(x_ref[...])
        a = _dot(x, wb1[s])
        g = _dot(x, wb3[s])
        y_ref[...] = _pack_halves(_dot(_silu(a) * g, wb2[s]))

    @pl.when(jnp.logical_not(used))
    def _():
        y_ref[...] = jnp.zeros(y_ref.shape, y_ref.dtype)


def _experts(xs, block_e, nb_used, first, nxt_e, slot, w1, w3, w2, layer, tm):
    s, pw = xs.shape
    d, f = w1.shape[-2], w1.shape[-1]
    nblk = s // tm
    return pl.pallas_call(
        functools.partial(_experts_body, layer=layer),
        grid_spec=pltpu.PrefetchScalarGridSpec(
            num_scalar_prefetch=5,
            grid=(nblk,),
            in_specs=[pl.BlockSpec((tm, pw), lambda b, be, nb, *_: (jnp.minimum(b, nb[0] - 1), 0)),
                      pl.BlockSpec(memory_space=pl.ANY),
                      pl.BlockSpec(memory_space=pl.ANY),
                      pl.BlockSpec(memory_space=pl.ANY)],
            out_specs=pl.BlockSpec((tm, pw), lambda b, *_: (b, 0)),
            scratch_shapes=[pltpu.VMEM((2, d, f), F32),
                            pltpu.VMEM((2, d, f), F32),
                            pltpu.VMEM((2, f, d), F32),
                            pltpu.SemaphoreType.DMA((2, 3))]),
        out_shape=jax.ShapeDtypeStruct((s, pw), xs.dtype),
        compiler_params=_cparams(("arbitrary",)),
        name="moe_experts",
    )(block_e, nb_used, first, nxt_e, slot, xs, w1, w3, w2)


def _combine_body(eid_ref, rank_ref, pstart_ref, x_ref, meta_ref, gfin_ref, ys_ref, *rest, tm, n_first, final):
    if final:
        o_a_ref, o_b_ref, buf, sem = rest
    else:
        o_a_ref, buf, sem = rest
        o_b_ref = None
    i = pl.program_id(0)
    nsteps = pl.num_programs(0)

    def issue(step, slot):
        for r in range(tm):
            for kk in range(2):
                d = _slot_of(eid_ref, rank_ref, pstart_ref, kk * (nsteps * tm) + step * tm + r)
                pltpu.make_async_copy(ys_ref.at[pl.ds(d, 1), :], buf.at[slot, kk, pl.ds(r, 1), :],
                                      sem.at[slot]).start(priority=(r + kk) % 2)

    @pl.when(i == 0)
    def _():
        issue(0, 0)

    @pl.when(i + 1 < nsteps)
    def _():
        issue(i + 1, (i + 1) % 2)

    slot = i % 2
    for _ in range(2 * tm):
        pltpu.make_async_copy(ys_ref.at[pl.ds(0, 1), :], buf.at[slot, 0, pl.ds(0, 1), :], sem.at[slot]).wait()

    meta = meta_ref[...]
    g0 = meta[:, _META_G0:_META_G0 + 1]
    g1 = meta[:, _META_G1:_META_G1 + 1]
    out = x_ref[...] + (_unpack_halves(buf[slot, 0]) * g0 + _unpack_halves(buf[slot, 1]) * g1)
    if not final:
        o_a_ref[...] = out
    else:
        out = _rms(out, gfin_ref[...])

        @pl.when(i < n_first)
        def _():
            o_a_ref[...] = out

        @pl.when(i >= n_first)
        def _():
            o_b_ref[...] = out


def _combine(x, meta, eid, rank, p_start, ys, gfin, tm, n_first_rows, final):
    n, d = x.shape
    n_first = n_first_rows // tm
    if final:
        out_specs = [pl.BlockSpec((tm, d), lambda i, *_: (jnp.minimum(i, n_first - 1), 0)),
                     pl.BlockSpec((tm, d), lambda i, *_: (jnp.maximum(i - n_first, 0), 0))]
        out_shape = [jax.ShapeDtypeStruct((n_first_rows, d), F32),
                     jax.ShapeDtypeStruct((n - n_first_rows, d), F32)]
    else:
        out_specs = [pl.BlockSpec((tm, d), lambda i, *_: (i, 0))]
        out_shape = [jax.ShapeDtypeStruct((n, d), F32)]
    return pl.pallas_call(
        functools.partial(_combine_body, tm=tm, n_first=n_first, final=final),
        grid_spec=pltpu.PrefetchScalarGridSpec(
            num_scalar_prefetch=3,
            grid=(n // tm,),
            in_specs=[pl.BlockSpec((tm, d), lambda i, *_: (i, 0)),
                      pl.BlockSpec((tm, LANES), lambda i, *_: (i, 0)),
                      pl.BlockSpec((1, d), lambda i, *_: (0, 0)),
                      pl.BlockSpec(memory_space=pl.ANY)],
            out_specs=out_specs,
            scratch_shapes=[pltpu.VMEM((2, 2, tm, ys.shape[1]), ys.dtype),
                            pltpu.SemaphoreType.DMA((2,))]),
        out_shape=out_shape,
        compiler_params=_cparams(("arbitrary",)),
        name="moe_combine_final" if final else "moe_combine",
    )(eid, rank, p_start, x, meta, gfin.reshape(1, d), ys)


def _moe(x1, h, meta, idx, cnt, w1, w3, w2, layer, gfin, n_first_rows, final, tm_e=192, tm_t=256):
    n, d = h.shape
    idx = idx.astype(jnp.int32)
    eid = idx[:, 0:2, :].transpose(1, 0, 2).reshape(-1)
    rank = idx[:, 2:4, :].transpose(1, 0, 2).reshape(-1)
    counts = cnt[0, _EXP_LANE0:_EXP_LANE0 + N_EXPERTS].astype(jnp.int32)
    padded = (counts + tm_e - 1) // tm_e * tm_e
    p_end = jnp.cumsum(padded)
    p_start = (p_end - padded).astype(jnp.int32)
    s = n * 2
    nblk = (s + N_EXPERTS * (tm_e - 1) + tm_e - 1) // tm_e
    bidx = jnp.arange(nblk, dtype=jnp.int32)
    block_e = jnp.minimum(jnp.sum((p_end[None, :] <= (bidx * tm_e)[:, None]).astype(jnp.int32), axis=1),
                          N_EXPERTS - 1).astype(jnp.int32)
    nb_used = (p_end[-1:] // tm_e).astype(jnp.int32)
    prev_e = jnp.concatenate([jnp.full((1,), -1, jnp.int32), block_e[:-1]])
    first = ((bidx < nb_used[0]) & (block_e != prev_e)).astype(jnp.int32)
    slot = ((jnp.cumsum(first) - 1) % 2).astype(jnp.int32)
    eidx = jnp.arange(N_EXPERTS, dtype=jnp.int32)
    live = jnp.where(counts > 0, eidx, N_EXPERTS)
    next_live = jnp.min(jnp.where(eidx[None, :] > eidx[:, None], live[None, :], N_EXPERTS), axis=1)
    nxt_e = jnp.sum(jnp.where(block_e[:, None] == eidx[None, :], next_live[None, :], 0), axis=1)
    nxt_e = jnp.where(nxt_e < N_EXPERTS, nxt_e, -1).astype(jnp.int32)
    xs = _dispatch(h, eid, rank, p_start, (p_start + counts).astype(jnp.int32),
                   (padded - counts).astype(jnp.int32), nb_used, nblk, tm_t, tm_e)
    ys = _experts(xs, block_e, nb_used, first, nxt_e, slot, w1, w3, w2, layer, tm_e)
    return _combine(x1, meta, eid, rank, p_start, ys, gfin, tm_t, n_first_rows, final)


def _lambda_value(lam_ref, lambda_init):
    lam = lam_ref[...]
    s1 = jnp.sum(lam[0:1] * lam[1:2], axis=1, keepdims=True)
    s2 = jnp.sum(lam[2:3] * lam[3:4], axis=1, keepdims=True)
    return jnp.exp(s1) - jnp.exp(s2) + lambda_init


def _lane_tile(x, width):
    if width % LANES == 0:
        return jnp.concatenate([x] * (width // LANES), axis=1)
    return x[:, :width]


def _softmax_step(j, s, v, m_ref, l_ref, acc_ref, rows=slice(None)):
    m_prev = m_ref[j, rows]
    m_new = jnp.maximum(m_prev, jnp.max(s, axis=1, keepdims=True))
    alpha = jnp.exp(m_prev - m_new)
    p = jnp.exp(s - _lane_tile(m_new, s.shape[1]))
    l_ref[j, rows] = alpha * l_ref[j, rows] + jnp.sum(p, axis=1, keepdims=True)
    acc = acc_ref[j, rows]
    acc_ref[j, rows] = acc * _lane_tile(alpha, acc.shape[1]) + _dot(p, v)
    m_ref[j, rows] = m_new


def _attn_finish(lam_ref, g_ref, l_ref, acc_ref, lambda_init):
    lam = _lambda_value(lam_ref, lambda_init)
    width = acc_ref.shape[-1]
    o = acc_ref[0] / _lane_tile(l_ref[0], width) - lam * (acc_ref[1] / _lane_tile(l_ref[1], width))
    return _rms(o, g_ref[...]) * (1.0 - lambda_init)


def _alibi_slopes():
    slopes = [2.0 ** (-8.0 * (h + 1) / DIFF_HEADS) for h in range(DIFF_HEADS)]
    assert all(math.frexp(s)[0] == 0.5 for s in slopes)
    return slopes


def _attn_prompt_body(iq_ref, ik_ref, slope_ref, q_ref, k_ref, v_ref, kt_ref, lam_ref, g_ref,
                      o_ref, qa_ref, m_ref, l_ref, acc_ref, *, tq, lambda_init):
    hg = pl.program_id(1)
    t = pl.program_id(2)
    iq = iq_ref[t]
    ik = ik_ref[t]
    dh = DIFF_DH
    hw = 2 * dh
    slopes = [slope_ref[hg * ATTN_HEADS_PER_STEP + hh] for hh in range(ATTN_HEADS_PER_STEP)]

    @pl.when(ik == 0)
    def _():
        m_ref[...] = jnp.full(m_ref.shape, -jnp.inf, F32)
        l_ref[...] = jnp.zeros(l_ref.shape, F32)
        acc_ref[...] = jnp.zeros(acc_ref.shape, F32)
        qpos = iq * tq + lax.broadcasted_iota(jnp.int32, (tq, LANES), 0)
        lane = lax.broadcasted_iota(jnp.int32, (tq, LANES), 1)
        qa = (qpos // CHUNK).astype(F32)
        qb = (qpos % CHUNK).astype(F32)
        for hh, slope in enumerate(slopes):
            feat = jnp.where(lane == 0, qa * (-slope * CHUNK),
                             jnp.where(lane == 1, qb * (-slope), jnp.where(lane < 4, slope, 0.0)))
            for j in range(2):
                c0 = hh * hw + j * dh
                qa_ref[hh, j] = jnp.concatenate([q_ref[:, c0:c0 + dh] * (dh ** -0.5), feat], axis=1)

    kt = kt_ref[...]
    ts = tq // ATTN_ROW_SPLIT

    def keys(hh, j, nkeys):
        c0 = hh * hw + j * dh
        return jnp.concatenate([k_ref[0:nkeys, c0:c0 + dh], kt[:nkeys]], axis=1)

    @pl.when(ik < iq)
    def _():
        for hh in range(ATTN_HEADS_PER_STEP):
            v = v_ref[:, hh * hw:(hh + 1) * hw]
            for r in range(ATTN_ROW_SPLIT):
                rows = slice(r * ts, (r + 1) * ts)
                for j in range(2):
                    _softmax_step(j, _dot_nt(qa_ref[hh, j, rows], keys(hh, j, tq)), v,
                                  m_ref.at[hh], l_ref.at[hh], acc_ref.at[hh], rows)

    @pl.when(ik == iq)
    def _():
        outs = []
        for hh, slope in enumerate(slopes):
            for r in range(ATTN_ROW_SPLIT):
                rows = slice(r * ts, (r + 1) * ts)
                nkeys = (r + 1) * ts
                row = r * ts + lax.broadcasted_iota(jnp.int32, (ts, nkeys), 0)
                col = lax.broadcasted_iota(jnp.int32, (ts, nkeys), 1)
                fix = jnp.maximum(col - row, 0).astype(F32) * (-2.0 * slope)
                vis = (col // CHUNK) <= (row // CHUNK)
                v = v_ref[0:nkeys, hh * hw:(hh + 1) * hw]
                for j in range(2):
                    s = _dot_nt(qa_ref[hh, j, rows], keys(hh, j, nkeys))
                    _softmax_step(j, jnp.where(vis, s + fix, -jnp.inf), v,
                                  m_ref.at[hh], l_ref.at[hh], acc_ref.at[hh], rows)
            outs.append(_attn_finish(lam_ref, g_ref.at[hh], l_ref.at[hh], acc_ref.at[hh], lambda_init))
        o_ref[...] = _pack_groups(jnp.concatenate(outs, axis=1))


def _attn_prompt(q, k, v, nb, t, lam, g_subln, lambda_init, tq):
    hw = 2 * DIFF_DH
    assert t % tq == 0 and (tq // ATTN_ROW_SPLIT) % CHUNK == 0 and t // CHUNK <= 256
    nq = t // tq
    pairs = [(iq, ik) for iq in range(nq) for ik in range(iq + 1)]
    iq_tab = jnp.asarray(np.array([p[0] for p in pairs], np.int32))
    ik_tab = jnp.asarray(np.array([p[1] for p in pairs], np.int32))
    slopes = jnp.asarray(np.array(_alibi_slopes(), np.float32))
    pos = np.arange(t)
    ktab = np.zeros((t, LANES), np.float32)
    ktab[:, 0] = 1.0
    ktab[:, 1] = 1.0
    ktab[:, 2] = (pos // CHUNK) * CHUNK
    ktab[:, 3] = pos % CHUNK
    g3 = g_subln.reshape(DIFF_HEADS, 1, hw)
    hps = ATTN_HEADS_PER_STEP
    assert DIFF_HEADS % hps == 0
    bw = hps * hw
    assert bw % PACK_GROUP == 0
    op = _packed_row(bw, _pack_groups)
    return pl.pallas_call(
        functools.partial(_attn_prompt_body, tq=tq, lambda_init=lambda_init),
        grid_spec=pltpu.PrefetchScalarGridSpec(
            num_scalar_prefetch=3,
            grid=(nb, DIFF_HEADS // hps, len(pairs)),
            in_specs=[pl.BlockSpec((tq, bw), lambda b, h, p, iqt, ikt, sl: (b * nq + iqt[p], h)),
                      pl.BlockSpec((tq, bw), lambda b, h, p, iqt, ikt, sl: (b * nq + ikt[p], h)),
                      pl.BlockSpec((tq, bw), lambda b, h, p, iqt, ikt, sl: (b * nq + ikt[p], h)),
                      pl.BlockSpec((tq, LANES), lambda b, h, p, iqt, ikt, sl: (ikt[p], 0)),
                      pl.BlockSpec((4, DIFF_DH), lambda b, h, p, iqt, ikt, sl: (0, 0)),
                      pl.BlockSpec((hps, 1, hw), lambda b, h, p, iqt, ikt, sl: (h, 0, 0))],
            out_specs=pl.BlockSpec((tq, op.shape[1]), lambda b, h, p, iqt, ikt, sl: (b * nq + iqt[p], h)),
            scratch_shapes=[pltpu.VMEM((hps, 2, tq, hw), F32),
                            pltpu.VMEM((hps, 2, tq, LANES), F32),
                            pltpu.VMEM((hps, 2, tq, LANES), F32),
                            pltpu.VMEM((hps, 2, tq, hw), F32)]),
        out_shape=jax.ShapeDtypeStruct((nb * t, op.shape[1] * (DIFF_HEADS // hps)), op.dtype),
        compiler_params=_cparams(("arbitrary", "arbitrary", "arbitrary")),
        name="diff_attn_prompt",
    )(iq_tab, ik_tab, slopes, q, k, v, jnp.asarray(ktab), lam, g3)


def _attn_sample_body(q_ref, kc_hbm, vc_hbm, kn_ref, vn_ref, lam_ref, g_ref,
                      o_ref, kbuf, vbuf, sem, m_ref, l_ref, acc_ref, *, li, tk, past, lambda_init):
    b = pl.program_id(0)
    ik = pl.program_id(1)
    nb = pl.num_programs(0)
    nkb = pl.num_programs(1) - 1
    dh = DIFF_DH
    hw = 2 * dh
    tq = q_ref.shape[0]
    scale = dh ** -0.5

    def fetch_copies(f):
        fb, fk = f // nkb, f % nkb
        slot = f % 2
        cps = []
        for h in range(DIFF_HEADS):
            for hbm, buf, c in ((kc_hbm, kbuf, 0), (vc_hbm, vbuf, 1)):
                cps.append(pltpu.make_async_copy(hbm.at[li, fb, pl.ds(fk * tk, tk), h, :], buf.at[slot, h],
                                                 sem.at[slot, c]))
        return cps

    fcur = b * nkb + ik

    @pl.when(jnp.logical_and(b == 0, ik == 0))
    def _():
        for cp in fetch_copies(0):
            cp.start()

    @pl.when(ik < nkb)
    def _():
        @pl.when(fcur + 1 < nb * nkb)
        def _():
            for cp in fetch_copies(fcur + 1):
                cp.start()

        for cp in fetch_copies(fcur):
            cp.wait()

    @pl.when(ik == 0)
    def _():
        m_ref[...] = jnp.full(m_ref.shape, -jnp.inf, F32)
        l_ref[...] = jnp.zeros(l_ref.shape, F32)
        acc_ref[...] = jnp.zeros(acc_ref.shape, F32)

    def attend(h, k, v, k0, width):
        slope = _alibi_slopes()[h]
        q = q_ref[:, h * hw:(h + 1) * hw]
        qpos = past + lax.broadcasted_iota(jnp.int32, (tq, width), 0)
        kpos = k0 + lax.broadcasted_iota(jnp.int32, (tq, width), 1)
        bias = jnp.abs(qpos - kpos).astype(F32) * (-slope)
        vis = (kpos // CHUNK) <= (qpos // CHUNK)
        for j in range(2):
            s = _dot_nt(q[:, j * dh:(j + 1) * dh], k[:, j * dh:(j + 1) * dh]) * scale + bias
            s = jnp.where(vis, s, -jnp.inf)
            _softmax_step(j, s, v, m_ref.at[h], l_ref.at[h], acc_ref.at[h])

    @pl.when(ik < nkb)
    def _():
        slot = fcur % 2
        for h in range(DIFF_HEADS):
            attend(h, kbuf[slot, h], vbuf[slot, h], ik * tk, tk)

    @pl.when(ik == nkb)
    def _():
        outs = []
        for h in range(DIFF_HEADS):
            hs = slice(h * hw, (h + 1) * hw)
            attend(h, kn_ref[:, hs], vn_ref[:, hs], past, tq)
            outs.append(_attn_finish(lam_ref, g_ref.at[h], l_ref.at[h], acc_ref.at[h], lambda_init))
        o_ref[...] = _pack_groups(jnp.concatenate(outs, axis=1))


def _attn_sample(q, k, v, cache_k, cache_v, li, lam, g_subln, lambda_init, tk):
    n_rows = q.shape[0]
    _, nb, past, nh, hw = cache_k.shape
    aw = nh * hw
    tq = n_rows // nb
    tk = min(tk, past)
    assert past % tk == 0
    nkb = past // tk
    g3 = g_subln.reshape(nh, 1, hw)
    cache_spec = pl.BlockSpec(memory_space=pl.ANY)
    row_spec = pl.BlockSpec((tq, aw), lambda b, ik: (b, 0))
    op = _packed_row(aw, _pack_groups)
    return pl.pallas_call(
        functools.partial(_attn_sample_body, li=li, tk=tk, past=past, lambda_init=lambda_init),
        grid=(nb, nkb + 1),
        in_specs=[row_spec,
                  cache_spec, cache_spec,
                  row_spec,
                  row_spec,
                  pl.BlockSpec((4, DIFF_DH), lambda b, ik: (0, 0)),
                  pl.BlockSpec((nh, 1, hw), lambda b, ik: (0, 0, 0))],
        out_specs=pl.BlockSpec((tq, op.shape[1]), lambda b, ik: (b, 0)),
        scratch_shapes=[pltpu.VMEM((2, nh, tk, hw), F32),
                        pltpu.VMEM((2, nh, tk, hw), F32),
                        pltpu.SemaphoreType.DMA((2, 2)),
                        pltpu.VMEM((nh, 2, tq, LANES), F32),
                        pltpu.VMEM((nh, 2, tq, LANES), F32),
                        pltpu.VMEM((nh, 2, tq, hw), F32)],
        out_shape=jax.ShapeDtypeStruct((n_rows, op.shape[1]), op.dtype),
        compiler_params=_cparams(("arbitrary", "arbitrary")),
        name="diff_attn_sample",
    )(q, cache_k, cache_v, k, v, lam, g3)


def _pad_cols(a, width):
    return jnp.pad(a, ((0, 0), (0, width - a.shape[1])))


def _mixer_constants(dk, dv, w_gate_up, b_gate, g_gla, conv_w, conv_b, dt_bias, a_log, d_skip, g_ssd):
    C = REC_ROWS
    nh = GLA_HEADS
    inner = g_ssd.shape[0]
    n_ssd = inner // SSD_HEADDIM
    wup = jnp.pad(w_gate_up, ((0, LANES - w_gate_up.shape[0]), (0, 0)))
    tri = jnp.asarray(np.tril(np.ones((C, C), np.float32)), BF16)
    e64 = np.zeros((LANES, inner), np.float32)
    ec = np.zeros((LANES, n_ssd * C), np.float32)
    for hh in range(n_ssd):
        e64[hh, hh * SSD_HEADDIM:(hh + 1) * SSD_HEADDIM] = 1.0
        ec[hh, hh * C:(hh + 1) * C] = 1.0
    eye = np.tile(np.eye(C, dtype=np.float32), (1, n_ssd))
    caus = np.tile(np.tril(np.ones((C, C), np.float32)), (1, n_ssd))
    return [wup, b_gate.reshape(1, -1), g_gla.reshape(nh, dv), conv_w, conv_b.reshape(1, -1),
            _pad_cols(dt_bias.reshape(1, -1), LANES),
            _pad_cols(-jnp.exp(a_log.astype(F32)).reshape(1, -1), LANES),
            jnp.repeat(d_skip, SSD_HEADDIM).reshape(1, -1), g_ssd.reshape(1, -1),
            tri, jnp.asarray(e64, BF16), jnp.asarray(ec, BF16), jnp.asarray(eye), jnp.asarray(caus)]


def kernel(x_prompt, x_sample, state_gla, state_ssm, state_conv, cache_k, cache_v, norm_mix, norm_ffn, norm_final, w_in, w_gate_up, b_gate, g_gla, conv_w, conv_b, dt_bias, a_log, d_skip, g_ssd, w_out_mix, w_qkv, lam_q1, lam_k1, lam_q2, lam_k2, g_subln, w_o, router_group_w, router_group_b, router_expert_w, router_expert_b, w1, w3, w2):
    bp, tp, d = x_prompt.shape
    bs, ts, _ = x_sample.shape
    n_p, n_s = bp * tp, bs * ts
    n = n_p + n_s
    depth = norm_mix.shape[0]
    nh = GLA_HEADS
    dk, dv = state_gla.shape[-2], state_gla.shape[-1]
    qk_w, vw = nh * dk, nh * dv
    n_ssd = state_ssm.shape[2]
    inner = n_ssd * SSD_HEADDIM
    hpg = n_ssd // SSD_GROUPS
    gs = SSD_GROUPS * SSD_STATE
    cw = inner + 2 * gs
    rank = w_gate_up.shape[1]
    assert w_gate_up.shape[2] == qk_w and rank <= LANES and n_ssd <= LANES

    x = jnp.concatenate([x_prompt.reshape(n_p, d), x_sample.reshape(n_s, d)], axis=0)
    tm_row = next(tm for tm in (1408, 1056, 768, 256) if n % tm == 0)

    i_even = i_odd = 0
    gla_p, ssm_p, conv_p, gla_s, ssm_s, conv_s = [], [], [], [], [], []
    k_p, v_p, k_s, v_s = [], [], [], []
    y_p = y_s = None
    for layer in range(depth):
        final = layer == depth - 1
        if layer % 2 == 0:
            i = i_even
            i_even += 1
            offs = np.cumsum([0, qk_w, qk_w, vw, rank, vw, inner, cw, n_ssd])
            w_t = w_in[i].T
            seg = lambda j: w_t[offs[j]:offs[j + 1]]
            pad_rows = lambda a: jnp.pad(a, ((0, LANES - a.shape[0]), (0, 0)))
            w_cat = jnp.concatenate([seg(0), seg(1), seg(2), seg(4), seg(5), seg(6),
                                     pad_rows(seg(3)), pad_rows(seg(7))], axis=0).astype(BF16)
            proj = _norm_matmul(x, norm_mix[layer], w_cat, tm_row, 1280, w_is_transposed=True)
            consts = _mixer_constants(dk, dv, w_gate_up[i], b_gate[i], g_gla[i], conv_w[i], conv_b[i],
                                      dt_bias[i], a_log[i], d_skip[i], g_ssd[i])

            def to_group_state(sm):
                b_ = sm.shape[0]
                return sm.reshape(b_, SSD_GROUPS, hpg, SSD_STATE, SSD_HEADDIM).transpose(0, 1, 3, 2, 4) \
                         .reshape(b_, SSD_GROUPS, SSD_STATE, hpg * SSD_HEADDIM)

            def from_group_state(sg):
                b_ = sg.shape[0]
                return sg.reshape(b_, SSD_GROUPS, SSD_STATE, hpg, SSD_HEADDIM).transpose(0, 1, 3, 2, 4) \
                         .reshape(b_, n_ssd, SSD_STATE, SSD_HEADDIM)

            def pad_conv(cv):
                return jnp.pad(cv, ((0, 0), (CONV_PAD - cv.shape[1], 0), (0, 0)))

            mix_a, g_fin, s_fin, c_fin = _mixer_call(
                proj, 0, bp, tp, jnp.zeros((bp, nh, dv, dk), F32),
                jnp.zeros((bp, SSD_GROUPS, SSD_STATE, hpg * SSD_HEADDIM), F32),
                jnp.zeros((bp, CONV_PAD, cw), F32), consts, dk, dv)
            gla_p.append(g_fin.transpose(0, 1, 3, 2))
            ssm_p.append(from_group_state(s_fin))
            conv_p.append(c_fin[:, CONV_PAD - (SSD_CONV - 1):])
            mix_b, g_fin, s_fin, c_fin = _mixer_call(
                proj, n_p, bs, ts, state_gla[i].transpose(0, 1, 3, 2),
                to_group_state(state_ssm[i]), pad_conv(state_conv[i]), consts, dk, dv)
            gla_s.append(g_fin.transpose(0, 1, 3, 2))
            ssm_s.append(from_group_state(s_fin))
            conv_s.append(c_fin[:, CONV_PAD - (SSD_CONV - 1):])
            w_proj = w_out_mix[i]
        else:
            i = i_odd
            i_odd += 1
            lambda_init = 0.8 - 0.6 * math.exp(-0.3 * layer)
            tm_p = next(tm for tm in (1024, 512, 256) if n_p % tm == 0)
            w_qkv_b = w_qkv[i].astype(BF16)
            q_a, k_a, v_a = _norm_matmul(x, norm_mix[layer], w_qkv_b, tm_p, 1024, nrows=n_p, parts=3)
            q_b, k_b, v_b = _norm_matmul(x, norm_mix[layer], w_qkv_b, n_s, 1024, row0=n_p, parts=3)
            lam = jnp.stack([lam_q1[i], lam_k1[i], lam_q2[i], lam_k2[i]])
            mix_a = _attn_prompt(q_a, k_a, v_a, bp, tp, lam, g_subln[i], lambda_init, min(512, tp))
            mix_b = _attn_sample(q_b, k_b, v_b, cache_k, cache_v, i, lam, g_subln[i], lambda_init, 1024)
            k_p.append(k_a.reshape(bp, tp, DIFF_HEADS, 2 * DIFF_DH))
            v_p.append(v_a.reshape(bp, tp, DIFF_HEADS, 2 * DIFF_DH))
            k_s.append(k_b.reshape(bs, ts, DIFF_HEADS, 2 * DIFF_DH))
            v_s.append(v_b.reshape(bs, ts, DIFF_HEADS, 2 * DIFF_DH))
            w_proj = w_o[i]

        wr = _pad_cols(jnp.concatenate([router_group_w[layer], router_expert_w[layer]], axis=1), LANES)
        rbias = _pad_cols(jnp.concatenate([router_group_b[layer], router_expert_b[layer]]).reshape(1, -1), LANES)
        x1, h2, meta, idx, cnt = _proj_resid(mix_a, mix_b, w_proj, x, norm_ffn[layer], wr, rbias, 256)
        res = _moe(x1, h2, meta, idx, cnt, w1, w3, w2, layer, norm_final, n_p, final)
        if final:
            y_p, y_s = res
        else:
            x = res[0]

    return (y_p.reshape(bp, tp, d), y_s.reshape(bs, ts, d),
            jnp.stack(gla_p), jnp.stack(ssm_p), jnp.stack(conv_p), jnp.stack(k_p), jnp.stack(v_p),
            jnp.stack(gla_s), jnp.stack(ssm_s), jnp.stack(conv_s), jnp.stack(k_s), jnp.stack(v_s))
```
